```python
import math
import jax, jax.numpy as jnp
from jax import lax
import numpy as np

D_MODEL = 2048
BATCH = 1
SEQ = 16384
DEPTH = 1

N_META = 16
CHUNK = 64
FRONT_PAD = (-N_META) % CHUNK
CONV_K = 4
EPS = 1e-6

DN_QK_HEADS = 16
DN_V_HEADS = 32
DN_HEAD_DIM = 128
DN_GVA = DN_V_HEADS // DN_QK_HEADS
DN_KEY_DIM = DN_QK_HEADS * DN_HEAD_DIM
DN_VAL_DIM = DN_V_HEADS * DN_HEAD_DIM
DN_CONV_DIM = 2 * DN_KEY_DIM + DN_VAL_DIM

SSD_D_INNER = 2 * D_MODEL
SSD_HEAD_DIM = 64
SSD_HEADS = SSD_D_INNER // SSD_HEAD_DIM
SSD_GROUPS = 8
SSD_REP = SSD_HEADS // SSD_GROUPS
SSD_STATE = 128
SSD_GS = SSD_GROUPS * SSD_STATE
SSD_CONV_DIM = SSD_D_INNER + 2 * SSD_GS

IN_SIZES = (DN_CONV_DIM, DN_VAL_DIM, DN_V_HEADS, DN_V_HEADS,
            SSD_D_INNER, SSD_CONV_DIM, SSD_HEADS, D_MODEL, D_MODEL)
D_IN = sum(IN_SIZES)

N_EXPERTS = 32
TOP_K = 4
D_EXPERT = D_MODEL
SWIGLU_LIMIT = 7.0
SWIGLU_ALPHA = 1.702
MOE_BLOCK = 128

kernel_name = "hybrid_deltanet_ssd_moe_meta"


def rms_norm(x, w):
    xf = x.astype(jnp.float32)
    y = xf * lax.rsqrt(jnp.mean(xf * xf, axis=-1, keepdims=True) + EPS)
    return (y * w.astype(jnp.float32)).astype(x.dtype)


def l2norm(x):
    return x * lax.rsqrt(jnp.sum(x * x, axis=-1, keepdims=True) + EPS)


def causal_conv(x, w, b=None):
    y = lax.conv_general_dilated(x, w[:, None, :].astype(x.dtype), window_strides=(1,),
                                 padding=[(w.shape[0] - 1, 0)],
                                 dimension_numbers=('NWC', 'WIO', 'NWC'),
                                 feature_group_count=x.shape[-1])
    if b is not None:
        y = y + b.astype(y.dtype)
    return y


def front_pad(t):
    return jnp.pad(t, [(0, 0), (FRONT_PAD, 0)] + [(0, 0)] * (t.ndim - 2))


def gated_deltanet(qkv_raw, z, b_raw, a_raw, conv_w, a_log, dt_bias, norm_w):
    f32 = jnp.float32
    bsz, seq_len, _ = qkv_raw.shape
    qkv = jax.nn.silu(causal_conv(qkv_raw, conv_w)).astype(f32)
    q, k, v = jnp.split(qkv, [DN_KEY_DIM, 2 * DN_KEY_DIM], axis=-1)
    q = l2norm(q.reshape(bsz, seq_len, DN_QK_HEADS, DN_HEAD_DIM)) * (DN_HEAD_DIM ** -0.5)
    k = l2norm(k.reshape(bsz, seq_len, DN_QK_HEADS, DN_HEAD_DIM))
    q = jnp.repeat(q, DN_GVA, axis=2)
    k = jnp.repeat(k, DN_GVA, axis=2)
    v = v.reshape(bsz, seq_len, DN_V_HEADS, DN_HEAD_DIM)
    beta = jax.nn.sigmoid(b_raw.astype(f32))
    g = -jnp.exp(a_log.astype(f32)) * jax.nn.softplus(a_raw.astype(f32) + dt_bias.astype(f32))
    n_chunks = (seq_len + FRONT_PAD) // CHUNK

    def to_chunks(t):
        t = front_pad(t).reshape(bsz, n_chunks, CHUNK, *t.shape[2:])
        return jnp.moveaxis(t, 2, 3)

    qc, kc, vc = to_chunks(q), to_chunks(k), to_chunks(v)
    bc, gc = to_chunks(beta), to_chunks(g)
    g_cum = jnp.cumsum(gc, axis=-1)
    causal = jnp.tril(jnp.ones((CHUNK, CHUNK), dtype=bool))
    strict = jnp.tril(jnp.ones((CHUNK, CHUNK), dtype=bool), -1)
    decay = jnp.exp(jnp.where(causal, g_cum[..., :, None] - g_cum[..., None, :], -jnp.inf))
    kk = jnp.einsum('bnhid,bnhjd->bnhij', kc, kc)
    a_strict = jnp.where(strict, bc[..., :, None] * kk * decay, 0.0)
    rhs = jnp.concatenate([vc * bc[..., None], kc * (bc * jnp.exp(g_cum))[..., None]], axis=-1)
    sol = lax.linalg.triangular_solve(a_strict, rhs, left_side=True, lower=True, unit_diagonal=True)
    u, w = jnp.split(sol, 2, axis=-1)
    qk = jnp.einsum('bnhid,bnhjd->bnhij', qc, kc) * decay
    q_dec = qc * jnp.exp(g_cum)[..., None]
    k_dec = kc * jnp.exp(g_cum[..., -1:] - g_cum)[..., None]
    g_last = jnp.exp(g_cum[..., -1])

    def step(state, inp):
        u_n, w_n, qk_n, qd_n, kd_n, gl_n = inp
        v_new = u_n - jnp.einsum('bhcd,bhde->bhce', w_n, state)
        o_n = jnp.einsum('bhcd,bhde->bhce', qd_n, state) + jnp.einsum('bhij,bhje->bhie', qk_n, v_new)
        state = state * gl_n[..., None, None] + jnp.einsum('bhcd,bhce->bhde', kd_n, v_new)
        return state, o_n

    s0 = jnp.zeros((bsz, DN_V_HEADS, DN_HEAD_DIM, DN_HEAD_DIM), f32)
    xs = tuple(jnp.moveaxis(t, 1, 0) for t in (u, w, qk, q_dec, k_dec, g_last))
    _, o = lax.scan(step, s0, xs)
    o = jnp.transpose(o, (1, 0, 3, 2, 4)).reshape(bsz, n_chunks * CHUNK, DN_V_HEADS, DN_HEAD_DIM)
    o = o[:, FRONT_PAD:]
    zf = z.astype(f32).reshape(bsz, seq_len, DN_V_HEADS, DN_HEAD_DIM)
    o = o * lax.rsqrt(jnp.mean(o * o, axis=-1, keepdims=True) + EPS) * norm_w.astype(f32) * jax.nn.silu(zf)
    return o.reshape(bsz, seq_len, DN_VAL_DIM).astype(z.dtype)


def mamba2_ssd(xbc_raw, z, dt_raw, conv_w, conv_b, a_log, dt_bias, d_skip, norm_w):
    f32 = jnp.float32
    bsz, seq_len, _ = xbc_raw.shape
    xbc = jax.nn.silu(causal_conv(xbc_raw, conv_w, conv_b)).astype(f32)
    xin, bm, cm = jnp.split(xbc, [SSD_D_INNER, SSD_D_INNER + SSD_GS], axis=-1)
    xh = xin.reshape(bsz, seq_len, SSD_HEADS, SSD_HEAD_DIM)
    bm = bm.reshape(bsz, seq_len, SSD_GROUPS, SSD_STATE)
    cm = cm.reshape(bsz, seq_len, SSD_GROUPS, SSD_STATE)
    dt = jax.nn.softplus(dt_raw.astype(f32) + dt_bias.astype(f32))
    la = -jnp.exp(a_log.astype(f32)) * dt
    xdt = xh * dt[..., None]
    n_chunks = (seq_len + FRONT_PAD) // CHUNK

    def to_chunks(t):
        return front_pad(t).reshape(bsz, n_chunks, CHUNK, *t.shape[2:])

    xdt_c = to_chunks(xdt).reshape(bsz, n_chunks, CHUNK, SSD_GROUPS, SSD_REP, SSD_HEAD_DIM)
    la_c = to_chunks(la).reshape(bsz, n_chunks, CHUNK, SSD_GROUPS, SSD_REP)
    b_c, c_c = to_chunks(bm), to_chunks(cm)
    a_cum = jnp.cumsum(la_c, axis=2)
    causal = jnp.tril(jnp.ones((CHUNK, CHUNK), dtype=bool))[:, :, None, None]
    seg = jnp.exp(jnp.where(causal, a_cum[:, :, :, None] - a_cum[:, :, None, :], -jnp.inf))
    cb = jnp.einsum('bnlgk,bnsgk->bnlsg', c_c, b_c)
    y_diag = jnp.einsum('bnlsgr,bnsgrp->bnlgrp', cb[..., None] * seg, xdt_c)
    xd_c = xdt_c * jnp.exp(a_cum[:, :, -1:] - a_cum)[..., None]

    def step(hs, inp):
        b_n, c_n, xd_n, ac_n = inp
        y_off = jnp.einsum('blgk,bgrpk->blgrp', c_n, hs) * jnp.exp(ac_n)[..., None]
        hs = hs * jnp.exp(ac_n[:, -1])[..., None, None] + jnp.einsum('bsgk,bsgrp->bgrpk', b_n, xd_n)
        return hs, y_off

    h0 = jnp.zeros((bsz, SSD_GROUPS, SSD_REP, SSD_HEAD_DIM, SSD_STATE), f32)
    _, y_off = lax.scan(step, h0, tuple(jnp.moveaxis(t, 1, 0) for t in (b_c, c_c, xd_c, a_cum)))
    y = y_diag + jnp.moveaxis(y_off, 0, 1)
    y = y.reshape(bsz, n_chunks * CHUNK, SSD_HEADS, SSD_HEAD_DIM)[:, FRONT_PAD:]
    y = y + d_skip.astype(f32)[:, None] * xh
    y = y.reshape(bsz, seq_len, SSD_D_INNER) * jax.nn.silu(z.astype(f32))
    y = y.reshape(bsz, seq_len, SSD_GROUPS, SSD_D_INNER // SSD_GROUPS)
    y = y * lax.rsqrt(jnp.mean(y * y, axis=-1, keepdims=True) + EPS)
    return (y.reshape(bsz, seq_len, SSD_D_INNER) * norm_w.astype(f32)).astype(z.dtype)


def hybrid_mixer(h, w_in, dn_conv_w, dn_a_log, dn_dt_bias, dn_norm_w,
                 ssd_conv_w, ssd_conv_b, ssd_a_log, ssd_dt_bias, ssd_d, ssd_norm_w,
                 w_proj_dn, w_proj_ssd, w_out):
    offsets = []
    acc = 0
    for s in IN_SIZES[:-1]:
        acc += s
        offsets.append(acc)
    w_parts = jnp.split(w_in, offsets, axis=1)
    dn_qkv, dn_z, dn_b, dn_a, ssd_z, ssd_xbc, ssd_dt, gate_dn, gate_ssd = [h @ wp for wp in w_parts]
    y_dn = gated_deltanet(dn_qkv, dn_z, dn_b, dn_a, dn_conv_w, dn_a_log, dn_dt_bias, dn_norm_w)
    y_ssd = mamba2_ssd(ssd_xbc, ssd_z, ssd_dt, ssd_conv_w, ssd_conv_b, ssd_a_log, ssd_dt_bias,
                       ssd_d, ssd_norm_w)
    merged = (jax.nn.sigmoid(gate_dn) * (y_dn @ w_proj_dn)
              + jax.nn.sigmoid(gate_ssd) * (y_ssd @ w_proj_ssd))
    return merged @ w_out


def routed_moe(h, w_router, b_router, w_gate, b_gate, w_up, b_up, w_down, b_down):
    bsz, seq_len, d = h.shape
    n_tok = bsz * seq_len
    xt = h.reshape(n_tok, d)
    logits = (xt @ w_router + b_router).astype(jnp.float32)
    top_val, top_idx = lax.top_k(logits, TOP_K)
    probs = jax.nn.softmax(top_val, axis=-1)
    n_assign = n_tok * TOP_K
    flat_e = top_idx.reshape(-1)
    flat_tok = jnp.repeat(jnp.arange(n_tok, dtype=jnp.int32), TOP_K)
    flat_p = probs.reshape(-1)
    order = jnp.argsort(flat_e)
    sorted_e = flat_e[order]
    counts = jnp.bincount(flat_e, length=N_EXPERTS)
    padded = (counts + MOE_BLOCK - 1) // MOE_BLOCK * MOE_BLOCK
    pad_end = jnp.cumsum(padded)
    pad_start = pad_end - padded
    grp_start = jnp.cumsum(counts) - counts
    dest = pad_start[sorted_e] + jnp.arange(n_assign) - grp_start[sorted_e]
    n_blocks = -(-(n_assign + N_EXPERTS * (MOE_BLOCK - 1)) // MOE_BLOCK)
    n_rows = n_blocks * MOE_BLOCK
    row_tok = jnp.full((n_rows,), n_tok, jnp.int32).at[dest].set(flat_tok[order])
    row_p = jnp.zeros((n_rows,), jnp.float32).at[dest].set(flat_p[order])
    block_e = jnp.minimum(jnp.searchsorted(pad_end, jnp.arange(n_blocks) * MOE_BLOCK, side='right'),
                          N_EXPERTS - 1)
    x_ext = jnp.concatenate([xt, jnp.zeros((1, d), xt.dtype)], axis=0)

    def expert_block(args):
        tok, e = args
        xb = x_ext[tok]
        gt = jnp.minimum(xb @ w_gate[e] + b_gate[e], SWIGLU_LIMIT)
        up = jnp.clip(xb @ w_up[e] + b_up[e], -SWIGLU_LIMIT, SWIGLU_LIMIT)
        act = gt * jax.nn.sigmoid(SWIGLU_ALPHA * gt) * (up + 1.0)
        return act @ w_down[e] + b_down[e]

    y_rows = lax.map(expert_block, (row_tok.reshape(n_blocks, MOE_BLOCK), block_e))
    y_rows = y_rows.reshape(n_rows, d) * row_p[:, None].astype(y_rows.dtype)
    y = jax.ops.segment_sum(y_rows, row_tok, num_segments=n_tok + 1)[:n_tok]
    return y.reshape(bsz, seq_len, d)


def setup_inputs(seed: int = 0) -> dict:
    key = jax.random.key(seed)
    ks = jax.random.split(key, 32)
    f32 = jnp.float32

    def nrm(k, shape, fan_in):
        return jax.random.normal(k, shape, f32) * (fan_in ** -0.5)

    def gain(k, shape):
        return 1.0 + 0.02 * jax.random.normal(k, shape, f32)

    def small(k, shape):
        return 0.01 * jax.random.normal(k, shape, f32)

    def dt_bias_init(k, shape):
        dt = jnp.exp(jax.random.uniform(k, shape, f32, math.log(1e-3), math.log(1e-1)))
        return dt + jnp.log(-jnp.expm1(-dt))

    def a_log_init(k, shape):
        return jnp.log(jax.random.uniform(k, shape, f32, 1.0, 16.0))

    return {
        'x': jax.random.normal(ks[0], (BATCH, SEQ, D_MODEL), f32),
        'meta_tokens': jax.random.normal(ks[1], (N_META, D_MODEL), f32),
        'norm_mix_w': gain(ks[2], (DEPTH, D_MODEL)),
        'w_in': nrm(ks[3], (DEPTH, D_MODEL, D_IN), D_MODEL),
        'dn_conv_w': nrm(ks[4], (DEPTH, CONV_K, DN_CONV_DIM), CONV_K),
        'dn_a_log': a_log_init(ks[5], (DEPTH, DN_V_HEADS)),
        'dn_dt_bias': dt_bias_init(ks[6], (DEPTH, DN_V_HEADS)),
        'dn_norm_w': gain(ks[7], (DEPTH, DN_HEAD_DIM)),
        'ssd_conv_w': nrm(ks[8], (DEPTH, CONV_K, SSD_CONV_DIM), CONV_K),
        'ssd_conv_b': small(ks[9], (DEPTH, SSD_CONV_DIM)),
        'ssd_a_log': a_log_init(ks[10], (DEPTH, SSD_HEADS)),
        'ssd_dt_bias': dt_bias_init(ks[11], (DEPTH, SSD_HEADS)),
        'ssd_d': gain(ks[12], (DEPTH, SSD_HEADS)),
        'ssd_norm_w': gain(ks[13], (DEPTH, SSD_D_INNER)),
        'w_proj_dn': nrm(ks[14], (DEPTH, DN_VAL_DIM, D_MODEL), DN_VAL_DIM),
        'w_proj_ssd': nrm(ks[15], (DEPTH, SSD_D_INNER, D_MODEL), SSD_D_INNER),
        'w_out': nrm(ks[16], (DEPTH, D_MODEL, D_MODEL), D_MODEL),
        'norm_ffn_w': gain(ks[17], (DEPTH, D_MODEL)),
        'w_router': nrm(ks[18], (DEPTH, D_MODEL, N_EXPERTS), D_MODEL),
        'b_router': small(ks[19], (DEPTH, N_EXPERTS)),
        'w_gate': nrm(ks[20], (DEPTH, N_EXPERTS, D_MODEL, D_EXPERT), D_MODEL),
        'b_gate': small(ks[21], (DEPTH, N_EXPERTS, D_EXPERT)),
        'w_up': nrm(ks[22], (DEPTH, N_EXPERTS, D_MODEL, D_EXPERT), D_MODEL),
        'b_up': small(ks[23], (DEPTH, N_EXPERTS, D_EXPERT)),
        'w_down': nrm(ks[24], (DEPTH, N_EXPERTS, D_EXPERT, D_MODEL), D_EXPERT),
        'b_down': small(ks[25], (DEPTH, N_EXPERTS, D_MODEL)),
        'final_norm_w': gain(ks[26], (D_MODEL,)),
    }


def reference(x, meta_tokens, norm_mix_w, w_in, dn_conv_w, dn_a_log, dn_dt_bias, dn_norm_w,
              ssd_conv_w, ssd_conv_b, ssd_a_log, ssd_dt_bias, ssd_d, ssd_norm_w,
              w_proj_dn, w_proj_ssd, w_out, norm_ffn_w, w_router, b_router,
              w_gate, b_gate, w_up, b_up, w_down, b_down, final_norm_w):
    bsz = x.shape[0]
    meta = jnp.broadcast_to(meta_tokens.astype(x.dtype)[None], (bsz, N_META, D_MODEL))
    h = jnp.concatenate([meta, x], axis=1)
    for layer in range(DEPTH):
        hn = rms_norm(h, norm_mix_w[layer])
        h = h + hybrid_mixer(hn, w_in[layer], dn_conv_w[layer], dn_a_log[layer], dn_dt_bias[layer],
                             dn_norm_w[layer], ssd_conv_w[layer], ssd_conv_b[layer], ssd_a_log[layer],
                             ssd_dt_bias[layer], ssd_d[layer], ssd_norm_w[layer],
                             w_proj_dn[layer], w_proj_ssd[layer], w_out[layer])
        hn = rms_norm(h, norm_ffn_w[layer])
        h = h + routed_moe(hn, w_router[layer], b_router[layer], w_gate[layer], b_gate[layer],
                           w_up[layer], b_up[layer], w_down[layer], b_down[layer])
    return rms_norm(h, final_norm_w)[:, N_META:]
```

```python
import functools

import jax
import jax.numpy as jnp
from jax import lax
from jax.experimental import pallas as pl
from jax.experimental.pallas import tpu as pltpu

F32 = jnp.float32
BF16 = jnp.bfloat16

D_MODEL = 2048
N_META = 16
CHUNK = 64
EPS = 1e-6

DN_QK_HEADS = 16
DN_V_HEADS = 32
DN_HEAD_DIM = 128
DN_KEY_DIM = DN_QK_HEADS * DN_HEAD_DIM
DN_VAL_DIM = DN_V_HEADS * DN_HEAD_DIM
DN_CONV_DIM = 2 * DN_KEY_DIM + DN_VAL_DIM

SSD_D_INNER = 2 * D_MODEL
SSD_HEAD_DIM = 64
SSD_HEADS = SSD_D_INNER // SSD_HEAD_DIM
SSD_GROUPS = 8
SSD_REP = SSD_HEADS // SSD_GROUPS
SSD_STATE = 128
SSD_GS = SSD_GROUPS * SSD_STATE
SSD_CONV_DIM = SSD_D_INNER + 2 * SSD_GS
SSD_GROUP_DIM = SSD_D_INNER // SSD_GROUPS

N_EXPERTS = 32
TOP_K = 4
SWIGLU_LIMIT = 7.0
SWIGLU_ALPHA = 1.702

TB = 512
FRONT = TB
CPB = TB // CHUNK
MOE_BM = 256
MOE_TN = 512
COMB_TT = 256
ROUTER_RT = 256
VMEM_LIMIT = 56 * 1024 * 1024


def _cparams(sem):
    return pltpu.CompilerParams(dimension_semantics=sem, vmem_limit_bytes=VMEM_LIMIT)


def _bdot(a, b):
    return jnp.dot(a.astype(BF16), b.astype(BF16), preferred_element_type=F32)


def _bdot_nt(a, b):
    return lax.dot_general(a.astype(BF16), b.astype(BF16), (((1,), (1,)), ((), ())),
                           preferred_element_type=F32)


def _bdot_tn(a, b):
    return lax.dot_general(a.astype(BF16), b.astype(BF16), (((0,), (0,)), ((), ())),
                           preferred_element_type=F32)


def _split3(x):
    x1 = x.astype(BF16)
    r1 = x - x1.astype(F32)
    x2 = r1.astype(BF16)
    x3 = (r1 - x2.astype(F32)).astype(BF16)
    return x1, x2, x3


def _dot_sel(x, m):
    x1, x2, x3 = _split3(x)
    d = lambda a: jnp.dot(a, m, preferred_element_type=F32)
    return d(x1) + d(x2) + d(x3)


def _dot3(a, b, nt=False):
    a1 = a.astype(BF16)
    a2 = (a - a1.astype(F32)).astype(BF16)
    b1 = b.astype(BF16)
    b2 = (b - b1.astype(F32)).astype(BF16)
    if nt:
        d = lambda p, q: lax.dot_general(p, q, (((1,), (1,)), ((), ())), preferred_element_type=F32)
    else:
        d = lambda p, q: jnp.dot(p, q, preferred_element_type=F32)
    return d(a1, b1) + d(a2, b1) + d(a1, b2)


def _sigmoid(x):
    return 1.0 / (1.0 + jnp.exp(-x))


def _silu(x):
    return x * _sigmoid(x)


def _softplus(x):
    return jnp.maximum(x, 0.0) + jnp.log(1.0 + jnp.exp(-jnp.abs(x)))


def _tile_lanes(p, n):
    return jnp.concatenate([p] * n, axis=1)


def _norm_small_kernel(h_ref, nw_ref, ws_ref, hn_ref, sm_ref):
    h = h_ref[...]
    y = h * lax.rsqrt(jnp.mean(h * h, axis=-1, keepdims=True) + EPS) * nw_ref[...]
    hn_ref[...] = y.astype(BF16)
    sm_ref[...] = _dot3(y, ws_ref[...])


def _norm_small(h, nw, w_small):
    rows = h.shape[0]
    return pl.pallas_call(
        _norm_small_kernel,
        grid=(rows // TB,),
        in_specs=[pl.BlockSpec((TB, D_MODEL), lambda i: (i, 0)),
                  pl.BlockSpec((1, D_MODEL), lambda i: (0, 0)),
                  pl.BlockSpec((D_MODEL, 128), lambda i: (0, 0))],
        out_specs=[pl.BlockSpec((TB, D_MODEL), lambda i: (i, 0)),
                   pl.BlockSpec((TB, 128), lambda i: (i, 0))],
        out_shape=[jax.ShapeDtypeStruct((rows, D_MODEL), BF16),
                   jax.ShapeDtypeStruct((rows, 128), F32)],
        compiler_params=_cparams(("arbitrary",)),
    )(h, nw, w_small)


def _mm_kernel(x_ref, w_ref, o_ref):
    o_ref[...] = jnp.dot(x_ref[...], w_ref[...], preferred_element_type=F32).astype(o_ref.dtype)


def _matmul(x, w, out_dtype, tm=TB, tn=1024):
    m, k = x.shape
    n = w.shape[1]
    return pl.pallas_call(
        _mm_kernel,
        grid=(n // tn, m // tm),
        in_specs=[pl.BlockSpec((tm, k), lambda j, i: (i, 0)),
                  pl.BlockSpec((k, tn), lambda j, i: (0, j))],
        out_specs=pl.BlockSpec((tm, tn), lambda j, i: (i, j)),
        out_shape=jax.ShapeDtypeStruct((m, n), out_dtype),
        compiler_params=_cparams(("arbitrary", "arbitrary")),
    )(x, w)


def _conv_silu(raw_ref, tail_ref, w_ref, b_ref=None):
    x = raw_ref[...].astype(F32)
    rows = x.shape[0]
    ext = jnp.concatenate([tail_ref[...], x], axis=0)
    w = w_ref[...]
    y = x * w[3:4]
    for j in (1, 2, 3):
        y = y + pltpu.roll(ext, j, 0)[8:8 + rows] * w[3 - j:4 - j]
    tail_ref[...] = x[rows - 8:rows]
    if b_ref is not None:
        y = y + b_ref[...]
    return _silu(y)


def _fill_chunk_matrices(u_ref, e_ref):
    jj = lax.broadcasted_iota(jnp.int32, (TB, TB), 0)
    ii = lax.broadcasted_iota(jnp.int32, (TB, TB), 1)
    same = (jj // CHUNK) == (ii // CHUNK)
    u_ref[...] = jnp.where(same & (jj <= ii), 1.0, 0.0).astype(BF16)
    e_ref[...] = jnp.where(same & (jj % CHUNK == CHUNK - 1), 1.0, 0.0).astype(BF16)


def _col_bcast(row, width):
    return jnp.broadcast_to(row, (width, TB)).T


def _unit_lower_inverse(a, masks, eye):
    t = eye - jnp.where(masks[0], a, 0.0)
    for m in masks[1:]:
        x = _bdot(jnp.where(m, a, 0.0), t)
        t = t - _bdot(t, x)
    return t


def _dn_kernel(q_ref, k_ref, v_ref, z_ref, b_ref, a_ref, wq_ref, wk_ref, wv_ref, alog_ref, dtb_ref, nw_ref,
               o_ref, s_ref, tq_ref, tk_ref, tv_ref, u_ref, e_ref):
    t = pl.program_id(1)

    @pl.when(t == 0)
    def _():
        s_ref[...] = jnp.zeros_like(s_ref)
        tq_ref[...] = jnp.zeros_like(tq_ref)
        tk_ref[...] = jnp.zeros_like(tk_ref)
        tv_ref[...] = jnp.zeros_like(tv_ref)
        _fill_chunk_matrices(u_ref, e_ref)

    q = _conv_silu(q_ref, tq_ref, wq_ref)
    k = _conv_silu(k_ref, tk_ref, wk_ref)
    v = _conv_silu(v_ref, tv_ref, wv_ref)
    q = q * (lax.rsqrt(jnp.sum(q * q, axis=-1, keepdims=True) + EPS) * (DN_HEAD_DIM ** -0.5))
    k = k * lax.rsqrt(jnp.sum(k * k, axis=-1, keepdims=True) + EPS)

    beta = _sigmoid(b_ref[...])
    g = -jnp.exp(_tile_lanes(alog_ref[...], TB // 128)) * _softplus(a_ref[...] + _tile_lanes(dtb_ref[...], TB // 128))
    gc = _dot_sel(g, u_ref[...])
    gl = _dot_sel(gc, e_ref[...])
    eg = jnp.exp(gc)
    ek = jnp.exp(gl - gc)

    ii = lax.broadcasted_iota(jnp.int32, (CHUNK, CHUNK), 0)
    jj = lax.broadcasted_iota(jnp.int32, (CHUNK, CHUNK), 1)
    causal = ii >= jj
    strict = ii > jj
    eye = jnp.where(ii == jj, 1.0, 0.0)
    masks = [strict & ((ii >> 1) == (jj >> 1))]
    blk = 2
    while blk < CHUNK:
        sh = blk.bit_length() - 1
        masks.append(((ii >> (sh + 1)) == (jj >> (sh + 1))) & (((ii >> sh) & 1) == 1) & (((jj >> sh) & 1) == 0))
        blk *= 2

    kk = []
    qk = []
    for c in range(CPB):
        rows = slice(c * CHUNK, (c + 1) * CHUNK)
        kk.append(_bdot_nt(k[rows], k[rows]))
        qk.append(_bdot_nt(q[rows], k[rows]))

    nw = nw_ref[...]
    for hh in range(2):
        gc_cb = _col_bcast(gc[hh:hh + 1], DN_HEAD_DIM)
        beta_cb = _col_bcast(beta[hh:hh + 1], DN_HEAD_DIM)
        eg_cb = _col_bcast(eg[hh:hh + 1], DN_HEAD_DIM)
        ek_cb = _col_bcast(ek[hh:hh + 1], DN_HEAD_DIM)
        vh = v[:, hh * DN_HEAD_DIM:(hh + 1) * DN_HEAD_DIM]
        rhs_all = jnp.concatenate([vh * beta_cb, k * (beta_cb * eg_cb)], axis=1)
        qd_all = q * eg_cb
        kd_all = k * ek_cb
        state = s_ref[hh]
        outs = []
        for c in range(CPB):
            rows = slice(c * CHUNK, (c + 1) * CHUNK)
            diff = gc_cb[rows, :CHUNK] - gc[hh:hh + 1, c * CHUNK:(c + 1) * CHUNK]
            dec = jnp.where(causal, jnp.exp(jnp.where(causal, diff, 0.0)), 0.0)
            a = jnp.where(strict, beta_cb[rows, :CHUNK] * kk[c] * dec, 0.0)
            tinv = _unit_lower_inverse(a, masks, eye)
            uw = _bdot(tinv, rhs_all[rows])
            u = uw[:, :DN_HEAD_DIM]
            w = uw[:, DN_HEAD_DIM:]
            ps = _bdot(jnp.concatenate([w, qd_all[rows]], axis=0), state)
            v_new = u - ps[:CHUNK]
            outs.append(ps[CHUNK:] + _bdot(qk[c] * dec, v_new))
            gl_row = eg_cb[(c + 1) * CHUNK - 1:(c + 1) * CHUNK, :]
            state = state * gl_row + _bdot_tn(kd_all[rows], v_new)
        s_ref[hh] = state
        o = jnp.concatenate(outs, axis=0)
        zh = z_ref[:, hh * DN_HEAD_DIM:(hh + 1) * DN_HEAD_DIM].astype(F32)
        o = o * lax.rsqrt(jnp.mean(o * o, axis=-1, keepdims=True) + EPS) * nw * _silu(zh)
        o_ref[:, hh * DN_HEAD_DIM:(hh + 1) * DN_HEAD_DIM] = o.astype(o_ref.dtype)


def _gated_deltanet(qkv, z, b_rows, a_rows, conv_w, alog_b, dtb_b, norm_w):
    rows = qkv.shape[0]
    nqk = DN_QK_HEADS
    d = DN_HEAD_DIM
    return pl.pallas_call(
        _dn_kernel,
        grid=(nqk, rows // TB),
        in_specs=[
            pl.BlockSpec((TB, d), lambda j, t: (t, j)),
            pl.BlockSpec((TB, d), lambda j, t: (t, nqk + j)),
            pl.BlockSpec((TB, 2 * d), lambda j, t: (t, nqk + j)),
            pl.BlockSpec((TB, 2 * d), lambda j, t: (t, j)),
            pl.BlockSpec((None, 2, TB), lambda j, t: (j, 0, t)),
            pl.BlockSpec((None, 2, TB), lambda j, t: (j, 0, t)),
            pl.BlockSpec((4, d), lambda j, t: (0, j)),
            pl.BlockSpec((4, d), lambda j, t: (0, nqk + j)),
            pl.BlockSpec((4, 2 * d), lambda j, t: (0, nqk + j)),
            pl.BlockSpec((None, 2, 128), lambda j, t: (j, 0, 0)),
            pl.BlockSpec((None, 2, 128), lambda j, t: (j, 0, 0)),
            pl.BlockSpec((1, d), lambda j, t: (0, 0)),
        ],
        out_specs=pl.BlockSpec((TB, 2 * d), lambda j, t: (t, j)),
        out_shape=jax.ShapeDtypeStruct((rows, DN_VAL_DIM), BF16),
        scratch_shapes=[
            pltpu.VMEM((2, d, d), F32),
            pltpu.VMEM((8, d), F32),
            pltpu.VMEM((8, d), F32),
            pltpu.VMEM((8, 2 * d), F32),
            pltpu.VMEM((TB, TB), BF16),
            pltpu.VMEM((TB, TB), BF16),
        ],
        compiler_params=_cparams(("arbitrary", "arbitrary")),
    )(qkv, qkv, qkv, z, b_rows, a_rows, conv_w, conv_w, conv_w, alog_b, dtb_b, norm_w)


def _ssd_kernel(x_ref, b_ref, c_ref, z_ref, dt_ref, wx_ref, wb_ref, wc_ref, bx_ref, bb_ref, bc_ref,
                alog_ref, dtb_ref, dexp_ref, nw_ref, o_ref, h_ref, tx_ref, tb_ref, tc_ref, u_ref, e_ref):
    t = pl.program_id(1)

    @pl.when(t == 0)
    def _():
        h_ref[...] = jnp.zeros_like(h_ref)
        tx_ref[...] = jnp.zeros_like(tx_ref)
        tb_ref[...] = jnp.zeros_like(tb_ref)
        tc_ref[...] = jnp.zeros_like(tc_ref)
        _fill_chunk_matrices(u_ref, e_ref)

    xh = _conv_silu(x_ref, tx_ref, wx_ref, bx_ref)
    bm = _conv_silu(b_ref, tb_ref, wb_ref, bb_ref)
    cm = _conv_silu(c_ref, tc_ref, wc_ref, bc_ref)

    nrep = TB // 128
    dt = _softplus(dt_ref[...] + _tile_lanes(dtb_ref[...], nrep))
    pos = t * TB + lax.broadcasted_iota(jnp.int32, (SSD_REP, TB), 1)
    dt = jnp.where(pos >= FRONT - N_META, dt, 0.0)
    la = -jnp.exp(_tile_lanes(alog_ref[...], nrep)) * dt
    ac = _dot_sel(la, u_ref[...])
    al = _dot_sel(ac, e_ref[...])
    dtek = dt * jnp.exp(al - ac)

    def col_bcast_heads(rowsv):
        stacked = jnp.concatenate(
            [jnp.broadcast_to(rowsv[r:r + 1], (SSD_HEAD_DIM, TB)) for r in range(SSD_REP)], axis=0)
        return stacked.T

    ac_cb = col_bcast_heads(ac)
    ea_cb = jnp.exp(ac_cb)
    xd = xh * col_bcast_heads(dtek)

    ii = lax.broadcasted_iota(jnp.int32, (CHUNK, 2 * CHUNK), 0)
    jj = lax.broadcasted_iota(jnp.int32, (CHUNK, 2 * CHUNK), 1)
    causal2 = ii >= (jj & (CHUNK - 1))
    li = lax.broadcasted_iota(jnp.int32, (CHUNK, 2 * CHUNK), 1)
    lo_half = li < CHUNK

    state = h_ref[...]
    outs = []
    for c in range(CPB):
        rows = slice(c * CHUNK, (c + 1) * CHUNK)
        cc = cm[rows]
        bc = bm[rows]
        cb = _bdot_nt(cc, bc)
        cb2 = jnp.concatenate([cb, cb], axis=1)
        y_off = _bdot(cc, state) * ea_cb[rows]
        yd = []
        for m in range(SSD_REP // 2):
            lanes = slice(m * 128, (m + 1) * 128)
            rowv = jnp.concatenate([ac[2 * m:2 * m + 1, rows], ac[2 * m + 1:2 * m + 2, rows]], axis=1)
            dtrow = jnp.concatenate([dt[2 * m:2 * m + 1, rows], dt[2 * m + 1:2 * m + 2, rows]], axis=1)
            diff = ac_cb[rows, lanes] - rowv
            seg = jnp.where(causal2, jnp.exp(jnp.where(causal2, diff, 0.0)), 0.0)
            mm = cb2 * seg * dtrow
            x2 = xh[rows, lanes]
            rhs = jnp.concatenate([jnp.where(lo_half, x2, 0.0), jnp.where(lo_half, 0.0, x2)], axis=0)
            yd.append(_bdot(mm, rhs))
        outs.append(jnp.concatenate(yd, axis=1) + y_off)
        ea_last = ea_cb[(c + 1) * CHUNK - 1:(c + 1) * CHUNK, :]
        state = state * ea_last + _bdot_tn(bc, xd[rows])
    h_ref[...] = state

    y = jnp.concatenate(outs, axis=0) + dexp_ref[...] * xh
    y = y * _silu(z_ref[...].astype(F32))
    y = y * lax.rsqrt(jnp.mean(y * y, axis=-1, keepdims=True) + EPS) * nw_ref[...]
    o_ref[...] = y.astype(o_ref.dtype)


def _mamba2_ssd(xbc, z, dt_rows, conv_w, conv_b, alog_b, dtb_b, d_exp, norm_w):
    rows = xbc.shape[0]
    gd = SSD_GROUP_DIM
    nx = SSD_D_INNER // SSD_STATE
    return pl.pallas_call(
        _ssd_kernel,
        grid=(SSD_GROUPS, rows // TB),
        in_specs=[
            pl.BlockSpec((TB, gd), lambda g, t: (t, g)),
            pl.BlockSpec((TB, SSD_STATE), lambda g, t: (t, nx + g)),
            pl.BlockSpec((TB, SSD_STATE), lambda g, t: (t, nx + SSD_GROUPS + g)),
            pl.BlockSpec((TB, gd), lambda g, t: (t, g)),
            pl.BlockSpec((None, SSD_REP, TB), lambda g, t: (g, 0, t)),
            pl.BlockSpec((4, gd), lambda g, t: (0, g)),
            pl.BlockSpec((4, SSD_STATE), lambda g, t: (0, nx + g)),
            pl.BlockSpec((4, SSD_STATE), lambda g, t: (0, nx + SSD_GROUPS + g)),
            pl.BlockSpec((1, gd), lambda g, t: (0, g)),
            pl.BlockSpec((1, SSD_STATE), lambda g, t: (0, nx + g)),
            pl.BlockSpec((1, SSD_STATE), lambda g, t: (0, nx + SSD_GROUPS + g)),
            pl.BlockSpec((None, SSD_REP, 128), lambda g, t: (g, 0, 0)),
            pl.BlockSpec((None, SSD_REP, 128), lambda g, t: (g, 0, 0)),
            pl.BlockSpec((1, gd), lambda g, t: (0, g)),
            pl.BlockSpec((1, gd), lambda g, t: (0, g)),
        ],
        out_specs=pl.BlockSpec((TB, gd), lambda g, t: (t, g)),
        out_shape=jax.ShapeDtypeStruct((rows, SSD_D_INNER), BF16),
        scratch_shapes=[
            pltpu.VMEM((SSD_STATE, gd), F32),
            pltpu.VMEM((8, gd), F32),
            pltpu.VMEM((8, SSD_STATE), F32),
            pltpu.VMEM((8, SSD_STATE), F32),
            pltpu.VMEM((TB, TB), BF16),
            pltpu.VMEM((TB, TB), BF16),
        ],
        compiler_params=_cparams(("arbitrary", "arbitrary")),
    )(xbc, xbc, xbc, z, dt_rows, conv_w, conv_w, conv_w, conv_b, conv_b, conv_b, alog_b, dtb_b, d_exp, norm_w)


def _merge_kernel(ydn_ref, yssd_ref, gd_ref, gs_ref, wdn_ref, wssd_ref, o_ref):
    a = jnp.dot(ydn_ref[...], wdn_ref[...], preferred_element_type=F32)
    b = jnp.dot(yssd_ref[...], wssd_ref[...], preferred_element_type=F32)
    o = _sigmoid(gd_ref[...].astype(F32)) * a + _sigmoid(gs_ref[...].astype(F32)) * b
    o_ref[...] = o.astype(o_ref.dtype)


def _merge(y_dn, y_ssd, gates, w_dn, w_ssd, tn=512):
    ntile = (y_dn.shape[0] - FRONT) // TB
    off = FRONT // TB
    ng = D_MODEL // tn
    return pl.pallas_call(
        _merge_kernel,
        grid=(ng, ntile),
        in_specs=[
            pl.BlockSpec((TB, DN_VAL_DIM), lambda j, i: (i + off, 0)),
            pl.BlockSpec((TB, SSD_D_INNER), lambda j, i: (i + off, 0)),
            pl.BlockSpec((TB, tn), lambda j, i: (i + off, j)),
            pl.BlockSpec((TB, tn), lambda j, i: (i + off, ng + j)),
            pl.BlockSpec((DN_VAL_DIM, tn), lambda j, i: (0, j)),
            pl.BlockSpec((SSD_D_INNER, tn), lambda j, i: (0, j)),
        ],
        out_specs=pl.BlockSpec((TB, tn), lambda j, i: (i, j)),
        out_shape=jax.ShapeDtypeStruct((ntile * TB, D_MODEL), BF16),
        compiler_params=_cparams(("arbitrary", "arbitrary")),
    )(y_dn, y_ssd, gates, gates, w_dn, w_ssd)


def _out_router_kernel(m_ref, wo_ref, x_ref, nw_ref, wr_ref, br_ref,
                       h2_ref, hn_ref, idx_ref, p_ref, rank_ref, cnt_ref, carry_ref, us_ref):
    i = pl.program_id(0)
    rt = x_ref.shape[0]

    @pl.when(i == 0)
    def _():
        carry_ref[...] = jnp.zeros_like(carry_ref)
        jj = lax.broadcasted_iota(jnp.int32, (rt, rt), 0)
        ii = lax.broadcasted_iota(jnp.int32, (rt, rt), 1)
        us_ref[...] = jnp.where(jj < ii, 1.0, 0.0).astype(BF16)

    h2 = x_ref[...] + jnp.dot(m_ref[...], wo_ref[...], preferred_element_type=F32)
    h2_ref[...] = h2
    hn = h2 * lax.rsqrt(jnp.mean(h2 * h2, axis=-1, keepdims=True) + EPS) * nw_ref[...]
    hn_ref[...] = hn

    lg = _dot3(wr_ref[...], hn, nt=True) + br_ref[...]
    eio = lax.broadcasted_iota(jnp.int32, (N_EXPERTS, rt), 0)
    vals, hits, idxs = [], [], []
    cur = lg
    for _ in range(TOP_K):
        m = jnp.max(cur, axis=0, keepdims=True)
        ik = jnp.min(jnp.where(cur == m, eio, N_EXPERTS), axis=0, keepdims=True)
        hit = eio == ik
        vals.append(m)
        idxs.append(ik)
        hits.append(hit)
        cur = jnp.where(hit, -jnp.inf, cur)
    ex = [jnp.exp(vk - vals[0]) for vk in vals]
    den = ex[0] + ex[1] + ex[2] + ex[3]
    idx_ref[...] = jnp.concatenate(idxs, axis=0)
    p_ref[...] = jnp.concatenate([e / den for e in ex], axis=0)

    sel = jnp.zeros((N_EXPERTS, rt), F32)
    for hit in hits:
        sel = sel + jnp.where(hit, 1.0, 0.0)
    before = jnp.dot(sel.astype(BF16), us_ref[...], preferred_element_type=F32) \
        + _tile_lanes(carry_ref[...], rt // 128)
    ranks = [jnp.sum(jnp.where(hit, before, 0.0), axis=0, keepdims=True) for hit in hits]
    rank_ref[...] = jnp.concatenate(ranks, axis=0).astype(jnp.int32)
    carry = carry_ref[...] + jnp.dot(sel.astype(BF16), jnp.ones((rt, 128), BF16), preferred_element_type=F32)
    carry_ref[...] = carry
    cnt_ref[...] = carry


def _out_router(merged, w_out, x, nw, w_router_t, b_router_c):
    n_tok = x.shape[0]
    rt = ROUTER_RT
    tok_spec = pl.BlockSpec((TOP_K, rt), lambda i: (0, i))
    return pl.pallas_call(
        _out_router_kernel,
        grid=(n_tok // rt,),
        in_specs=[
            pl.BlockSpec((rt, D_MODEL), lambda i: (i, 0)),
            pl.BlockSpec((D_MODEL, D_MODEL), lambda i: (0, 0)),
            pl.BlockSpec((rt, D_MODEL), lambda i: (i, 0)),
            pl.BlockSpec((1, D_MODEL), lambda i: (0, 0)),
            pl.BlockSpec((N_EXPERTS, D_MODEL), lambda i: (0, 0)),
            pl.BlockSpec((N_EXPERTS, 1), lambda i: (0, 0)),
        ],
        out_specs=[
            pl.BlockSpec((rt, D_MODEL), lambda i: (i, 0)),
            pl.BlockSpec((rt, D_MODEL), lambda i: (i, 0)),
            tok_spec, tok_spec, tok_spec,
            pl.BlockSpec((N_EXPERTS, 128), lambda i: (0, 0)),
        ],
        out_shape=[
            jax.ShapeDtypeStruct((n_tok, D_MODEL), F32),
            jax.ShapeDtypeStruct((n_tok, D_MODEL), F32),
            jax.ShapeDtypeStruct((TOP_K, n_tok), jnp.int32),
            jax.ShapeDtypeStruct((TOP_K, n_tok), F32),
            jax.ShapeDtypeStruct((TOP_K, n_tok), jnp.int32),
            jax.ShapeDtypeStruct((N_EXPERTS, 128), F32),
        ],
        scratch_shapes=[pltpu.VMEM((N_EXPERTS, 128), F32), pltpu.VMEM((rt, rt), BF16)],
        compiler_params=_cparams(("arbitrary",)),
    )(merged, w_out, x, nw, w_router_t, b_router_c)


def _gather_kernel(nused_ref, tok_ref, src_ref, o_ref, buf_ref, sem):
    b = pl.program_id(0)

    @pl.when(b < nused_ref[0])
    def _():
        def issue(r, carry):
            pltpu.make_async_copy(src_ref.at[pl.ds(tok_ref[0, 0, r], 1)], buf_ref.at[pl.ds(r, 1)], sem).start()
            return carry

        lax.fori_loop(0, MOE_BM, issue, 0)

        def drain(r, carry):
            pltpu.make_async_copy(src_ref.at[pl.ds(0, 1)], buf_ref.at[pl.ds(0, 1)], sem).wait()
            return carry

        lax.fori_loop(0, MOE_BM, drain, 0)
        o_ref[...] = buf_ref[...].astype(o_ref.dtype)

    @pl.when(b >= nused_ref[0])
    def _():
        o_ref[...] = jnp.zeros_like(o_ref)


def _gather_rows(n_used, row_tok, src):
    nb = row_tok.shape[0]
    return pl.pallas_call(
        _gather_kernel,
        grid_spec=pltpu.PrefetchScalarGridSpec(
            num_scalar_prefetch=1,
            grid=(nb,),
            in_specs=[pl.BlockSpec((1, 1, MOE_BM), lambda b, nu: (b, 0, 0), memory_space=pltpu.SMEM),
                      pl.BlockSpec(memory_space=pl.ANY)],
            out_specs=pl.BlockSpec((MOE_BM, D_MODEL), lambda b, nu: (b, 0)),
            scratch_shapes=[pltpu.VMEM((MOE_BM, D_MODEL), F32), pltpu.SemaphoreType.DMA],
        ),
        out_shape=jax.ShapeDtypeStruct((nb * MOE_BM, D_MODEL), BF16),
        compiler_params=_cparams(("arbitrary",)),
    )(n_used, row_tok, src)


def _expert_changed(be_ref, b):
    return jnp.logical_or(b == 0, be_ref[b] != be_ref[jnp.maximum(b - 1, 0)])


def _expert_up_kernel(be_ref, nused_ref, x_ref, wg_ref, wu_ref, bg_ref, bu_ref, o_ref, wgb_ref, wub_ref):
    b = pl.program_id(1)

    @pl.when(b < nused_ref[0])
    def _():
        @pl.when(_expert_changed(be_ref, b))
        def _():
            wgb_ref[...] = wg_ref[...].astype(BF16)
            wub_ref[...] = wu_ref[...].astype(BF16)

        x = x_ref[...]
        gt = jnp.dot(x, wgb_ref[...], preferred_element_type=F32) + bg_ref[...]
        up = jnp.dot(x, wub_ref[...], preferred_element_type=F32) + bu_ref[...]
        gt = jnp.minimum(gt, SWIGLU_LIMIT)
        up = jnp.clip(up, -SWIGLU_LIMIT, SWIGLU_LIMIT)
        act = gt * _sigmoid(SWIGLU_ALPHA * gt) * (up + 1.0)
        o_ref[...] = act.astype(o_ref.dtype)

    @pl.when(b >= nused_ref[0])
    def _():
        o_ref[...] = jnp.zeros_like(o_ref)


def _expert_up(block_e, n_used, xs, w_gate, w_up, b_gate, b_up):
    nb = xs.shape[0] // MOE_BM
    tn = MOE_TN
    wspec = pl.BlockSpec((None, D_MODEL, tn), lambda j, b, be, nu: (be[b], 0, j))
    bspec = pl.BlockSpec((None, 1, tn), lambda j, b, be, nu: (be[b], 0, j))
    return pl.pallas_call(
        _expert_up_kernel,
        grid_spec=pltpu.PrefetchScalarGridSpec(
            num_scalar_prefetch=2,
            grid=(D_MODEL // tn, nb),
            in_specs=[pl.BlockSpec((MOE_BM, D_MODEL), lambda j, b, be, nu: (b, 0)), wspec, wspec, bspec, bspec],
            out_specs=pl.BlockSpec((MOE_BM, tn), lambda j, b, be, nu: (b, j)),
            scratch_shapes=[pltpu.VMEM((D_MODEL, tn), BF16), pltpu.VMEM((D_MODEL, tn), BF16)],
        ),
        out_shape=jax.ShapeDtypeStruct((nb * MOE_BM, D_MODEL), BF16),
        compiler_params=_cparams(("arbitrary", "arbitrary")),
    )(block_e, n_used, xs, w_gate, w_up, b_gate, b_up)


def _expert_down_kernel(be_ref, nused_ref, a_ref, wd_ref, bd_ref, o_ref, wdb_ref):
    b = pl.program_id(1)

    @pl.when(b < nused_ref[0])
    def _():
        @pl.when(_expert_changed(be_ref, b))
        def _():
            wdb_ref[...] = wd_ref[...].astype(BF16)

        o_ref[...] = jnp.dot(a_ref[...], wdb_ref[...], preferred_element_type=F32) + bd_ref[...]

    @pl.when(b >= nused_ref[0])
    def _():
        o_ref[...] = jnp.zeros_like(o_ref)


def _expert_down(block_e, n_used, act, w_down, b_down):
    nb = act.shape[0] // MOE_BM
    tn = MOE_TN
    return pl.pallas_call(
        _expert_down_kernel,
        grid_spec=pltpu.PrefetchScalarGridSpec(
            num_scalar_prefetch=2,
            grid=(D_MODEL // tn, nb),
            in_specs=[pl.BlockSpec((MOE_BM, D_MODEL), lambda j, b, be, nu: (b, 0)),
                      pl.BlockSpec((None, D_MODEL, tn), lambda j, b, be, nu: (be[b], 0, j)),
                      pl.BlockSpec((None, 1, tn), lambda j, b, be, nu: (be[b], 0, j))],
            out_specs=pl.BlockSpec((MOE_BM, tn), lambda j, b, be, nu: (b, j)),
            scratch_shapes=[pltpu.VMEM((D_MODEL, tn), BF16)],
        ),
        out_shape=jax.ShapeDtypeStruct((nb * MOE_BM, D_MODEL), F32),
        compiler_params=_cparams(("arbitrary", "arbitrary")),
    )(block_e, n_used, act, w_down, b_down)


def _combine_kernel(dest_ref, h2_ref, p_ref, fw_ref, y_ref, o_ref, buf_ref, sem):
    def issue(r, carry):
        for k in range(TOP_K):
            pltpu.make_async_copy(y_ref.at[pl.ds(dest_ref[0, k, r], 1)], buf_ref.at[k, pl.ds(r, 1)], sem).start()
        return carry

    lax.fori_loop(0, COMB_TT, issue, 0)

    def drain(r, carry):
        for k in range(TOP_K):
            pltpu.make_async_copy(y_ref.at[pl.ds(0, 1)], buf_ref.at[k, pl.ds(0, 1)], sem).wait()
        return carry

    lax.fori_loop(0, COMB_TT, drain, 0)
    p = p_ref[...]
    acc = h2_ref[...]
    for k in range(TOP_K):
        acc = acc + p[:, k:k + 1] * buf_ref[k]
    o_ref[...] = acc * lax.rsqrt(jnp.mean(acc * acc, axis=-1, keepdims=True) + EPS) * fw_ref[...]


def _combine(dest, h2, p_cols, final_w, y_rows):
    n_tok = h2.shape[0]
    tt = COMB_TT
    return pl.pallas_call(
        _combine_kernel,
        grid=(n_tok // tt,),
        in_specs=[
            pl.BlockSpec((1, TOP_K, tt), lambda i: (i, 0, 0), memory_space=pltpu.SMEM),
            pl.BlockSpec((tt, D_MODEL), lambda i: (i, 0)),
            pl.BlockSpec((tt, TOP_K), lambda i: (i, 0)),
            pl.BlockSpec((1, D_MODEL), lambda i: (0, 0)),
            pl.BlockSpec(memory_space=pl.ANY),
        ],
        out_specs=pl.BlockSpec((tt, D_MODEL), lambda i: (i, 0)),
        out_shape=jax.ShapeDtypeStruct((n_tok, D_MODEL), F32),
        scratch_shapes=[pltpu.VMEM((TOP_K, tt, D_MODEL), F32), pltpu.SemaphoreType.DMA],
        compiler_params=_cparams(("arbitrary",)),
    )(dest, h2, p_cols, final_w, y_rows)


def _mixer(h_ext, norm_w, w_in, dn_conv_w, dn_a_log, dn_dt_bias, dn_norm_w,
           ssd_conv_w, ssd_conv_b, ssd_a_log, ssd_dt_bias, ssd_d, ssd_norm_w, w_proj_dn, w_proj_ssd):
    o_qkv = 0
    o_dnz = o_qkv + DN_CONV_DIM
    o_b = o_dnz + DN_VAL_DIM
    o_a = o_b + DN_V_HEADS
    o_sz = o_a + DN_V_HEADS
    o_xbc = o_sz + SSD_D_INNER
    o_dt = o_xbc + SSD_CONV_DIM
    o_gate = o_dt + SSD_HEADS
    d_in = o_gate + 2 * D_MODEL
    w_small = jnp.concatenate([w_in[:, o_b:o_sz], w_in[:, o_dt:o_gate]], axis=1)

    hn, small = _norm_small(h_ext, norm_w.reshape(1, D_MODEL), w_small)
    qkv = _matmul(hn, w_in[:, o_qkv:o_dnz].astype(BF16), BF16)
    dn_z = _matmul(hn, w_in[:, o_dnz:o_b].astype(BF16), BF16)
    ssd_z = _matmul(hn, w_in[:, o_sz:o_xbc].astype(BF16), BF16)
    xbc = _matmul(hn, w_in[:, o_xbc:o_dt].astype(BF16), BF16)
    gates = _matmul(hn, w_in[:, o_gate:d_in].astype(BF16), BF16)

    rows = h_ext.shape[0]
    sm_t = small.T
    b_rows = sm_t[0:DN_V_HEADS].reshape(DN_QK_HEADS, 2, rows)
    a_rows = sm_t[DN_V_HEADS:2 * DN_V_HEADS].reshape(DN_QK_HEADS, 2, rows)
    dt_rows = sm_t[2 * DN_V_HEADS:].reshape(SSD_GROUPS, SSD_REP, rows)

    bc = lambda p, g, r: jnp.broadcast_to(p.astype(F32).reshape(g, r, 1), (g, r, 128))
    y_dn = _gated_deltanet(qkv, dn_z, b_rows, a_rows, dn_conv_w,
                           bc(dn_a_log, DN_QK_HEADS, 2), bc(dn_dt_bias, DN_QK_HEADS, 2),
                           dn_norm_w.reshape(1, DN_HEAD_DIM))
    y_ssd = _mamba2_ssd(xbc, ssd_z, dt_rows, ssd_conv_w, ssd_conv_b.reshape(1, SSD_CONV_DIM),
                        bc(ssd_a_log, SSD_GROUPS, SSD_REP), bc(ssd_dt_bias, SSD_GROUPS, SSD_REP),
                        jnp.repeat(ssd_d, SSD_HEAD_DIM).reshape(1, SSD_D_INNER),
                        ssd_norm_w.reshape(1, SSD_D_INNER))
    return _merge(y_dn, y_ssd, gates, w_proj_dn.astype(BF16), w_proj_ssd.astype(BF16))


def _moe_and_final(merged, x2, w_out, norm_ffn_w, w_router, b_router, w_gate, b_gate, w_up, b_up,
                   w_down, b_down, final_norm_w):
    n_tok = x2.shape[0]
    h2, hn2, idx, probs, rank, cnt = _out_router(
        merged, w_out.astype(BF16), x2, norm_ffn_w.reshape(1, D_MODEL), w_router.T,
        b_router.reshape(N_EXPERTS, 1))

    counts = cnt[:, 0].astype(jnp.int32)
    padded = (counts + MOE_BM - 1) // MOE_BM * MOE_BM
    pad_end = jnp.cumsum(padded)
    pad_start = pad_end - padded
    dest = pad_start[idx] + rank
    nb = -(-(n_tok * TOP_K + N_EXPERTS * (MOE_BM - 1)) // MOE_BM)
    tok_ids = jnp.broadcast_to(jnp.arange(n_tok, dtype=jnp.int32)[None], (TOP_K, n_tok))
    row_tok = jnp.zeros((nb * MOE_BM,), jnp.int32).at[dest.reshape(-1)].set(tok_ids.reshape(-1))
    block_e = jnp.minimum(jnp.searchsorted(pad_end, jnp.arange(nb, dtype=jnp.int32) * MOE_BM, side='right'),
                          N_EXPERTS - 1).astype(jnp.int32)
    n_used = (pad_end[-1:] // MOE_BM).astype(jnp.int32)

    xs = _gather_rows(n_used, row_tok.reshape(nb, 1, MOE_BM), hn2)
    act = _expert_up(block_e, n_used, xs, w_gate, w_up,
                     b_gate.reshape(N_EXPERTS, 1, D_MODEL), b_up.reshape(N_EXPERTS, 1, D_MODEL))
    y_rows = _expert_down(block_e, n_used, act, w_down, b_down.reshape(N_EXPERTS, 1, D_MODEL))
    dest_t = dest.reshape(TOP_K, n_tok // COMB_TT, COMB_TT).transpose(1, 0, 2)
    return _combine(dest_t, h2, probs.T, final_norm_w.reshape(1, D_MODEL), y_rows)


def kernel(x, meta_tokens, norm_mix_w, w_in, dn_conv_w, dn_a_log, dn_dt_bias, dn_norm_w, ssd_conv_w, ssd_conv_b, ssd_a_log, ssd_dt_bias, ssd_d, ssd_norm_w, w_proj_dn, w_proj_ssd, w_out, norm_ffn_w, w_router, b_router, w_gate, b_gate, w_up, b_up, w_down, b_down, final_norm_w):
    bsz, seq, d = x.shape
    assert bsz == 1 and d == D_MODEL and seq % TB == 0
    assert norm_mix_w.shape[0] == 1, "single-layer block"
    x2 = x.reshape(seq, d)
    h_ext = jnp.concatenate([jnp.zeros((FRONT - N_META, d), x.dtype), meta_tokens.astype(x.dtype), x2], axis=0)
    merged = _mixer(h_ext, norm_mix_w[0], w_in[0], dn_conv_w[0], dn_a_log[0], dn_dt_bias[0], dn_norm_w[0],
                    ssd_conv_w[0], ssd_conv_b[0], ssd_a_log[0], ssd_dt_bias[0], ssd_d[0], ssd_norm_w[0],
                    w_proj_dn[0], w_proj_ssd[0])
    out = _moe_and_final(merged, x2, w_out[0], norm_ffn_w[0], w_router[0], b_router[0], w_gate[0], b_gate[0],
                         w_up[0], b_up[0], w_down[0], b_down[0], final_norm_w)
    return out.reshape(bsz, seq, d)
```

```python
import functools

import jax
import jax.numpy as jnp
from jax import lax
from jax.experimental import pallas as pl
from jax.experimental.pallas import tpu as pltpu

F32 = jnp.float32
BF16 = jnp.bfloat16

D_MODEL = 2048
N_META = 16
CHUNK = 64
EPS = 1e-6

DN_QK_HEADS = 16
DN_V_HEADS = 32
DN_HEAD_DIM = 128
DN_KEY_DIM = DN_QK_HEADS * DN_HEAD_DIM
DN_VAL_DIM = DN_V_HEADS * DN_HEAD_DIM
DN_CONV_DIM = 2 * DN_KEY_DIM + DN_VAL_DIM

SSD_D_INNER = 2 * D_MODEL
SSD_HEAD_DIM = 64
SSD_HEADS = SSD_D_INNER // SSD_HEAD_DIM
SSD_GROUPS = 8
SSD_REP = SSD_HEADS // SSD_GROUPS
SSD_STATE = 128
SSD_GS = SSD_GROUPS * SSD_STATE
SSD_CONV_DIM = SSD_D_INNER + 2 * SSD_GS
SSD_GROUP_DIM = SSD_D_INNER // SSD_GROUPS

N_EXPERTS = 32
TOP_K = 4
SWIGLU_LIMIT = 7.0
SWIGLU_ALPHA = 1.702

TB = 512
FRONT = TB
CPB = TB // CHUNK
MOE_BM = 256
MOE_TN_UP = 1024
MOE_TN_DOWN = 2048
COMB_TT = 256
ROUTER_RT = 256
VMEM_LIMIT = 56 * 1024 * 1024


def _cparams(sem):
    return pltpu.CompilerParams(dimension_semantics=sem, vmem_limit_bytes=VMEM_LIMIT)


def _bdot(a, b):
    return jnp.dot(a.astype(BF16), b.astype(BF16), preferred_element_type=F32)


def _bdot_nt(a, b):
    return lax.dot_general(a.astype(BF16), b.astype(BF16), (((1,), (1,)), ((), ())),
                           preferred_element_type=F32)


def _bdot_tn(a, b):
    return lax.dot_general(a.astype(BF16), b.astype(BF16), (((0,), (0,)), ((), ())),
                           preferred_element_type=F32)


def _split3(x):
    x1 = x.astype(BF16)
    r1 = x - x1.astype(F32)
    x2 = r1.astype(BF16)
    x3 = (r1 - x2.astype(F32)).astype(BF16)
    return x1, x2, x3


def _dot_sel(x, m):
    x1, x2, x3 = _split3(x)
    d = lambda a: jnp.dot(a, m, preferred_element_type=F32)
    return d(x1) + d(x2) + d(x3)


def _dot3(a, b, nt=False):
    a1 = a.astype(BF16)
    a2 = (a - a1.astype(F32)).astype(BF16)
    b1 = b.astype(BF16)
    b2 = (b - b1.astype(F32)).astype(BF16)
    if nt:
        d = lambda p, q: lax.dot_general(p, q, (((1,), (1,)), ((), ())), preferred_element_type=F32)
    else:
        d = lambda p, q: jnp.dot(p, q, preferred_element_type=F32)
    return d(a1, b1) + d(a2, b1) + d(a1, b2)


def _sigmoid(x):
    return 1.0 / (1.0 + jnp.exp(-x))


def _silu(x):
    return x * _sigmoid(x)


def _softplus(x):
    return jnp.maximum(x, 0.0) + jnp.log(1.0 + jnp.exp(-jnp.abs(x)))


def _tile_lanes(p, n):
    return jnp.concatenate([p] * n, axis=1)


def _norm_small_kernel(h_ref, nw_ref, ws_ref, hn_ref, sm_ref):
    h = h_ref[...]
    y = h * lax.rsqrt(jnp.mean(h * h, axis=-1, keepdims=True) + EPS) * nw_ref[...]
    hn_ref[...] = y.astype(BF16)
    sm_ref[...] = _dot3(y, ws_ref[...])


def _norm_small(h, nw, w_small):
    rows = h.shape[0]
    return pl.pallas_call(
        _norm_small_kernel,
        grid=(rows // TB,),
        in_specs=[pl.BlockSpec((TB, D_MODEL), lambda i: (i, 0)),
                  pl.BlockSpec((1, D_MODEL), lambda i: (0, 0)),
                  pl.BlockSpec((D_MODEL, 128), lambda i: (0, 0))],
        out_specs=[pl.BlockSpec((TB, D_MODEL), lambda i: (i, 0)),
                   pl.BlockSpec((TB, 128), lambda i: (i, 0))],
        out_shape=[jax.ShapeDtypeStruct((rows, D_MODEL), BF16),
                   jax.ShapeDtypeStruct((rows, 128), F32)],
        compiler_params=_cparams(("arbitrary",)),
    )(h, nw, w_small)


def _mm_kernel(x_ref, w_ref, o_ref):
    o_ref[...] = jnp.dot(x_ref[...], w_ref[...], preferred_element_type=F32).astype(o_ref.dtype)


def _matmul(x, w, out_dtype, tm=TB, tn=1024):
    m, k = x.shape
    n = w.shape[1]
    return pl.pallas_call(
        _mm_kernel,
        grid=(n // tn, m // tm),
        in_specs=[pl.BlockSpec((tm, k), lambda j, i: (i, 0)),
                  pl.BlockSpec((k, tn), lambda j, i: (0, j))],
        out_specs=pl.BlockSpec((tm, tn), lambda j, i: (i, j)),
        out_shape=jax.ShapeDtypeStruct((m, n), out_dtype),
        compiler_params=_cparams(("arbitrary", "arbitrary")),
    )(x, w)


def _conv_silu(raw_ref, tail_ref, w_ref, b_ref=None):
    x = raw_ref[...].astype(F32)
    rows = x.shape[0]
    ext = jnp.concatenate([tail_ref[...], x], axis=0)
    w = w_ref[...]
    y = x * w[3:4]
    for j in (1, 2, 3):
        y = y + pltpu.roll(ext, j, 0)[8:8 + rows] * w[3 - j:4 - j]
    tail_ref[...] = x[rows - 8:rows]
    if b_ref is not None:
        y = y + b_ref[...]
    return _silu(y)


def _fill_chunk_matrices(u_ref, e_ref):
    jj = lax.broadcasted_iota(jnp.int32, (TB, TB), 0)
    ii = lax.broadcasted_iota(jnp.int32, (TB, TB), 1)
    same = (jj // CHUNK) == (ii // CHUNK)
    u_ref[...] = jnp.where(same & (jj <= ii), 1.0, 0.0).astype(BF16)
    e_ref[...] = jnp.where(same & (jj % CHUNK == CHUNK - 1), 1.0, 0.0).astype(BF16)


def _col_bcast(row, width):
    return jnp.broadcast_to(row, (width, TB)).T


def _dn_kernel(q_ref, k_ref, v_ref, z_ref, b_ref, a_ref, wq_ref, wk_ref, wv_ref, alog_ref, dtb_ref, nw_ref,
               o_ref, s_ref, tq_ref, tk_ref, tv_ref, u_ref, e_ref):
    t = pl.program_id(1)

    @pl.when(t == 0)
    def _():
        s_ref[...] = jnp.zeros_like(s_ref)
        tq_ref[...] = jnp.zeros_like(tq_ref)
        tk_ref[...] = jnp.zeros_like(tk_ref)
        tv_ref[...] = jnp.zeros_like(tv_ref)
        _fill_chunk_matrices(u_ref, e_ref)

    q = _conv_silu(q_ref, tq_ref, wq_ref)
    k = _conv_silu(k_ref, tk_ref, wk_ref)
    v = _conv_silu(v_ref, tv_ref, wv_ref)
    q = q * (lax.rsqrt(jnp.sum(q * q, axis=-1, keepdims=True) + EPS) * (DN_HEAD_DIM ** -0.5))
    k = k * lax.rsqrt(jnp.sum(k * k, axis=-1, keepdims=True) + EPS)

    beta = _sigmoid(b_ref[...])
    g = -jnp.exp(_tile_lanes(alog_ref[...], TB // 128)) * _softplus(a_ref[...] + _tile_lanes(dtb_ref[...], TB // 128))
    gc = _dot_sel(g, u_ref[...])
    gl = _dot_sel(gc, e_ref[...])
    eg = jnp.exp(gc)
    ek = jnp.exp(gl - gc)

    ii = lax.broadcasted_iota(jnp.int32, (CHUNK, CHUNK), 0)
    jj = lax.broadcasted_iota(jnp.int32, (CHUNK, CHUNK), 1)
    causal = ii >= jj
    strict = ii > jj
    eye = jnp.where(ii == jj, 1.0, 0.0)
    masks = [strict & ((ii >> 1) == (jj >> 1))]
    blk = 2
    while blk < CHUNK:
        sh = blk.bit_length() - 1
        masks.append(((ii >> (sh + 1)) == (jj >> (sh + 1))) & (((ii >> sh) & 1) == 1) & (((jj >> sh) & 1) == 0))
        blk *= 2

    kk = []
    qk = []
    for c in range(CPB):
        rows = slice(c * CHUNK, (c + 1) * CHUNK)
        kk.append(_bdot_nt(k[rows], k[rows]))
        qk.append(_bdot_nt(q[rows], k[rows]))

    heads = range(2)
    chunks = range(CPB)
    rows_of = [slice(c * CHUNK, (c + 1) * CHUNK) for c in chunks]
    eg_cb, rhs_all, qd_all, kd_all = [], [], [], []
    dec, a = {}, {}
    for hh in heads:
        gc_cb = _col_bcast(gc[hh:hh + 1], DN_HEAD_DIM)
        beta_cb = _col_bcast(beta[hh:hh + 1], DN_HEAD_DIM)
        eg_cb.append(_col_bcast(eg[hh:hh + 1], DN_HEAD_DIM))
        ek_cb = _col_bcast(ek[hh:hh + 1], DN_HEAD_DIM)
        vh = v[:, hh * DN_HEAD_DIM:(hh + 1) * DN_HEAD_DIM]
        rhs_all.append(jnp.concatenate([vh * beta_cb, k * (beta_cb * eg_cb[hh])], axis=1))
        qd_all.append(q * eg_cb[hh])
        kd_all.append(k * ek_cb)
        for c in chunks:
            diff = gc_cb[rows_of[c], :CHUNK] - gc[hh:hh + 1, c * CHUNK:(c + 1) * CHUNK]
            dec[hh, c] = jnp.where(causal, jnp.exp(jnp.where(causal, diff, 0.0)), 0.0)
            a[hh, c] = jnp.where(strict, beta_cb[rows_of[c], :CHUNK] * kk[c] * dec[hh, c], 0.0)
    inst = [(hh, c) for c in chunks for hh in heads]

    tinv = {i: eye - jnp.where(masks[0], a[i], 0.0) for i in inst}
    for m in masks[1:]:
        x = {i: _bdot(jnp.where(m, a[i], 0.0), tinv[i]) for i in inst}
        tinv = {i: tinv[i] - _bdot(tinv[i], x[i]) for i in inst}
    uw = {(hh, c): _bdot(tinv[hh, c], rhs_all[hh][rows_of[c]]) for hh, c in inst}
    kn = {(hh, c): _bdot_tn(kd_all[hh][rows_of[c]], uw[hh, c]) for hh, c in inst}
    qo = {(hh, c): _bdot(qk[c] * dec[hh, c], uw[hh, c]) for hh, c in inst}

    state = [s_ref[hh] for hh in heads]
    before = {}
    for c in chunks:
        for hh in heads:
            before[hh, c] = state[hh]
            g_end = eg_cb[hh][(c + 1) * CHUNK - 1:(c + 1) * CHUNK, :]
            state[hh] = (state[hh] * g_end - _bdot(kn[hh, c][:, DN_HEAD_DIM:], state[hh])
                         + kn[hh, c][:, :DN_HEAD_DIM])
    nw = nw_ref[...]
    for hh in heads:
        s_ref[hh] = state[hh]
        outs = [_bdot(qd_all[hh][rows_of[c]] - qo[hh, c][:, DN_HEAD_DIM:], before[hh, c])
                + qo[hh, c][:, :DN_HEAD_DIM] for c in chunks]
        o = jnp.concatenate(outs, axis=0)
        zh = z_ref[:, hh * DN_HEAD_DIM:(hh + 1) * DN_HEAD_DIM].astype(F32)
        o = o * lax.rsqrt(jnp.mean(o * o, axis=-1, keepdims=True) + EPS) * nw * _silu(zh)
        o_ref[:, hh * DN_HEAD_DIM:(hh + 1) * DN_HEAD_DIM] = o.astype(o_ref.dtype)


def _gated_deltanet(qkv, z, b_rows, a_rows, conv_w, alog_b, dtb_b, norm_w):
    rows = qkv.shape[0]
    nqk = DN_QK_HEADS
    d = DN_HEAD_DIM
    return pl.pallas_call(
        _dn_kernel,
        grid=(nqk, rows // TB),
        in_specs=[
            pl.BlockSpec((TB, d), lambda j, t: (t, j)),
            pl.BlockSpec((TB, d), lambda j, t: (t, nqk + j)),
            pl.BlockSpec((TB, 2 * d), lambda j, t: (t, nqk + j)),
            pl.BlockSpec((TB, 2 * d), lambda j, t: (t, j)),
            pl.BlockSpec((None, 2, TB), lambda j, t: (j, 0, t)),
            pl.BlockSpec((None, 2, TB), lambda j, t: (j, 0, t)),
            pl.BlockSpec((4, d), lambda j, t: (0, j)),
            pl.BlockSpec((4, d), lambda j, t: (0, nqk + j)),
            pl.BlockSpec((4, 2 * d), lambda j, t: (0, nqk + j)),
            pl.BlockSpec((None, 2, 128), lambda j, t: (j, 0, 0)),
            pl.BlockSpec((None, 2, 128), lambda j, t: (j, 0, 0)),
            pl.BlockSpec((1, d), lambda j, t: (0, 0)),
        ],
        out_specs=pl.BlockSpec((TB, 2 * d), lambda j, t: (t, j)),
        out_shape=jax.ShapeDtypeStruct((rows, DN_VAL_DIM), BF16),
        scratch_shapes=[
            pltpu.VMEM((2, d, d), F32),
            pltpu.VMEM((8, d), F32),
            pltpu.VMEM((8, d), F32),
            pltpu.VMEM((8, 2 * d), F32),
            pltpu.VMEM((TB, TB), BF16),
            pltpu.VMEM((TB, TB), BF16),
        ],
        compiler_params=_cparams(("arbitrary", "arbitrary")),
    )(qkv, qkv, qkv, z, b_rows, a_rows, conv_w, conv_w, conv_w, alog_b, dtb_b, norm_w)


def _ssd_kernel(x_ref, b_ref, c_ref, z_ref, dt_ref, wx_ref, wb_ref, wc_ref, bx_ref, bb_ref, bc_ref,
                alog_ref, dtb_ref, dexp_ref, nw_ref, o_ref, h_ref, tx_ref, tb_ref, tc_ref, u_ref, e_ref):
    t = pl.program_id(1)

    @pl.when(t == 0)
    def _():
        h_ref[...] = jnp.zeros_like(h_ref)
        tx_ref[...] = jnp.zeros_like(tx_ref)
        tb_ref[...] = jnp.zeros_like(tb_ref)
        tc_ref[...] = jnp.zeros_like(tc_ref)
        _fill_chunk_matrices(u_ref, e_ref)

    xh = _conv_silu(x_ref, tx_ref, wx_ref, bx_ref)
    bm = _conv_silu(b_ref, tb_ref, wb_ref, bb_ref)
    cm = _conv_silu(c_ref, tc_ref, wc_ref, bc_ref)

    nrep = TB // 128
    dt = _softplus(dt_ref[...] + _tile_lanes(dtb_ref[...], nrep))
    pos = t * TB + lax.broadcasted_iota(jnp.int32, (SSD_REP, TB), 1)
    dt = jnp.where(pos >= FRONT - N_META, dt, 0.0)
    la = -jnp.exp(_tile_lanes(alog_ref[...], nrep)) * dt
    ac = _dot_sel(la, u_ref[...])
    al = _dot_sel(ac, e_ref[...])
    dtek = dt * jnp.exp(al - ac)

    def col_bcast_heads(rowsv):
        stacked = jnp.concatenate(
            [jnp.broadcast_to(rowsv[r:r + 1], (SSD_HEAD_DIM, TB)) for r in range(SSD_REP)], axis=0)
        return stacked.T

    ac_cb = col_bcast_heads(ac)
    ea_cb = jnp.exp(ac_cb)
    xd = xh * col_bcast_heads(dtek)

    ii = lax.broadcasted_iota(jnp.int32, (CHUNK, 2 * CHUNK), 0)
    jj = lax.broadcasted_iota(jnp.int32, (CHUNK, 2 * CHUNK), 1)
    causal2 = ii >= (jj & (CHUNK - 1))
    li = lax.broadcasted_iota(jnp.int32, (CHUNK, 2 * CHUNK), 1)
    lo_half = li < CHUNK

    state = h_ref[...]
    outs = []
    for c in range(CPB):
        rows = slice(c * CHUNK, (c + 1) * CHUNK)
        cc = cm[rows]
        bc = bm[rows]
        cb = _bdot_nt(cc, bc)
        cb2 = jnp.concatenate([cb, cb], axis=1)
        y_off = _bdot(cc, state) * ea_cb[rows]
        yd = []
        for m in range(SSD_REP // 2):
            lanes = slice(m * 128, (m + 1) * 128)
            rowv = jnp.concatenate([ac[2 * m:2 * m + 1, rows], ac[2 * m + 1:2 * m + 2, rows]], axis=1)
            dtrow = jnp.concatenate([dt[2 * m:2 * m + 1, rows], dt[2 * m + 1:2 * m + 2, rows]], axis=1)
            diff = ac_cb[rows, lanes] - rowv
            seg = jnp.where(causal2, jnp.exp(jnp.where(causal2, diff, 0.0)), 0.0)
            mm = cb2 * seg * dtrow
            x2 = xh[rows, lanes]
            rhs = jnp.concatenate([jnp.where(lo_half, x2, 0.0), jnp.where(lo_half, 0.0, x2)], axis=0)
            yd.append(_bdot(mm, rhs))
        outs.append(jnp.concatenate(yd, axis=1) + y_off)
        ea_last = ea_cb[(c + 1) * CHUNK - 1:(c + 1) * CHUNK, :]
        state = state * ea_last + _bdot_tn(bc, xd[rows])
    h_ref[...] = state

    y = jnp.concatenate(outs, axis=0) + dexp_ref[...] * xh
    y = y * _silu(z_ref[...].astype(F32))
    y = y * lax.rsqrt(jnp.mean(y * y, axis=-1, keepdims=True) + EPS) * nw_ref[...]
    o_ref[...] = y.astype(o_ref.dtype)


def _mamba2_ssd(xbc, z, dt_rows, conv_w, conv_b, alog_b, dtb_b, d_exp, norm_w):
    rows = xbc.shape[0]
    gd = SSD_GROUP_DIM
    nx = SSD_D_INNER // SSD_STATE
    return pl.pallas_call(
        _ssd_kernel,
        grid=(SSD_GROUPS, rows // TB),
        in_specs=[
            pl.BlockSpec((TB, gd), lambda g, t: (t, g)),
            pl.BlockSpec((TB, SSD_STATE), lambda g, t: (t, nx + g)),
            pl.BlockSpec((TB, SSD_STATE), lambda g, t: (t, nx + SSD_GROUPS + g)),
            pl.BlockSpec((TB, gd), lambda g, t: (t, g)),
            pl.BlockSpec((None, SSD_REP, TB), lambda g, t: (g, 0, t)),
            pl.BlockSpec((4, gd), lambda g, t: (0, g)),
            pl.BlockSpec((4, SSD_STATE), lambda g, t: (0, nx + g)),
            pl.BlockSpec((4, SSD_STATE), lambda g, t: (0, nx + SSD_GROUPS + g)),
            pl.BlockSpec((1, gd), lambda g, t: (0, g)),
            pl.BlockSpec((1, SSD_STATE), lambda g, t: (0, nx + g)),
            pl.BlockSpec((1, SSD_STATE), lambda g, t: (0, nx + SSD_GROUPS + g)),
            pl.BlockSpec((None, SSD_REP, 128), lambda g, t: (g, 0, 0)),
            pl.BlockSpec((None, SSD_REP, 128), lambda g, t: (g, 0, 0)),
            pl.BlockSpec((1, gd), lambda g, t: (0, g)),
            pl.BlockSpec((1, gd), lambda g, t: (0, g)),
        ],
        out_specs=pl.BlockSpec((TB, gd), lambda g, t: (t, g)),
        out_shape=jax.ShapeDtypeStruct((rows, SSD_D_INNER), BF16),
        scratch_shapes=[
            pltpu.VMEM((SSD_STATE, gd), F32),
            pltpu.VMEM((8, gd), F32),
            pltpu.VMEM((8, SSD_STATE), F32),
            pltpu.VMEM((8, SSD_STATE), F32),
            pltpu.VMEM((TB, TB), BF16),
            pltpu.VMEM((TB, TB), BF16),
        ],
        compiler_params=_cparams(("arbitrary", "arbitrary")),
    )(xbc, xbc, xbc, z, dt_rows, conv_w, conv_w, conv_w, conv_b, conv_b, conv_b, alog_b, dtb_b, d_exp, norm_w)


def _merge_kernel(ydn_ref, yssd_ref, gd_ref, gs_ref, wdn_ref, wssd_ref, o_ref):
    a = jnp.dot(ydn_ref[...], wdn_ref[...], preferred_element_type=F32)
    b = jnp.dot(yssd_ref[...], wssd_ref[...], preferred_element_type=F32)
    o = _sigmoid(gd_ref[...].astype(F32)) * a + _sigmoid(gs_ref[...].astype(F32)) * b
    o_ref[...] = o.astype(o_ref.dtype)


def _merge(y_dn, y_ssd, gates, w_dn, w_ssd, tn=512):
    ntile = (y_dn.shape[0] - FRONT) // TB
    off = FRONT // TB
    ng = D_MODEL // tn
    return pl.pallas_call(
        _merge_kernel,
        grid=(ng, ntile),
        in_specs=[
            pl.BlockSpec((TB, DN_VAL_DIM), lambda j, i: (i + off, 0)),
            pl.BlockSpec((TB, SSD_D_INNER), lambda j, i: (i + off, 0)),
            pl.BlockSpec((TB, tn), lambda j, i: (i + off, j)),
            pl.BlockSpec((TB, tn), lambda j, i: (i + off, ng + j)),
            pl.BlockSpec((DN_VAL_DIM, tn), lambda j, i: (0, j)),
            pl.BlockSpec((SSD_D_INNER, tn), lambda j, i: (0, j)),
        ],
        out_specs=pl.BlockSpec((TB, tn), lambda j, i: (i, j)),
        out_shape=jax.ShapeDtypeStruct((ntile * TB, D_MODEL), BF16),
        compiler_params=_cparams(("arbitrary", "arbitrary")),
    )(y_dn, y_ssd, gates, gates, w_dn, w_ssd)


def _out_router_kernel(m_ref, wo_ref, x_ref, nw_ref, wr_ref, br_ref,
                       h2_ref, hn_ref, idx_ref, p_ref, rank_ref, cnt_ref, carry_ref, us_ref):
    i = pl.program_id(0)
    rt = x_ref.shape[0]

    @pl.when(i == 0)
    def _():
        carry_ref[...] = jnp.zeros_like(carry_ref)
        jj = lax.broadcasted_iota(jnp.int32, (rt, rt), 0)
        ii = lax.broadcasted_iota(jnp.int32, (rt, rt), 1)
        us_ref[...] = jnp.where(jj < ii, 1.0, 0.0).astype(BF16)

    h2 = x_ref[...] + jnp.dot(m_ref[...], wo_ref[...], preferred_element_type=F32)
    h2_ref[...] = h2
    hn = h2 * lax.rsqrt(jnp.mean(h2 * h2, axis=-1, keepdims=True) + EPS) * nw_ref[...]
    hn_ref[...] = hn

    lg = _dot3(wr_ref[...], hn, nt=True) + br_ref[...]
    eio = lax.broadcasted_iota(jnp.int32, (N_EXPERTS, rt), 0)
    vals, hits, idxs = [], [], []
    cur = lg
    for _ in range(TOP_K):
        m = jnp.max(cur, axis=0, keepdims=True)
        ik = jnp.min(jnp.where(cur == m, eio, N_EXPERTS), axis=0, keepdims=True)
        hit = eio == ik
        vals.append(m)
        idxs.append(ik)
        hits.append(hit)
        cur = jnp.where(hit, -jnp.inf, cur)
    ex = [jnp.exp(vk - vals[0]) for vk in vals]
    den = ex[0] + ex[1] + ex[2] + ex[3]
    idx_ref[...] = jnp.concatenate(idxs, axis=0)
    p_ref[...] = jnp.concatenate([e / den for e in ex], axis=0)

    sel = jnp.zeros((N_EXPERTS, rt), F32)
    for hit in hits:
        sel = sel + jnp.where(hit, 1.0, 0.0)
    before = jnp.dot(sel.astype(BF16), us_ref[...], preferred_element_type=F32) \
        + _tile_lanes(carry_ref[...], rt // 128)
    ranks = [jnp.sum(jnp.where(hit, before, 0.0), axis=0, keepdims=True) for hit in hits]
    rank_ref[...] = jnp.concatenate(ranks, axis=0).astype(jnp.int32)
    carry = carry_ref[...] + jnp.dot(sel.astype(BF16), jnp.ones((rt, 128), BF16), preferred_element_type=F32)
    carry_ref[...] = carry
    cnt_ref[...] = carry


def _out_router(merged, w_out, x, nw, w_router_t, b_router_c):
    n_tok = x.shape[0]
    rt = ROUTER_RT
    tok_spec = pl.BlockSpec((TOP_K, rt), lambda i: (0, i))
    return pl.pallas_call(
        _out_router_kernel,
        grid=(n_tok // rt,),
        in_specs=[
            pl.BlockSpec((rt, D_MODEL), lambda i: (i, 0)),
            pl.BlockSpec((D_MODEL, D_MODEL), lambda i: (0, 0)),
            pl.BlockSpec((rt, D_MODEL), lambda i: (i, 0)),
            pl.BlockSpec((1, D_MODEL), lambda i: (0, 0)),
            pl.BlockSpec((N_EXPERTS, D_MODEL), lambda i: (0, 0)),
            pl.BlockSpec((N_EXPERTS, 1), lambda i: (0, 0)),
        ],
        out_specs=[
            pl.BlockSpec((rt, D_MODEL), lambda i: (i, 0)),
            pl.BlockSpec((rt, D_MODEL), lambda i: (i, 0)),
            tok_spec, tok_spec, tok_spec,
            pl.BlockSpec((N_EXPERTS, 128), lambda i: (0, 0)),
        ],
        out_shape=[
            jax.ShapeDtypeStruct((n_tok, D_MODEL), F32),
            jax.ShapeDtypeStruct((n_tok, D_MODEL), F32),
            jax.ShapeDtypeStruct((TOP_K, n_tok), jnp.int32),
            jax.ShapeDtypeStruct((TOP_K, n_tok), F32),
            jax.ShapeDtypeStruct((TOP_K, n_tok), jnp.int32),
            jax.ShapeDtypeStruct((N_EXPERTS, 128), F32),
        ],
        scratch_shapes=[pltpu.VMEM((N_EXPERTS, 128), F32), pltpu.VMEM((rt, rt), BF16)],
        compiler_params=_cparams(("arbitrary",)),
    )(merged, w_out, x, nw, w_router_t, b_router_c)


def _gather_kernel(nused_ref, tok_ref, src_ref, o_ref, buf_ref, sem):
    b = pl.program_id(0)
    nu = nused_ref[0]

    def issue(blk, slot):
        base = blk * MOE_BM

        def body(r, carry):
            pltpu.make_async_copy(src_ref.at[pl.ds(tok_ref[base + r], 1)], buf_ref.at[slot, pl.ds(r, 1)],
                                  sem.at[slot]).start()
            return carry

        lax.fori_loop(0, MOE_BM, body, 0, unroll=8)

    @pl.when(b == 0)
    def _():
        issue(0, 0)

    @pl.when(b + 1 < nu)
    def _():
        issue(b + 1, (b + 1) % 2)

    @pl.when(b < nu)
    def _():
        slot = b % 2
        pltpu.make_async_copy(src_ref.at[pl.ds(0, MOE_BM)], buf_ref.at[slot], sem.at[slot]).wait()
        o_ref[...] = buf_ref[slot].astype(o_ref.dtype)

    @pl.when(b >= nu)
    def _():
        o_ref[...] = jnp.zeros_like(o_ref)


def _gather_rows(n_used, row_tok, src):
    nb = row_tok.shape[0] // MOE_BM
    return pl.pallas_call(
        _gather_kernel,
        grid_spec=pltpu.PrefetchScalarGridSpec(
            num_scalar_prefetch=2,
            grid=(nb,),
            in_specs=[pl.BlockSpec(memory_space=pl.ANY)],
            out_specs=pl.BlockSpec((MOE_BM, D_MODEL), lambda b, nu, tok: (b, 0)),
            scratch_shapes=[pltpu.VMEM((2, MOE_BM, D_MODEL), F32), pltpu.SemaphoreType.DMA((2,))],
        ),
        out_shape=jax.ShapeDtypeStruct((nb * MOE_BM, D_MODEL), BF16),
        compiler_params=_cparams(("arbitrary",)),
    )(n_used, row_tok, src)


def _expert_changed(be_ref, b):
    return jnp.logical_or(b == 0, be_ref[b] != be_ref[jnp.maximum(b - 1, 0)])


def _expert_up_kernel(be_ref, nused_ref, x_ref, wg_ref, wu_ref, bg_ref, bu_ref, o_ref, wgb_ref, wub_ref):
    b = pl.program_id(1)

    @pl.when(b < nused_ref[0])
    def _():
        @pl.when(_expert_changed(be_ref, b))
        def _():
            wgb_ref[...] = wg_ref[...].astype(BF16)
            wub_ref[...] = wu_ref[...].astype(BF16)

        x = x_ref[...]
        gt = jnp.dot(x, wgb_ref[...], preferred_element_type=F32) + bg_ref[...]
        up = jnp.dot(x, wub_ref[...], preferred_element_type=F32) + bu_ref[...]
        gt = jnp.minimum(gt, SWIGLU_LIMIT)
        up = jnp.clip(up, -SWIGLU_LIMIT, SWIGLU_LIMIT)
        act = gt * _sigmoid(SWIGLU_ALPHA * gt) * (up + 1.0)
        o_ref[...] = act.astype(o_ref.dtype)

    @pl.when(b >= nused_ref[0])
    def _():
        o_ref[...] = jnp.zeros_like(o_ref)


def _expert_up(block_e, n_used, xs, w_gate, w_up, b_gate, b_up):
    nb = xs.shape[0] // MOE_BM
    tn = MOE_TN_UP
    wspec = pl.BlockSpec((None, D_MODEL, tn), lambda j, b, be, nu: (be[b], 0, j))
    bspec = pl.BlockSpec((None, 1, tn), lambda j, b, be, nu: (be[b], 0, j))
    return pl.pallas_call(
        _expert_up_kernel,
        grid_spec=pltpu.PrefetchScalarGridSpec(
            num_scalar_prefetch=2,
            grid=(D_MODEL // tn, nb),
            in_specs=[pl.BlockSpec((MOE_BM, D_MODEL), lambda j, b, be, nu: (b, 0)), wspec, wspec, bspec, bspec],
            out_specs=pl.BlockSpec((MOE_BM, tn), lambda j, b, be, nu: (b, j)),
            scratch_shapes=[pltpu.VMEM((D_MODEL, tn), BF16), pltpu.VMEM((D_MODEL, tn), BF16)],
        ),
        out_shape=jax.ShapeDtypeStruct((nb * MOE_BM, D_MODEL), BF16),
        compiler_params=_cparams(("arbitrary", "arbitrary")),
    )(block_e, n_used, xs, w_gate, w_up, b_gate, b_up)


def _expert_down_kernel(be_ref, nused_ref, a_ref, wd_ref, bd_ref, o_ref, wdb_ref):
    b = pl.program_id(1)

    @pl.when(b < nused_ref[0])
    def _():
        @pl.when(_expert_changed(be_ref, b))
        def _():
            wdb_ref[...] = wd_ref[...].astype(BF16)

        o_ref[...] = jnp.dot(a_ref[...], wdb_ref[...], preferred_element_type=F32) + bd_ref[...]

    @pl.when(b >= nused_ref[0])
    def _():
        o_ref[...] = jnp.zeros_like(o_ref)


def _expert_down(block_e, n_used, act, w_down, b_down):
    nb = act.shape[0] // MOE_BM
    tn = MOE_TN_DOWN
    return pl.pallas_call(
        _expert_down_kernel,
        grid_spec=pltpu.PrefetchScalarGridSpec(
            num_scalar_prefetch=2,
            grid=(D_MODEL // tn, nb),
            in_specs=[pl.BlockSpec((MOE_BM, D_MODEL), lambda j, b, be, nu: (b, 0)),
                      pl.BlockSpec((None, D_MODEL, tn), lambda j, b, be, nu: (be[b], 0, j)),
                      pl.BlockSpec((None, 1, tn), lambda j, b, be, nu: (be[b], 0, j))],
            out_specs=pl.BlockSpec((MOE_BM, tn), lambda j, b, be, nu: (b, j)),
            scratch_shapes=[pltpu.VMEM((D_MODEL, tn), BF16)],
        ),
        out_shape=jax.ShapeDtypeStruct((nb * MOE_BM, D_MODEL), F32),
        compiler_params=_cparams(("arbitrary", "arbitrary")),
    )(block_e, n_used, act, w_down, b_down)


def _combine_kernel(dest_ref, h2_ref, p_ref, fw_ref, y_ref, o_ref, buf_ref, sem):
    i = pl.program_id(0)
    tt = COMB_TT

    def issue(tile, slot):
        base = tile * (TOP_K * tt)

        def body(r, carry):
            for k in range(TOP_K):
                pltpu.make_async_copy(y_ref.at[pl.ds(dest_ref[base + k * tt + r], 1)],
                                      buf_ref.at[slot, k, pl.ds(r, 1)], sem.at[slot]).start()
            return carry

        lax.fori_loop(0, tt, body, 0, unroll=4)

    @pl.when(i == 0)
    def _():
        issue(0, 0)

    @pl.when(i + 1 < pl.num_programs(0))
    def _():
        issue(i + 1, (i + 1) % 2)

    slot = i % 2
    for k in range(TOP_K):
        pltpu.make_async_copy(y_ref.at[pl.ds(0, tt)], buf_ref.at[slot, k], sem.at[slot]).wait()
    p = p_ref[...]
    acc = h2_ref[...]
    for k in range(TOP_K):
        acc = acc + p[:, k:k + 1] * buf_ref[slot, k]
    o_ref[...] = acc * lax.rsqrt(jnp.mean(acc * acc, axis=-1, keepdims=True) + EPS) * fw_ref[...]


def _combine(dest, h2, p_cols, final_w, y_rows):
    n_tok = h2.shape[0]
    tt = COMB_TT
    return pl.pallas_call(
        _combine_kernel,
        grid_spec=pltpu.PrefetchScalarGridSpec(
            num_scalar_prefetch=1,
            grid=(n_tok // tt,),
            in_specs=[
                pl.BlockSpec((tt, D_MODEL), lambda i, d: (i, 0)),
                pl.BlockSpec((tt, TOP_K), lambda i, d: (i, 0)),
                pl.BlockSpec((1, D_MODEL), lambda i, d: (0, 0)),
                pl.BlockSpec(memory_space=pl.ANY),
            ],
            out_specs=pl.BlockSpec((tt, D_MODEL), lambda i, d: (i, 0)),
            scratch_shapes=[pltpu.VMEM((2, TOP_K, tt, D_MODEL), F32), pltpu.SemaphoreType.DMA((2,))],
        ),
        out_shape=jax.ShapeDtypeStruct((n_tok, D_MODEL), F32),
        compiler_params=_cparams(("arbitrary",)),
    )(dest, h2, p_cols, final_w, y_rows)


def _mixer(h_ext, norm_w, w_in, dn_conv_w, dn_a_log, dn_dt_bias, dn_norm_w,
           ssd_conv_w, ssd_conv_b, ssd_a_log, ssd_dt_bias, ssd_d, ssd_norm_w, w_proj_dn, w_proj_ssd):
    o_qkv = 0
    o_dnz = o_qkv + DN_CONV_DIM
    o_b = o_dnz + DN_VAL_DIM
    o_a = o_b + DN_V_HEADS
    o_sz = o_a + DN_V_HEADS
    o_xbc = o_sz + SSD_D_INNER
    o_dt = o_xbc + SSD_CONV_DIM
    o_gate = o_dt + SSD_HEADS
    d_in = o_gate + 2 * D_MODEL
    w_small = jnp.concatenate([w_in[:, o_b:o_sz], w_in[:, o_dt:o_gate]], axis=1)

    hn, small = _norm_small(h_ext, norm_w.reshape(1, D_MODEL), w_small)
    qkv = _matmul(hn, w_in[:, o_qkv:o_dnz].astype(BF16), BF16)
    dn_z = _matmul(hn, w_in[:, o_dnz:o_b].astype(BF16), BF16)
    ssd_z = _matmul(hn, w_in[:, o_sz:o_xbc].astype(BF16), BF16)
    xbc = _matmul(hn, w_in[:, o_xbc:o_dt].astype(BF16), BF16)
    gates = _matmul(hn, w_in[:, o_gate:d_in].astype(BF16), BF16)

    rows = h_ext.shape[0]
    sm_t = small.T
    b_rows = sm_t[0:DN_V_HEADS].reshape(DN_QK_HEADS, 2, rows)
    a_rows = sm_t[DN_V_HEADS:2 * DN_V_HEADS].reshape(DN_QK_HEADS, 2, rows)
    dt_rows = sm_t[2 * DN_V_HEADS:].reshape(SSD_GROUPS, SSD_REP, rows)

    bc = lambda p, g, r: jnp.broadcast_to(p.astype(F32).reshape(g, r, 1), (g, r, 128))
    y_dn = _gated_deltanet(qkv, dn_z, b_rows, a_rows, dn_conv_w,
                           bc(dn_a_log, DN_QK_HEADS, 2), bc(dn_dt_bias, DN_QK_HEADS, 2),
                           dn_norm_w.reshape(1, DN_HEAD_DIM))
    y_ssd = _mamba2_ssd(xbc, ssd_z, dt_rows, ssd_conv_w, ssd_conv_b.reshape(1, SSD_CONV_DIM),
                        bc(ssd_a_log, SSD_GROUPS, SSD_REP), bc(ssd_dt_bias, SSD_GROUPS, SSD_REP),
                        jnp.repeat(ssd_d, SSD_HEAD_DIM).reshape(1, SSD_D_INNER),
                        ssd_norm_w.reshape(1, SSD_D_INNER))
    return _merge(y_dn, y_ssd, gates, w_proj_dn.astype(BF16), w_proj_ssd.astype(BF16))


def _moe_and_final(merged, x2, w_out, norm_ffn_w, w_router, b_router, w_gate, b_gate, w_up, b_up,
                   w_down, b_down, final_norm_w):
    n_tok = x2.shape[0]
    h2, hn2, idx, probs, rank, cnt = _out_router(
        merged, w_out.astype(BF16), x2, norm_ffn_w.reshape(1, D_MODEL), w_router.T,
        b_router.reshape(N_EXPERTS, 1))

    counts = cnt[:, 0].astype(jnp.int32)
    padded = (counts + MOE_BM - 1) // MOE_BM * MOE_BM
    pad_end = jnp.cumsum(padded)
    pad_start = pad_end - padded
    hit = idx[:, :, None] == jnp.arange(N_EXPERTS, dtype=jnp.int32)
    dest = jnp.sum(jnp.where(hit, pad_start.astype(jnp.int32), 0), axis=-1) + rank
    nb = -(-(n_tok * TOP_K + N_EXPERTS * (MOE_BM - 1)) // MOE_BM)
    tok_ids = jnp.broadcast_to(jnp.arange(n_tok, dtype=jnp.int32)[None], (TOP_K, n_tok))
    row_tok = jnp.zeros((nb * MOE_BM,), jnp.int32).at[dest.reshape(-1)].set(tok_ids.reshape(-1))
    block_e = jnp.minimum(jnp.searchsorted(pad_end, jnp.arange(nb, dtype=jnp.int32) * MOE_BM, side='right'),
                          N_EXPERTS - 1).astype(jnp.int32)
    n_used = (pad_end[-1:] // MOE_BM).astype(jnp.int32)

    xs = _gather_rows(n_used, row_tok, hn2)
    act = _expert_up(block_e, n_used, xs, w_gate, w_up,
                     b_gate.reshape(N_EXPERTS, 1, D_MODEL), b_up.reshape(N_EXPERTS, 1, D_MODEL))
    y_rows = _expert_down(block_e, n_used, act, w_down, b_down.reshape(N_EXPERTS, 1, D_MODEL))
    dest_t = dest.reshape(TOP_K, n_tok // COMB_TT, COMB_TT).transpose(1, 0, 2).reshape(-1)
    return _combine(dest_t, h2, probs.T, final_norm_w.reshape(1, D_MODEL), y_rows)


def kernel(x, meta_tokens, norm_mix_w, w_in, dn_conv_w, dn_a_log, dn_dt_bias, dn_norm_w, ssd_conv_w, ssd_conv_b, ssd_a_log, ssd_dt_bias, ssd_d, ssd_norm_w, w_proj_dn, w_proj_ssd, w_out, norm_ffn_w, w_router, b_router, w_gate, b_gate, w_up, b_up, w_down, b_down, final_norm_w):
    bsz, seq, d = x.shape
    assert bsz == 1 and d == D_MODEL and seq % TB == 0
    assert norm_mix_w.shape[0] == 1, "single-layer block"
    x2 = x.reshape(seq, d)
    h_ext = jnp.concatenate([jnp.zeros((FRONT - N_META, d), x.dtype), meta_tokens.astype(x.dtype), x2], axis=0)
    merged = _mixer(h_ext, norm_mix_w[0], w_in[0], dn_conv_w[0], dn_a_log[0], dn_dt_bias[0], dn_norm_w[0],
                    ssd_conv_w[0], ssd_conv_b[0], ssd_a_log[0], ssd_dt_bias[0], ssd_d[0], ssd_norm_w[0],
                    w_proj_dn[0], w_proj_ssd[0])
    out = _moe_and_final(merged, x2, w_out[0], norm_ffn_w[0], w_router[0], b_router[0], w_gate[0], b_gate[0],
                         w_up[0], b_up[0], w_down[0], b_down[0], final_norm_w)
    return out.reshape(bsz, seq, d)
```

```python
import functools

import jax
import jax.numpy as jnp
from jax import lax
from jax.experimental import pallas as pl
from jax.experimental.pallas import tpu as pltpu

F32 = jnp.float32
BF16 = jnp.bfloat16

D_MODEL = 2048
N_META = 16
CHUNK = 64
EPS = 1e-6

DN_QK_HEADS = 16
DN_V_HEADS = 32
DN_HEAD_DIM = 128
DN_KEY_DIM = DN_QK_HEADS * DN_HEAD_DIM
DN_VAL_DIM = DN_V_HEADS * DN_HEAD_DIM
DN_CONV_DIM = 2 * DN_KEY_DIM + DN_VAL_DIM

SSD_D_INNER = 2 * D_MODEL
SSD_HEAD_DIM = 64
SSD_HEADS = SSD_D_INNER // SSD_HEAD_DIM
SSD_GROUPS = 8
SSD_REP = SSD_HEADS // SSD_GROUPS
SSD_STATE = 128
SSD_GS = SSD_GROUPS * SSD_STATE
SSD_CONV_DIM = SSD_D_INNER + 2 * SSD_GS
SSD_GROUP_DIM = SSD_D_INNER // SSD_GROUPS

N_EXPERTS = 32
TOP_K = 4
SWIGLU_LIMIT = 7.0
SWIGLU_ALPHA = 1.702

TB = 512
FRONT = TB
CPB = TB // CHUNK
MOE_BM = 256
MOE_TN_UP = 1024
MOE_TN_DOWN = 2048
COMB_TT = 256
ROUTER_RT = 256
DN_GROUP = 2
DN_QPS = 4
VMEM_LIMIT = 56 * 1024 * 1024


def _cparams(sem):
    return pltpu.CompilerParams(dimension_semantics=sem, vmem_limit_bytes=VMEM_LIMIT)


def _bdot(a, b):
    return jnp.dot(a.astype(BF16), b.astype(BF16), preferred_element_type=F32)


def _bdot_nt(a, b):
    return lax.dot_general(a.astype(BF16), b.astype(BF16), (((1,), (1,)), ((), ())),
                           preferred_element_type=F32)


def _bdot_tn(a, b):
    return lax.dot_general(a.astype(BF16), b.astype(BF16), (((0,), (0,)), ((), ())),
                           preferred_element_type=F32)


def _split3(x):
    x1 = x.astype(BF16)
    r1 = x - x1.astype(F32)
    x2 = r1.astype(BF16)
    x3 = (r1 - x2.astype(F32)).astype(BF16)
    return x1, x2, x3


def _dot_sel(x, m):
    x1, x2, x3 = _split3(x)
    d = lambda a: jnp.dot(a, m, preferred_element_type=F32)
    return d(x1) + d(x2) + d(x3)


def _dot3(a, b, nt=False):
    a1 = a.astype(BF16)
    a2 = (a - a1.astype(F32)).astype(BF16)
    b1 = b.astype(BF16)
    b2 = (b - b1.astype(F32)).astype(BF16)
    if nt:
        d = lambda p, q: lax.dot_general(p, q, (((1,), (1,)), ((), ())), preferred_element_type=F32)
    else:
        d = lambda p, q: jnp.dot(p, q, preferred_element_type=F32)
    return d(a1, b1) + d(a2, b1) + d(a1, b2)


def _sigmoid(x):
    return 1.0 / (1.0 + jnp.exp(-x))


def _silu(x):
    return x * _sigmoid(x)


def _softplus(x):
    return jnp.maximum(x, 0.0) + jnp.log(1.0 + jnp.exp(-jnp.abs(x)))


def _tile_lanes(p, n):
    return jnp.concatenate([p] * n, axis=1)


def _norm_small_kernel(h_ref, nw_ref, ws_ref, hn_ref, sm_ref):
    h = h_ref[...]
    y = h * lax.rsqrt(jnp.mean(h * h, axis=-1, keepdims=True) + EPS) * nw_ref[...]
    hn_ref[...] = y.astype(BF16)
    sm_ref[...] = _dot3(y, ws_ref[...])


def _norm_small(h, nw, w_small):
    rows = h.shape[0]
    return pl.pallas_call(
        _norm_small_kernel,
        grid=(rows // TB,),
        in_specs=[pl.BlockSpec((TB, D_MODEL), lambda i: (i, 0)),
                  pl.BlockSpec((1, D_MODEL), lambda i: (0, 0)),
                  pl.BlockSpec((D_MODEL, 128), lambda i: (0, 0))],
        out_specs=[pl.BlockSpec((TB, D_MODEL), lambda i: (i, 0)),
                   pl.BlockSpec((TB, 128), lambda i: (i, 0))],
        out_shape=[jax.ShapeDtypeStruct((rows, D_MODEL), BF16),
                   jax.ShapeDtypeStruct((rows, 128), F32)],
        compiler_params=_cparams(("arbitrary",)),
    )(h, nw, w_small)


def _mm_kernel(x_ref, w_ref, o_ref):
    o_ref[...] = jnp.dot(x_ref[...], w_ref[...], preferred_element_type=F32).astype(o_ref.dtype)


def _matmul(x, w, out_dtype, tm=TB, tn=1024):
    m, k = x.shape
    n = w.shape[1]
    return pl.pallas_call(
        _mm_kernel,
        grid=(n // tn, m // tm),
        in_specs=[pl.BlockSpec((tm, k), lambda j, i: (i, 0)),
                  pl.BlockSpec((k, tn), lambda j, i: (0, j))],
        out_specs=pl.BlockSpec((tm, tn), lambda j, i: (i, j)),
        out_shape=jax.ShapeDtypeStruct((m, n), out_dtype),
        compiler_params=_cparams(("arbitrary", "arbitrary")),
    )(x, w)


def _conv_silu(raw_ref, tail_ref, w_ref, b_ref=None):
    x = raw_ref[...].astype(F32)
    rows = x.shape[0]
    ext = jnp.concatenate([tail_ref[...], x], axis=0)
    w = w_ref[...]
    y = x * w[3:4]
    for j in (1, 2, 3):
        y = y + pltpu.roll(ext, j, 0)[8:8 + rows] * w[3 - j:4 - j]
    tail_ref[...] = x[rows - 8:rows]
    if b_ref is not None:
        y = y + b_ref[...]
    return _silu(y)


def _fill_chunk_matrices(u_ref, e_ref):
    jj = lax.broadcasted_iota(jnp.int32, (TB, TB), 0)
    ii = lax.broadcasted_iota(jnp.int32, (TB, TB), 1)
    same = (jj // CHUNK) == (ii // CHUNK)
    u_ref[...] = jnp.where(same & (jj <= ii), 1.0, 0.0).astype(BF16)
    e_ref[...] = jnp.where(same & (jj % CHUNK == CHUNK - 1), 1.0, 0.0).astype(BF16)


def _col_bcast(row, width):
    return jnp.broadcast_to(row, (width, TB)).T


def _dn_kernel(q_ref, k_ref, v_ref, z_ref, b_ref, a_ref, wq_ref, wk_ref, wv_ref, alog_ref, dtb_ref, nw_ref,
               o_ref, s_ref, tq_ref, tk_ref, tv_ref, u_ref, e_ref):
    t = pl.program_id(1)

    @pl.when(t == 0)
    def _():
        s_ref[...] = jnp.zeros_like(s_ref)
        tq_ref[...] = jnp.zeros_like(tq_ref)
        tk_ref[...] = jnp.zeros_like(tk_ref)
        tv_ref[...] = jnp.zeros_like(tv_ref)
        _fill_chunk_matrices(u_ref, e_ref)

    d = DN_HEAD_DIM
    nq = q_ref.shape[1] // d
    q_all = _conv_silu(q_ref, tq_ref, wq_ref)
    k_all = _conv_silu(k_ref, tk_ref, wk_ref)
    v = _conv_silu(v_ref, tv_ref, wv_ref)
    qs, ks = [], []
    for i in range(nq):
        q = q_all[:, i * d:(i + 1) * d]
        k = k_all[:, i * d:(i + 1) * d]
        qs.append(q * (lax.rsqrt(jnp.sum(q * q, axis=-1, keepdims=True) + EPS) * (d ** -0.5)))
        ks.append(k * lax.rsqrt(jnp.sum(k * k, axis=-1, keepdims=True) + EPS))

    beta = _sigmoid(b_ref[...])
    g = -jnp.exp(_tile_lanes(alog_ref[...], TB // 128)) * _softplus(a_ref[...] + _tile_lanes(dtb_ref[...], TB // 128))
    gc = _dot_sel(g, u_ref[...])
    gl = _dot_sel(gc, e_ref[...])
    eg = jnp.exp(gc)
    ek = jnp.exp(gl - gc)

    gw = DN_GROUP * CHUNK
    lg = CHUNK.bit_length() - 1
    ii = lax.broadcasted_iota(jnp.int32, (gw, gw), 0)
    jj = lax.broadcasted_iota(jnp.int32, (gw, gw), 1)
    same = (ii >> lg) == (jj >> lg)
    causal = same & (ii >= jj)
    strict = same & (ii > jj)
    eye = jnp.where(ii == jj, 1.0, 0.0)
    masks = [strict & ((ii >> 1) == (jj >> 1))]
    blk = 2
    while blk < CHUNK:
        sh = blk.bit_length() - 1
        masks.append(((ii >> (sh + 1)) == (jj >> (sh + 1))) & (((ii >> sh) & 1) == 1) & (((jj >> sh) & 1) == 0))
        blk *= 2

    heads = range(2 * nq)
    chunks = range(CPB)
    groups = range(TB // gw)
    rows_of = [slice(c * CHUNK, (c + 1) * CHUNK) for c in chunks]
    grows = [slice(s * gw, (s + 1) * gw) for s in groups]
    kk = [[_bdot_nt(ks[i][grows[s]], ks[i][grows[s]]) for s in groups] for i in range(nq)]
    qk = [[_bdot_nt(qs[i][grows[s]], ks[i][grows[s]]) for s in groups] for i in range(nq)]

    eg_cb, rhs_all, qd_all, kd_all = [], [], [], []
    dec, a = {}, {}
    for hh in heads:
        q, k = qs[hh // 2], ks[hh // 2]
        gc_cb = _col_bcast(gc[hh:hh + 1], max(gw, DN_HEAD_DIM))
        beta_cb = _col_bcast(beta[hh:hh + 1], max(gw, DN_HEAD_DIM))
        eg_cb.append(_col_bcast(eg[hh:hh + 1], DN_HEAD_DIM))
        ek_cb = _col_bcast(ek[hh:hh + 1], DN_HEAD_DIM)
        beta_h = beta_cb[:, :DN_HEAD_DIM]
        vh = v[:, hh * DN_HEAD_DIM:(hh + 1) * DN_HEAD_DIM]
        rhs_all.append(jnp.concatenate([vh * beta_h, k * (beta_h * eg_cb[hh])], axis=1))
        qd_all.append(q * eg_cb[hh])
        kd_all.append(k * ek_cb)
        for s in groups:
            diff = gc_cb[grows[s], :gw] - gc[hh:hh + 1, s * gw:(s + 1) * gw]
            dec[hh, s] = jnp.where(causal, jnp.exp(jnp.where(causal, diff, 0.0)), 0.0)
            a[hh, s] = jnp.where(strict, beta_cb[grows[s], :gw] * kk[hh // 2][s] * dec[hh, s], 0.0)
    inst = [(hh, s) for s in groups for hh in heads]

    tinv = {i: eye - jnp.where(masks[0], a[i], 0.0) for i in inst}
    for m in masks[1:]:
        x = {i: _bdot(jnp.where(m, a[i], 0.0), tinv[i]) for i in inst}
        tinv = {i: tinv[i] - _bdot(tinv[i], x[i]) for i in inst}
    uw_g = {(hh, s): _bdot(tinv[hh, s], rhs_all[hh][grows[s]]) for hh, s in inst}
    qo_g = {(hh, s): _bdot(qk[hh // 2][s] * dec[hh, s], uw_g[hh, s]) for hh, s in inst}
    in_group = [slice((c % DN_GROUP) * CHUNK, (c % DN_GROUP + 1) * CHUNK) for c in chunks]
    uw = {(hh, c): uw_g[hh, c // DN_GROUP][in_group[c]] for c in chunks for hh in heads}
    qo = {(hh, c): qo_g[hh, c // DN_GROUP][in_group[c]] for c in chunks for hh in heads}
    nk = {(hh, c): _bdot_tn(uw[hh, c], kd_all[hh][rows_of[c]]) for c in chunks for hh in heads}

    state_t = [s_ref[hh] for hh in heads]
    before = {}
    for c in chunks:
        for hh in heads:
            before[hh, c] = state_t[hh]
            g_end = eg_cb[hh][(c + 1) * CHUNK - 1:(c + 1) * CHUNK, :]
            state_t[hh] = (state_t[hh] * g_end - _bdot(state_t[hh], nk[hh, c][DN_HEAD_DIM:])
                           + nk[hh, c][:DN_HEAD_DIM])
    nw = nw_ref[...]
    for hh in heads:
        s_ref[hh] = state_t[hh]
        outs = [_bdot_nt(qd_all[hh][rows_of[c]] - qo[hh, c][:, DN_HEAD_DIM:], before[hh, c])
                + qo[hh, c][:, :DN_HEAD_DIM] for c in chunks]
        o = jnp.concatenate(outs, axis=0)
        zh = z_ref[:, hh * DN_HEAD_DIM:(hh + 1) * DN_HEAD_DIM].astype(F32)
        o = o * lax.rsqrt(jnp.mean(o * o, axis=-1, keepdims=True) + EPS) * nw * _silu(zh)
        o_ref[:, hh * DN_HEAD_DIM:(hh + 1) * DN_HEAD_DIM] = o.astype(o_ref.dtype)


def _gated_deltanet(qkv, z, b_rows, a_rows, conv_w, alog_b, dtb_b, norm_w):
    rows = qkv.shape[0]
    nstep = DN_QK_HEADS // DN_QPS
    qw = DN_QPS * DN_HEAD_DIM
    return pl.pallas_call(
        _dn_kernel,
        grid=(nstep, rows // TB),
        in_specs=[
            pl.BlockSpec((TB, qw), lambda j, t: (t, j)),
            pl.BlockSpec((TB, qw), lambda j, t: (t, nstep + j)),
            pl.BlockSpec((TB, 2 * qw), lambda j, t: (t, nstep + j)),
            pl.BlockSpec((TB, 2 * qw), lambda j, t: (t, j)),
            pl.BlockSpec((None, 2 * DN_QPS, TB), lambda j, t: (j, 0, t)),
            pl.BlockSpec((None, 2 * DN_QPS, TB), lambda j, t: (j, 0, t)),
            pl.BlockSpec((4, qw), lambda j, t: (0, j)),
            pl.BlockSpec((4, qw), lambda j, t: (0, nstep + j)),
            pl.BlockSpec((4, 2 * qw), lambda j, t: (0, nstep + j)),
            pl.BlockSpec((None, 2 * DN_QPS, 128), lambda j, t: (j, 0, 0)),
            pl.BlockSpec((None, 2 * DN_QPS, 128), lambda j, t: (j, 0, 0)),
            pl.BlockSpec((1, DN_HEAD_DIM), lambda j, t: (0, 0)),
        ],
        out_specs=pl.BlockSpec((TB, 2 * qw), lambda j, t: (t, j)),
        out_shape=jax.ShapeDtypeStruct((rows, DN_VAL_DIM), BF16),
        scratch_shapes=[
            pltpu.VMEM((2 * DN_QPS, DN_HEAD_DIM, DN_HEAD_DIM), F32),
            pltpu.VMEM((8, qw), F32),
            pltpu.VMEM((8, qw), F32),
            pltpu.VMEM((8, 2 * qw), F32),
            pltpu.VMEM((TB, TB), BF16),
            pltpu.VMEM((TB, TB), BF16),
        ],
        compiler_params=_cparams(("arbitrary", "arbitrary")),
    )(qkv, qkv, qkv, z, b_rows, a_rows, conv_w, conv_w, conv_w, alog_b, dtb_b, norm_w)


def _ssd_kernel(x_ref, b_ref, c_ref, z_ref, dt_ref, wx_ref, wb_ref, wc_ref, bx_ref, bb_ref, bc_ref,
                alog_ref, dtb_ref, dexp_ref, nw_ref, o_ref, h_ref, tx_ref, tb_ref, tc_ref, u_ref, e_ref):
    t = pl.program_id(1)

    @pl.when(t == 0)
    def _():
        h_ref[...] = jnp.zeros_like(h_ref)
        tx_ref[...] = jnp.zeros_like(tx_ref)
        tb_ref[...] = jnp.zeros_like(tb_ref)
        tc_ref[...] = jnp.zeros_like(tc_ref)
        _fill_chunk_matrices(u_ref, e_ref)

    xh = _conv_silu(x_ref, tx_ref, wx_ref, bx_ref)
    bm = _conv_silu(b_ref, tb_ref, wb_ref, bb_ref)
    cm = _conv_silu(c_ref, tc_ref, wc_ref, bc_ref)

    nrep = TB // 128
    dt = _softplus(dt_ref[...] + _tile_lanes(dtb_ref[...], nrep))
    pos = t * TB + lax.broadcasted_iota(jnp.int32, (SSD_REP, TB), 1)
    dt = jnp.where(pos >= FRONT - N_META, dt, 0.0)
    la = -jnp.exp(_tile_lanes(alog_ref[...], nrep)) * dt
    ac = _dot_sel(la, u_ref[...])
    al = _dot_sel(ac, e_ref[...])
    dtek = dt * jnp.exp(al - ac)

    def col_bcast_heads(rowsv):
        stacked = jnp.concatenate(
            [jnp.broadcast_to(rowsv[r:r + 1], (SSD_HEAD_DIM, TB)) for r in range(SSD_REP)], axis=0)
        return stacked.T

    ac_cb = col_bcast_heads(ac)
    ea_cb = jnp.exp(ac_cb)
    xd = xh * col_bcast_heads(dtek)

    ii = lax.broadcasted_iota(jnp.int32, (CHUNK, 2 * CHUNK), 0)
    jj = lax.broadcasted_iota(jnp.int32, (CHUNK, 2 * CHUNK), 1)
    causal2 = ii >= (jj & (CHUNK - 1))
    li = lax.broadcasted_iota(jnp.int32, (CHUNK, 2 * CHUNK), 1)
    lo_half = li < CHUNK

    state = h_ref[...]
    outs = []
    for c in range(CPB):
        rows = slice(c * CHUNK, (c + 1) * CHUNK)
        cc = cm[rows]
        bc = bm[rows]
        cb = _bdot_nt(cc, bc)
        cb2 = jnp.concatenate([cb, cb], axis=1)
        y_off = _bdot(cc, state) * ea_cb[rows]
        yd = []
        for m in range(SSD_REP // 2):
            lanes = slice(m * 128, (m + 1) * 128)
            rowv = jnp.concatenate([ac[2 * m:2 * m + 1, rows], ac[2 * m + 1:2 * m + 2, rows]], axis=1)
            dtrow = jnp.concatenate([dt[2 * m:2 * m + 1, rows], dt[2 * m + 1:2 * m + 2, rows]], axis=1)
            diff = ac_cb[rows, lanes] - rowv
            seg = jnp.where(causal2, jnp.exp(jnp.where(causal2, diff, 0.0)), 0.0)
            mm = cb2 * seg * dtrow
            x2 = xh[rows, lanes]
            rhs = jnp.concatenate([jnp.where(lo_half, x2, 0.0), jnp.where(lo_half, 0.0, x2)], axis=0)
            yd.append(_bdot(mm, rhs))
        outs.append(jnp.concatenate(yd, axis=1) + y_off)
        ea_last = ea_cb[(c + 1) * CHUNK - 1:(c + 1) * CHUNK, :]
        state = state * ea_last + _bdot_tn(bc, xd[rows])
    h_ref[...] = state

    y = jnp.concatenate(outs, axis=0) + dexp_ref[...] * xh
    y = y * _silu(z_ref[...].astype(F32))
    y = y * lax.rsqrt(jnp.mean(y * y, axis=-1, keepdims=True) + EPS) * nw_ref[...]
    o_ref[...] = y.astype(o_ref.dtype)


def _mamba2_ssd(xbc, z, dt_rows, conv_w, conv_b, alog_b, dtb_b, d_exp, norm_w):
    rows = xbc.shape[0]
    gd = SSD_GROUP_DIM
    nx = SSD_D_INNER // SSD_STATE
    return pl.pallas_call(
        _ssd_kernel,
        grid=(SSD_GROUPS, rows // TB),
        in_specs=[
            pl.BlockSpec((TB, gd), lambda g, t: (t, g)),
            pl.BlockSpec((TB, SSD_STATE), lambda g, t: (t, nx + g)),
            pl.BlockSpec((TB, SSD_STATE), lambda g, t: (t, nx + SSD_GROUPS + g)),
            pl.BlockSpec((TB, gd), lambda g, t: (t, g)),
            pl.BlockSpec((None, SSD_REP, TB), lambda g, t: (g, 0, t)),
            pl.BlockSpec((4, gd), lambda g, t: (0, g)),
            pl.BlockSpec((4, SSD_STATE), lambda g, t: (0, nx + g)),
            pl.BlockSpec((4, SSD_STATE), lambda g, t: (0, nx + SSD_GROUPS + g)),
            pl.BlockSpec((1, gd), lambda g, t: (0, g)),
            pl.BlockSpec((1, SSD_STATE), lambda g, t: (0, nx + g)),
            pl.BlockSpec((1, SSD_STATE), lambda g, t: (0, nx + SSD_GROUPS + g)),
            pl.BlockSpec((None, SSD_REP, 128), lambda g, t: (g, 0, 0)),
            pl.BlockSpec((None, SSD_REP, 128), lambda g, t: (g, 0, 0)),
            pl.BlockSpec((1, gd), lambda g, t: (0, g)),
            pl.BlockSpec((1, gd), lambda g, t: (0, g)),
        ],
        out_specs=pl.BlockSpec((TB, gd), lambda g, t: (t, g)),
        out_shape=jax.ShapeDtypeStruct((rows, SSD_D_INNER), BF16),
        scratch_shapes=[
            pltpu.VMEM((SSD_STATE, gd), F32),
            pltpu.VMEM((8, gd), F32),
            pltpu.VMEM((8, SSD_STATE), F32),
            pltpu.VMEM((8, SSD_STATE), F32),
            pltpu.VMEM((TB, TB), BF16),
            pltpu.VMEM((TB, TB), BF16),
        ],
        compiler_params=_cparams(("arbitrary", "arbitrary")),
    )(xbc, xbc, xbc, z, dt_rows, conv_w, conv_w, conv_w, conv_b, conv_b, conv_b, alog_b, dtb_b, d_exp, norm_w)


def _merge_kernel(ydn_ref, yssd_ref, gd_ref, gs_ref, wdn_ref, wssd_ref, o_ref):
    a = jnp.dot(ydn_ref[...], wdn_ref[...], preferred_element_type=F32)
    b = jnp.dot(yssd_ref[...], wssd_ref[...], preferred_element_type=F32)
    o = _sigmoid(gd_ref[...].astype(F32)) * a + _sigmoid(gs_ref[...].astype(F32)) * b
    o_ref[...] = o.astype(o_ref.dtype)


def _merge(y_dn, y_ssd, gates, w_dn, w_ssd, tn=512):
    ntile = (y_dn.shape[0] - FRONT) // TB
    off = FRONT // TB
    ng = D_MODEL // tn
    return pl.pallas_call(
        _merge_kernel,
        grid=(ng, ntile),
        in_specs=[
            pl.BlockSpec((TB, DN_VAL_DIM), lambda j, i: (i + off, 0)),
            pl.BlockSpec((TB, SSD_D_INNER), lambda j, i: (i + off, 0)),
            pl.BlockSpec((TB, tn), lambda j, i: (i + off, j)),
            pl.BlockSpec((TB, tn), lambda j, i: (i + off, ng + j)),
            pl.BlockSpec((DN_VAL_DIM, tn), lambda j, i: (0, j)),
            pl.BlockSpec((SSD_D_INNER, tn), lambda j, i: (0, j)),
        ],
        out_specs=pl.BlockSpec((TB, tn), lambda j, i: (i, j)),
        out_shape=jax.ShapeDtypeStruct((ntile * TB, D_MODEL), BF16),
        compiler_params=_cparams(("arbitrary", "arbitrary")),
    )(y_dn, y_ssd, gates, gates, w_dn, w_ssd)


def _out_router_kernel(m_ref, wo_ref, x_ref, nw_ref, wr_ref, br_ref,
                       h2_ref, hn_ref, idx_ref, p_ref, rank_ref, cnt_ref, carry_ref, us_ref):
    i = pl.program_id(0)
    rt = x_ref.shape[0]

    @pl.when(i == 0)
    def _():
        carry_ref[...] = jnp.zeros_like(carry_ref)
        jj = lax.broadcasted_iota(jnp.int32, (rt, rt), 0)
        ii = lax.broadcasted_iota(jnp.int32, (rt, rt), 1)
        us_ref[...] = jnp.where(jj < ii, 1.0, 0.0).astype(BF16)

    h2 = x_ref[...] + jnp.dot(m_ref[...], wo_ref[...], preferred_element_type=F32)
    h2_ref[...] = h2
    hn = h2 * lax.rsqrt(jnp.mean(h2 * h2, axis=-1, keepdims=True) + EPS) * nw_ref[...]
    hn_ref[...] = hn

    lg = _dot3(wr_ref[...], hn, nt=True) + br_ref[...]
    eio = lax.broadcasted_iota(jnp.int32, (N_EXPERTS, rt), 0)
    vals, hits, idxs = [], [], []
    cur = lg
    for _ in range(TOP_K):
        m = jnp.max(cur, axis=0, keepdims=True)
        ik = jnp.min(jnp.where(cur == m, eio, N_EXPERTS), axis=0, keepdims=True)
        hit = eio == ik
        vals.append(m)
        idxs.append(ik)
        hits.append(hit)
        cur = jnp.where(hit, -jnp.inf, cur)
    ex = [jnp.exp(vk - vals[0]) for vk in vals]
    den = ex[0] + ex[1] + ex[2] + ex[3]
    idx_ref[...] = jnp.concatenate(idxs, axis=0)
    p_ref[...] = jnp.concatenate([e / den for e in ex], axis=0)

    sel = jnp.zeros((N_EXPERTS, rt), F32)
    for hit in hits:
        sel = sel + jnp.where(hit, 1.0, 0.0)
    before = jnp.dot(sel.astype(BF16), us_ref[...], preferred_element_type=F32) \
        + _tile_lanes(carry_ref[...], rt // 128)
    ranks = [jnp.sum(jnp.where(hit, before, 0.0), axis=0, keepdims=True) for hit in hits]
    rank_ref[...] = jnp.concatenate(ranks, axis=0).astype(jnp.int32)
    carry = carry_ref[...] + jnp.dot(sel.astype(BF16), jnp.ones((rt, 128), BF16), preferred_element_type=F32)
    carry_ref[...] = carry
    cnt_ref[...] = carry


def _out_router(merged, w_out, x, nw, w_router_t, b_router_c):
    n_tok = x.shape[0]
    rt = ROUTER_RT
    tok_spec = pl.BlockSpec((TOP_K, rt), lambda i: (0, i))
    return pl.pallas_call(
        _out_router_kernel,
        grid=(n_tok // rt,),
        in_specs=[
            pl.BlockSpec((rt, D_MODEL), lambda i: (i, 0)),
            pl.BlockSpec((D_MODEL, D_MODEL), lambda i: (0, 0)),
            pl.BlockSpec((rt, D_MODEL), lambda i: (i, 0)),
            pl.BlockSpec((1, D_MODEL), lambda i: (0, 0)),
            pl.BlockSpec((N_EXPERTS, D_MODEL), lambda i: (0, 0)),
            pl.BlockSpec((N_EXPERTS, 1), lambda i: (0, 0)),
        ],
        out_specs=[
            pl.BlockSpec((rt, D_MODEL), lambda i: (i, 0)),
            pl.BlockSpec((rt, D_MODEL), lambda i: (i, 0)),
            tok_spec, tok_spec, tok_spec,
            pl.BlockSpec((N_EXPERTS, 128), lambda i: (0, 0)),
        ],
        out_shape=[
            jax.ShapeDtypeStruct((n_tok, D_MODEL), F32),
            jax.ShapeDtypeStruct((n_tok, D_MODEL), F32),
            jax.ShapeDtypeStruct((TOP_K, n_tok), jnp.int32),
            jax.ShapeDtypeStruct((TOP_K, n_tok), F32),
            jax.ShapeDtypeStruct((TOP_K, n_tok), jnp.int32),
            jax.ShapeDtypeStruct((N_EXPERTS, 128), F32),
        ],
        scratch_shapes=[pltpu.VMEM((N_EXPERTS, 128), F32), pltpu.VMEM((rt, rt), BF16)],
        compiler_params=_cparams(("arbitrary",)),
    )(merged, w_out, x, nw, w_router_t, b_router_c)


def _gather_kernel(nused_ref, tok_ref, src_ref, o_ref, buf_ref, sem):
    b = pl.program_id(0)
    nu = nused_ref[0]

    def issue(blk, slot):
        base = blk * MOE_BM

        def body(r, carry):
            pltpu.make_async_copy(src_ref.at[pl.ds(tok_ref[base + r], 1)], buf_ref.at[slot, pl.ds(r, 1)],
                                  sem.at[slot]).start()
            return carry

        lax.fori_loop(0, MOE_BM, body, 0, unroll=8)

    @pl.when(b == 0)
    def _():
        issue(0, 0)

    @pl.when(b + 1 < nu)
    def _():
        issue(b + 1, (b + 1) % 2)

    @pl.when(b < nu)
    def _():
        slot = b % 2
        pltpu.make_async_copy(src_ref.at[pl.ds(0, MOE_BM)], buf_ref.at[slot], sem.at[slot]).wait()
        o_ref[...] = buf_ref[slot].astype(o_ref.dtype)

    @pl.when(b >= nu)
    def _():
        o_ref[...] = jnp.zeros_like(o_ref)


def _gather_rows(n_used, row_tok, src):
    nb = row_tok.shape[0] // MOE_BM
    return pl.pallas_call(
        _gather_kernel,
        grid_spec=pltpu.PrefetchScalarGridSpec(
            num_scalar_prefetch=2,
            grid=(nb,),
            in_specs=[pl.BlockSpec(memory_space=pl.ANY)],
            out_specs=pl.BlockSpec((MOE_BM, D_MODEL), lambda b, nu, tok: (b, 0)),
            scratch_shapes=[pltpu.VMEM((2, MOE_BM, D_MODEL), F32), pltpu.SemaphoreType.DMA((2,))],
        ),
        out_shape=jax.ShapeDtypeStruct((nb * MOE_BM, D_MODEL), BF16),
        compiler_params=_cparams(("arbitrary",)),
    )(n_used, row_tok, src)


def _expert_row_loop(first_blk, n, cols, src_ref, dst_ref, xbuf, obuf, in_sem, out_sem, compute):
    def rows(i):
        return pl.ds(pl.multiple_of((first_blk + i) * MOE_BM, MOE_BM), MOE_BM)

    def xcopy(i, slot):
        return pltpu.make_async_copy(src_ref.at[rows(i)], xbuf.at[slot], in_sem.at[slot])

    def ocopy(i, slot):
        dst = dst_ref.at[rows(i)] if cols is None else dst_ref.at[rows(i), cols]
        return pltpu.make_async_copy(obuf.at[slot], dst, out_sem.at[slot])

    @pl.when(n > 0)
    def _():
        xcopy(0, 0).start()

        def body(i, carry):
            slot = i % 2

            @pl.when(i + 1 < n)
            def _():
                xcopy(i + 1, 1 - slot).start()

            xcopy(i, slot).wait()

            @pl.when(i >= 2)
            def _():
                ocopy(i - 2, slot).wait()

            obuf[slot] = compute(xbuf[slot]).astype(obuf.dtype)
            ocopy(i, slot).start()
            return carry

        lax.fori_loop(0, n, body, 0)

        @pl.when(n >= 2)
        def _():
            ocopy(n - 2, n % 2).wait()

        ocopy(n - 1, (n - 1) % 2).wait()


def _zero_unused_blocks(n_used, cols, dst_ref, obuf, out_sem):
    nb = dst_ref.shape[0] // MOE_BM
    obuf[0] = jnp.zeros(obuf.shape[1:], obuf.dtype)

    def body(i, carry):
        r = pl.ds(pl.multiple_of(i * MOE_BM, MOE_BM), MOE_BM)
        dst = dst_ref.at[r] if cols is None else dst_ref.at[r, cols]
        cp = pltpu.make_async_copy(obuf.at[0], dst, out_sem.at[0])
        cp.start()
        cp.wait()
        return carry

    lax.fori_loop(n_used, nb, body, 0)


def _expert_up_kernel(first_ref, nblk_ref, nused_ref, xs_ref, wg_ref, wu_ref, bg_ref, bu_ref, act_ref,
                      xbuf, obuf, wgb_ref, wub_ref, in_sem, out_sem):
    e = pl.program_id(0)
    j = pl.program_id(1)
    tn = wgb_ref.shape[1]
    wgb_ref[...] = wg_ref[...].astype(BF16)
    wub_ref[...] = wu_ref[...].astype(BF16)

    def compute(x):
        gt = jnp.dot(x, wgb_ref[...], preferred_element_type=F32) + bg_ref[...]
        up = jnp.dot(x, wub_ref[...], preferred_element_type=F32) + bu_ref[...]
        gt = jnp.minimum(gt, SWIGLU_LIMIT)
        up = jnp.clip(up, -SWIGLU_LIMIT, SWIGLU_LIMIT)
        return gt * _sigmoid(SWIGLU_ALPHA * gt) * (up + 1.0)

    cols = pl.ds(pl.multiple_of(j * tn, tn), tn)
    _expert_row_loop(first_ref[e], nblk_ref[e], cols, xs_ref, act_ref, xbuf, obuf, in_sem, out_sem, compute)

    @pl.when(e == N_EXPERTS - 1)
    def _():
        _zero_unused_blocks(nused_ref[0], cols, act_ref, obuf, out_sem)


def _expert_up(first_blk, n_blk, n_used, xs, w_gate, w_up, b_gate, b_up):
    tn = MOE_TN_UP
    wspec = pl.BlockSpec((None, D_MODEL, tn), lambda e, j, *_: (e, 0, j))
    bspec = pl.BlockSpec((None, 1, tn), lambda e, j, *_: (e, 0, j))
    return pl.pallas_call(
        _expert_up_kernel,
        grid_spec=pltpu.PrefetchScalarGridSpec(
            num_scalar_prefetch=3,
            grid=(N_EXPERTS, D_MODEL // tn),
            in_specs=[pl.BlockSpec(memory_space=pl.ANY), wspec, wspec, bspec, bspec],
            out_specs=pl.BlockSpec(memory_space=pl.ANY),
            scratch_shapes=[pltpu.VMEM((2, MOE_BM, D_MODEL), BF16), pltpu.VMEM((2, MOE_BM, tn), BF16),
                            pltpu.VMEM((D_MODEL, tn), BF16), pltpu.VMEM((D_MODEL, tn), BF16),
                            pltpu.SemaphoreType.DMA((2,)), pltpu.SemaphoreType.DMA((2,))],
        ),
        out_shape=jax.ShapeDtypeStruct(xs.shape, BF16),
        compiler_params=_cparams(("arbitrary", "arbitrary")),
    )(first_blk, n_blk, n_used, xs, w_gate, w_up, b_gate, b_up)


def _expert_down_kernel(first_ref, nblk_ref, nused_ref, act_ref, wd_ref, bd_ref, y_ref,
                        xbuf, obuf, wdb_ref, in_sem, out_sem):
    e = pl.program_id(0)
    wdb_ref[...] = wd_ref[...].astype(BF16)

    def compute(a):
        return jnp.dot(a, wdb_ref[...], preferred_element_type=F32) + bd_ref[...]

    _expert_row_loop(first_ref[e], nblk_ref[e], None, act_ref, y_ref, xbuf, obuf, in_sem, out_sem, compute)

    @pl.when(e == N_EXPERTS - 1)
    def _():
        _zero_unused_blocks(nused_ref[0], None, y_ref, obuf, out_sem)


def _expert_down(first_blk, n_blk, n_used, act, w_down, b_down):
    return pl.pallas_call(
        _expert_down_kernel,
        grid_spec=pltpu.PrefetchScalarGridSpec(
            num_scalar_prefetch=3,
            grid=(N_EXPERTS,),
            in_specs=[pl.BlockSpec(memory_space=pl.ANY),
                      pl.BlockSpec((None, D_MODEL, D_MODEL), lambda e, *_: (e, 0, 0)),
                      pl.BlockSpec((None, 1, D_MODEL), lambda e, *_: (e, 0, 0))],
            out_specs=pl.BlockSpec(memory_space=pl.ANY),
            scratch_shapes=[pltpu.VMEM((2, MOE_BM, D_MODEL), BF16), pltpu.VMEM((2, MOE_BM, D_MODEL), F32),
                            pltpu.VMEM((D_MODEL, D_MODEL), BF16),
                            pltpu.SemaphoreType.DMA((2,)), pltpu.SemaphoreType.DMA((2,))],
        ),
        out_shape=jax.ShapeDtypeStruct(act.shape, F32),
        compiler_params=_cparams(("arbitrary",)),
    )(first_blk, n_blk, n_used, act, w_down, b_down)


def _combine_kernel(dest_ref, h2_ref, p_ref, fw_ref, y_ref, o_ref, buf_ref, sem):
    i = pl.program_id(0)
    tt = COMB_TT

    def issue(tile, slot):
        base = tile * (TOP_K * tt)

        def body(r, carry):
            for k in range(TOP_K):
                pltpu.make_async_copy(y_ref.at[pl.ds(dest_ref[base + k * tt + r], 1)],
                                      buf_ref.at[slot, k, pl.ds(r, 1)], sem.at[slot]).start()
            return carry

        lax.fori_loop(0, tt, body, 0, unroll=4)

    @pl.when(i == 0)
    def _():
        issue(0, 0)

    @pl.when(i + 1 < pl.num_programs(0))
    def _():
        issue(i + 1, (i + 1) % 2)

    slot = i % 2
    for k in range(TOP_K):
        pltpu.make_async_copy(y_ref.at[pl.ds(0, tt)], buf_ref.at[slot, k], sem.at[slot]).wait()
    p = p_ref[...]
    acc = h2_ref[...]
    for k in range(TOP_K):
        acc = acc + p[:, k:k + 1] * buf_ref[slot, k]
    o_ref[...] = acc * lax.rsqrt(jnp.mean(acc * acc, axis=-1, keepdims=True) + EPS) * fw_ref[...]


def _combine(dest, h2, p_cols, final_w, y_rows):
    n_tok = h2.shape[0]
    tt = COMB_TT
    return pl.pallas_call(
        _combine_kernel,
        grid_spec=pltpu.PrefetchScalarGridSpec(
            num_scalar_prefetch=1,
            grid=(n_tok // tt,),
            in_specs=[
                pl.BlockSpec((tt, D_MODEL), lambda i, d: (i, 0)),
                pl.BlockSpec((tt, TOP_K), lambda i, d: (i, 0)),
                pl.BlockSpec((1, D_MODEL), lambda i, d: (0, 0)),
                pl.BlockSpec(memory_space=pl.ANY),
            ],
            out_specs=pl.BlockSpec((tt, D_MODEL), lambda i, d: (i, 0)),
            scratch_shapes=[pltpu.VMEM((2, TOP_K, tt, D_MODEL), F32), pltpu.SemaphoreType.DMA((2,))],
        ),
        out_shape=jax.ShapeDtypeStruct((n_tok, D_MODEL), F32),
        compiler_params=_cparams(("arbitrary",)),
    )(dest, h2, p_cols, final_w, y_rows)


def _mixer(h_ext, norm_w, w_in, dn_conv_w, dn_a_log, dn_dt_bias, dn_norm_w,
           ssd_conv_w, ssd_conv_b, ssd_a_log, ssd_dt_bias, ssd_d, ssd_norm_w, w_proj_dn, w_proj_ssd):
    o_qkv = 0
    o_dnz = o_qkv + DN_CONV_DIM
    o_b = o_dnz + DN_VAL_DIM
    o_a = o_b + DN_V_HEADS
    o_sz = o_a + DN_V_HEADS
    o_xbc = o_sz + SSD_D_INNER
    o_dt = o_xbc + SSD_CONV_DIM
    o_gate = o_dt + SSD_HEADS
    d_in = o_gate + 2 * D_MODEL
    w_small = jnp.concatenate([w_in[:, o_b:o_sz], w_in[:, o_dt:o_gate]], axis=1)

    hn, small = _norm_small(h_ext, norm_w.reshape(1, D_MODEL), w_small)
    qkv = _matmul(hn, w_in[:, o_qkv:o_dnz].astype(BF16), BF16)
    dn_z = _matmul(hn, w_in[:, o_dnz:o_b].astype(BF16), BF16)
    ssd_z = _matmul(hn, w_in[:, o_sz:o_xbc].astype(BF16), BF16)
    xbc = _matmul(hn, w_in[:, o_xbc:o_dt].astype(BF16), BF16)
    gates = _matmul(hn, w_in[:, o_gate:d_in].astype(BF16), BF16)

    rows = h_ext.shape[0]
    sm_t = small.T
    dn_steps, dn_vps = DN_QK_HEADS // DN_QPS, 2 * DN_QPS
    b_rows = sm_t[0:DN_V_HEADS].reshape(dn_steps, dn_vps, rows)
    a_rows = sm_t[DN_V_HEADS:2 * DN_V_HEADS].reshape(dn_steps, dn_vps, rows)
    dt_rows = sm_t[2 * DN_V_HEADS:].reshape(SSD_GROUPS, SSD_REP, rows)

    bc = lambda p, g, r: jnp.broadcast_to(p.astype(F32).reshape(g, r, 1), (g, r, 128))
    y_dn = _gated_deltanet(qkv, dn_z, b_rows, a_rows, dn_conv_w,
                           bc(dn_a_log, dn_steps, dn_vps), bc(dn_dt_bias, dn_steps, dn_vps),
                           dn_norm_w.reshape(1, DN_HEAD_DIM))
    y_ssd = _mamba2_ssd(xbc, ssd_z, dt_rows, ssd_conv_w, ssd_conv_b.reshape(1, SSD_CONV_DIM),
                        bc(ssd_a_log, SSD_GROUPS, SSD_REP), bc(ssd_dt_bias, SSD_GROUPS, SSD_REP),
                        jnp.repeat(ssd_d, SSD_HEAD_DIM).reshape(1, SSD_D_INNER),
                        ssd_norm_w.reshape(1, SSD_D_INNER))
    return _merge(y_dn, y_ssd, gates, w_proj_dn.astype(BF16), w_proj_ssd.astype(BF16))


def _moe_and_final(merged, x2, w_out, norm_ffn_w, w_router, b_router, w_gate, b_gate, w_up, b_up,
                   w_down, b_down, final_norm_w):
    n_tok = x2.shape[0]
    h2, hn2, idx, probs, rank, cnt = _out_router(
        merged, w_out.astype(BF16), x2, norm_ffn_w.reshape(1, D_MODEL), w_router.T,
        b_router.reshape(N_EXPERTS, 1))

    counts = cnt[:, 0].astype(jnp.int32)
    padded = (counts + MOE_BM - 1) // MOE_BM * MOE_BM
    pad_end = jnp.cumsum(padded)
    pad_start = pad_end - padded
    hit = idx[:, :, None] == jnp.arange(N_EXPERTS, dtype=jnp.int32)
    dest = jnp.sum(jnp.where(hit, pad_start.astype(jnp.int32), 0), axis=-1) + rank
    nb = -(-(n_tok * TOP_K + N_EXPERTS * (MOE_BM - 1)) // MOE_BM)
    tok_ids = jnp.broadcast_to(jnp.arange(n_tok, dtype=jnp.int32)[None], (TOP_K, n_tok))
    row_tok = jnp.zeros((nb * MOE_BM,), jnp.int32).at[dest.reshape(-1)].set(tok_ids.reshape(-1))
    first_blk = (pad_start // MOE_BM).astype(jnp.int32)
    n_blk = (padded // MOE_BM).astype(jnp.int32)
    n_used = (pad_end[-1:] // MOE_BM).astype(jnp.int32)

    xs = _gather_rows(n_used, row_tok, hn2)
    act = _expert_up(first_blk, n_blk, n_used, xs, w_gate, w_up,
                     b_gate.reshape(N_EXPERTS, 1, D_MODEL), b_up.reshape(N_EXPERTS, 1, D_MODEL))
    y_rows = _expert_down(first_blk, n_blk, n_used, act, w_down, b_down.reshape(N_EXPERTS, 1, D_MODEL))
    dest_t = dest.reshape(TOP_K, n_tok // COMB_TT, COMB_TT).transpose(1, 0, 2).reshape(-1)
    return _combine(dest_t, h2, probs.T, final_norm_w.reshape(1, D_MODEL), y_rows)


def kernel(x, meta_tokens, norm_mix_w, w_in, dn_conv_w, dn_a_log, dn_dt_bias, dn_norm_w, ssd_conv_w, ssd_conv_b, ssd_a_log, ssd_dt_bias, ssd_d, ssd_norm_w, w_proj_dn, w_proj_ssd, w_out, norm_ffn_w, w_router, b_router, w_gate, b_gate, w_up, b_up, w_down, b_down, final_norm_w):
    bsz, seq, d = x.shape
    assert bsz == 1 and d == D_MODEL and seq % TB == 0
    assert norm_mix_w.shape[0] == 1, "single-layer block"
    x2 = x.reshape(seq, d)
    h_ext = jnp.concatenate([jnp.zeros((FRONT - N_META, d), x.dtype), meta_tokens.astype(x.dtype), x2], axis=0)
    merged = _mixer(h_ext, norm_mix_w[0], w_in[0], dn_conv_w[0], dn_a_log[0], dn_dt_bias[0], dn_norm_w[0],
                    ssd_conv_w[0], ssd_conv_b[0], ssd_a_log[0], ssd_dt_bias[0], ssd_d[0], ssd_norm_w[0],
                    w_proj_dn[0], w_proj_ssd[0])
    out = _moe_and_final(merged, x2, w_out[0], norm_ffn_w[0], w_router[0], b_router[0], w_gate[0], b_gate[0],
                         w_up[0], b_up[0], w_down[0], b_down[0], final_norm_w)
    return out.reshape(bsz, seq, d)
```

```python
import functools

import jax
import jax.numpy as jnp
from jax import lax
from jax.experimental import pallas as pl
from jax.experimental.pallas import tpu as pltpu

F32 = jnp.float32
BF16 = jnp.bfloat16

D_MODEL = 2048
N_META = 16
CHUNK = 64
EPS = 1e-6

DN_QK_HEADS = 16
DN_V_HEADS = 32
DN_HEAD_DIM = 128
DN_KEY_DIM = DN_QK_HEADS * DN_HEAD_DIM
DN_VAL_DIM = DN_V_HEADS * DN_HEAD_DIM
DN_CONV_DIM = 2 * DN_KEY_DIM + DN_VAL_DIM

SSD_D_INNER = 2 * D_MODEL
SSD_HEAD_DIM = 64
SSD_HEADS = SSD_D_INNER // SSD_HEAD_DIM
SSD_GROUPS = 8
SSD_REP = SSD_HEADS // SSD_GROUPS
SSD_STATE = 128
SSD_GS = SSD_GROUPS * SSD_STATE
SSD_CONV_DIM = SSD_D_INNER + 2 * SSD_GS
SSD_GROUP_DIM = SSD_D_INNER // SSD_GROUPS

N_EXPERTS = 32
TOP_K = 4
SWIGLU_LIMIT = 7.0
SWIGLU_ALPHA = 1.702

TB = 512
FRONT = TB
CPB = TB // CHUNK
MOE_BM = 512
MOE_TN_UP = 512
MOE_TN_DOWN = 1024
COMB_TT = 256
ROUTER_RT = 512
DN_GROUP = 2
DN_QPS = 4
VMEM_LIMIT = 56 * 1024 * 1024


def _cparams(sem):
    return pltpu.CompilerParams(dimension_semantics=sem, vmem_limit_bytes=VMEM_LIMIT)


def _bdot(a, b):
    return jnp.dot(a.astype(BF16), b.astype(BF16), preferred_element_type=F32)


def _bdot_nt(a, b):
    return lax.dot_general(a.astype(BF16), b.astype(BF16), (((1,), (1,)), ((), ())),
                           preferred_element_type=F32)


def _bdot_tn(a, b):
    return lax.dot_general(a.astype(BF16), b.astype(BF16), (((0,), (0,)), ((), ())),
                           preferred_element_type=F32)


def _split3(x):
    x1 = x.astype(BF16)
    r1 = x - x1.astype(F32)
    x2 = r1.astype(BF16)
    x3 = (r1 - x2.astype(F32)).astype(BF16)
    return x1, x2, x3


def _dot_sel(x, m):
    x1, x2, x3 = _split3(x)
    d = lambda a: jnp.dot(a, m, preferred_element_type=F32)
    return d(x1) + d(x2) + d(x3)


def _dot3(a, b, nt=False):
    a1 = a.astype(BF16)
    a2 = (a - a1.astype(F32)).astype(BF16)
    b1 = b.astype(BF16)
    b2 = (b - b1.astype(F32)).astype(BF16)
    if nt:
        d = lambda p, q: lax.dot_general(p, q, (((1,), (1,)), ((), ())), preferred_element_type=F32)
    else:
        d = lambda p, q: jnp.dot(p, q, preferred_element_type=F32)
    return d(a1, b1) + d(a2, b1) + d(a1, b2)


def _sigmoid(x):
    return 1.0 / (1.0 + jnp.exp(-x))


def _silu(x):
    return x * _sigmoid(x)


def _softplus(x):
    return jnp.maximum(x, 0.0) + jnp.log(1.0 + jnp.exp(-jnp.abs(x)))


def _tile_lanes(p, n):
    return jnp.concatenate([p] * n, axis=1)


def _norm_small_kernel(x_ref, meta_ref, nw_ref, ws_ref, hn_ref, sm_ref):
    i = pl.program_id(0)

    def emit(h):
        y = h * lax.rsqrt(jnp.mean(h * h, axis=-1, keepdims=True) + EPS) * nw_ref[...]
        hn_ref[...] = y.astype(BF16)
        sm_ref[...] = _dot3(y, ws_ref[...])

    @pl.when(i == 0)
    def _():
        emit(jnp.concatenate([jnp.zeros((FRONT - N_META, D_MODEL), F32), meta_ref[...]], axis=0))

    @pl.when(i > 0)
    def _():
        emit(x_ref[...])


def _norm_small(x2, meta, nw, w_small):
    rows = FRONT + x2.shape[0]
    return pl.pallas_call(
        _norm_small_kernel,
        grid=(rows // TB,),
        in_specs=[pl.BlockSpec((TB, D_MODEL), lambda i: (jnp.maximum(i - 1, 0), 0)),
                  pl.BlockSpec((N_META, D_MODEL), lambda i: (0, 0)),
                  pl.BlockSpec((1, D_MODEL), lambda i: (0, 0)),
                  pl.BlockSpec((D_MODEL, 128), lambda i: (0, 0))],
        out_specs=[pl.BlockSpec((TB, D_MODEL), lambda i: (i, 0)),
                   pl.BlockSpec((TB, 128), lambda i: (i, 0))],
        out_shape=[jax.ShapeDtypeStruct((rows, D_MODEL), BF16),
                   jax.ShapeDtypeStruct((rows, 128), F32)],
        compiler_params=_cparams(("arbitrary",)),
    )(x2, meta, nw, w_small)


def _mm_kernel(x_ref, w_ref, o_ref):
    o_ref[...] = jnp.dot(x_ref[...], w_ref[...], preferred_element_type=F32).astype(o_ref.dtype)


def _matmul(x, w, out_dtype, tm=1024, tn=1024):
    m, k = x.shape
    n = w.shape[1]
    return pl.pallas_call(
        _mm_kernel,
        grid=(n // tn, pl.cdiv(m, tm)),
        in_specs=[pl.BlockSpec((tm, k), lambda j, i: (i, 0)),
                  pl.BlockSpec((k, tn), lambda j, i: (0, j))],
        out_specs=pl.BlockSpec((tm, tn), lambda j, i: (i, j)),
        out_shape=jax.ShapeDtypeStruct((m, n), out_dtype),
        compiler_params=_cparams(("arbitrary", "arbitrary")),
    )(x, w)


def _conv_silu(raw_ref, tail_ref, w_ref, b_ref=None):
    x = raw_ref[...].astype(F32)
    rows = x.shape[0]
    ext = jnp.concatenate([tail_ref[...], x], axis=0)
    w = w_ref[...]
    y = x * w[3:4]
    for j in (1, 2, 3):
        y = y + pltpu.roll(ext, j, 0)[8:8 + rows] * w[3 - j:4 - j]
    tail_ref[...] = x[rows - 8:rows]
    if b_ref is not None:
        y = y + b_ref[...]
    return _silu(y)


def _fill_chunk_matrices(u_ref, e_ref):
    jj = lax.broadcasted_iota(jnp.int32, (TB, TB), 0)
    ii = lax.broadcasted_iota(jnp.int32, (TB, TB), 1)
    same = (jj // CHUNK) == (ii // CHUNK)
    u_ref[...] = jnp.where(same & (jj <= ii), 1.0, 0.0).astype(BF16)
    e_ref[...] = jnp.where(same & (jj % CHUNK == CHUNK - 1), 1.0, 0.0).astype(BF16)


def _col_bcast(row, width):
    return jnp.broadcast_to(row, (width, TB)).T


def _dn_kernel(q_ref, k_ref, v_ref, z_ref, b_ref, a_ref, wq_ref, wk_ref, wv_ref, alog_ref, dtb_ref, nw_ref,
               o_ref, s_ref, tq_ref, tk_ref, tv_ref, u_ref, e_ref):
    t = pl.program_id(1)

    @pl.when(t == 0)
    def _():
        s_ref[...] = jnp.zeros_like(s_ref)
        tq_ref[...] = jnp.zeros_like(tq_ref)
        tk_ref[...] = jnp.zeros_like(tk_ref)
        tv_ref[...] = jnp.zeros_like(tv_ref)
        _fill_chunk_matrices(u_ref, e_ref)

    d = DN_HEAD_DIM
    nq = q_ref.shape[1] // d
    q_all = _conv_silu(q_ref, tq_ref, wq_ref)
    k_all = _conv_silu(k_ref, tk_ref, wk_ref)
    v = _conv_silu(v_ref, tv_ref, wv_ref)
    qs, ks = [], []
    for i in range(nq):
        q = q_all[:, i * d:(i + 1) * d]
        k = k_all[:, i * d:(i + 1) * d]
        qs.append(q * (lax.rsqrt(jnp.sum(q * q, axis=-1, keepdims=True) + EPS) * (d ** -0.5)))
        ks.append(k * lax.rsqrt(jnp.sum(k * k, axis=-1, keepdims=True) + EPS))

    beta = _sigmoid(b_ref[...])
    g = -jnp.exp(_tile_lanes(alog_ref[...], TB // 128)) * _softplus(a_ref[...] + _tile_lanes(dtb_ref[...], TB // 128))
    gc = _dot_sel(g, u_ref[...])
    gl = _dot_sel(gc, e_ref[...])
    eg = jnp.exp(gc)
    ek = jnp.exp(gl - gc)

    gw = DN_GROUP * CHUNK
    lg = CHUNK.bit_length() - 1
    ii = lax.broadcasted_iota(jnp.int32, (gw, gw), 0)
    jj = lax.broadcasted_iota(jnp.int32, (gw, gw), 1)
    same = (ii >> lg) == (jj >> lg)
    causal = same & (ii >= jj)
    strict = same & (ii > jj)
    eye = jnp.where(ii == jj, 1.0, 0.0)
    masks = [strict & ((ii >> 1) == (jj >> 1))]
    blk = 2
    while blk < CHUNK:
        sh = blk.bit_length() - 1
        masks.append(((ii >> (sh + 1)) == (jj >> (sh + 1))) & (((ii >> sh) & 1) == 1) & (((jj >> sh) & 1) == 0))
        blk *= 2

    heads = range(2 * nq)
    chunks = range(CPB)
    groups = range(TB // gw)
    rows_of = [slice(c * CHUNK, (c + 1) * CHUNK) for c in chunks]
    grows = [slice(s * gw, (s + 1) * gw) for s in groups]
    kk = [[_bdot_nt(ks[i][grows[s]], ks[i][grows[s]]) for s in groups] for i in range(nq)]
    qk = [[_bdot_nt(qs[i][grows[s]], ks[i][grows[s]]) for s in groups] for i in range(nq)]

    eg_cb, rhs_all, qd_all, kd_all = [], [], [], []
    dec, a = {}, {}
    for hh in heads:
        q, k = qs[hh // 2], ks[hh // 2]
        gc_cb = _col_bcast(gc[hh:hh + 1], max(gw, DN_HEAD_DIM))
        beta_cb = _col_bcast(beta[hh:hh + 1], max(gw, DN_HEAD_DIM))
        eg_cb.append(_col_bcast(eg[hh:hh + 1], DN_HEAD_DIM))
        ek_cb = _col_bcast(ek[hh:hh + 1], DN_HEAD_DIM)
        beta_h = beta_cb[:, :DN_HEAD_DIM]
        vh = v[:, hh * DN_HEAD_DIM:(hh + 1) * DN_HEAD_DIM]
        rhs_all.append(jnp.concatenate([vh * beta_h, k * (beta_h * eg_cb[hh])], axis=1))
        qd_all.append(q * eg_cb[hh])
        kd_all.append(k * ek_cb)
        for s in groups:
            diff = gc_cb[grows[s], :gw] - gc[hh:hh + 1, s * gw:(s + 1) * gw]
            dec[hh, s] = jnp.where(causal, jnp.exp(jnp.where(causal, diff, 0.0)), 0.0)
            a[hh, s] = jnp.where(strict, beta_cb[grows[s], :gw] * kk[hh // 2][s] * dec[hh, s], 0.0)
    inst = [(hh, s) for s in groups for hh in heads]

    tinv = {i: eye - jnp.where(masks[0], a[i], 0.0) for i in inst}
    for m in masks[1:]:
        x = {i: _bdot(jnp.where(m, a[i], 0.0), tinv[i]) for i in inst}
        tinv = {i: tinv[i] - _bdot(tinv[i], x[i]) for i in inst}
    uw_g = {(hh, s): _bdot(tinv[hh, s], rhs_all[hh][grows[s]]) for hh, s in inst}
    qo_g = {(hh, s): _bdot(qk[hh // 2][s] * dec[hh, s], uw_g[hh, s]) for hh, s in inst}
    in_group = [slice((c % DN_GROUP) * CHUNK, (c % DN_GROUP + 1) * CHUNK) for c in chunks]
    uw = {(hh, c): uw_g[hh, c // DN_GROUP][in_group[c]] for c in chunks for hh in heads}
    qo = {(hh, c): qo_g[hh, c // DN_GROUP][in_group[c]] for c in chunks for hh in heads}
    nk = {(hh, c): _bdot_tn(uw[hh, c], kd_all[hh][rows_of[c]]) for c in chunks for hh in heads}

    state_t = [s_ref[hh] for hh in heads]
    before = {}
    for c in chunks:
        for hh in heads:
            before[hh, c] = state_t[hh]
            g_end = eg_cb[hh][(c + 1) * CHUNK - 1:(c + 1) * CHUNK, :]
            state_t[hh] = (state_t[hh] * g_end - _bdot(state_t[hh], nk[hh, c][DN_HEAD_DIM:])
                           + nk[hh, c][:DN_HEAD_DIM])
    nw = nw_ref[...]
    for hh in heads:
        s_ref[hh] = state_t[hh]
        outs = [_bdot_nt(qd_all[hh][rows_of[c]] - qo[hh, c][:, DN_HEAD_DIM:], before[hh, c])
                + qo[hh, c][:, :DN_HEAD_DIM] for c in chunks]
        o = jnp.concatenate(outs, axis=0)
        zh = z_ref[:, hh * DN_HEAD_DIM:(hh + 1) * DN_HEAD_DIM].astype(F32)
        o = o * lax.rsqrt(jnp.mean(o * o, axis=-1, keepdims=True) + EPS) * nw * _silu(zh)
        o_ref[:, hh * DN_HEAD_DIM:(hh + 1) * DN_HEAD_DIM] = o.astype(o_ref.dtype)


def _gated_deltanet(qkv, z, b_rows, a_rows, conv_w, alog_b, dtb_b, norm_w):
    rows = qkv.shape[0]
    nstep = DN_QK_HEADS // DN_QPS
    qw = DN_QPS * DN_HEAD_DIM
    return pl.pallas_call(
        _dn_kernel,
        grid=(nstep, rows // TB),
        in_specs=[
            pl.BlockSpec((TB, qw), lambda j, t: (t, j)),
            pl.BlockSpec((TB, qw), lambda j, t: (t, nstep + j)),
            pl.BlockSpec((TB, 2 * qw), lambda j, t: (t, nstep + j)),
            pl.BlockSpec((TB, 2 * qw), lambda j, t: (t, j)),
            pl.BlockSpec((None, 2 * DN_QPS, TB), lambda j, t: (j, 0, t)),
            pl.BlockSpec((None, 2 * DN_QPS, TB), lambda j, t: (j, 0, t)),
            pl.BlockSpec((4, qw), lambda j, t: (0, j)),
            pl.BlockSpec((4, qw), lambda j, t: (0, nstep + j)),
            pl.BlockSpec((4, 2 * qw), lambda j, t: (0, nstep + j)),
            pl.BlockSpec((None, 2 * DN_QPS, 128), lambda j, t: (j, 0, 0)),
            pl.BlockSpec((None, 2 * DN_QPS, 128), lambda j, t: (j, 0, 0)),
            pl.BlockSpec((1, DN_HEAD_DIM), lambda j, t: (0, 0)),
        ],
        out_specs=pl.BlockSpec((TB, 2 * qw), lambda j, t: (t, j)),
        out_shape=jax.ShapeDtypeStruct((rows, DN_VAL_DIM), BF16),
        scratch_shapes=[
            pltpu.VMEM((2 * DN_QPS, DN_HEAD_DIM, DN_HEAD_DIM), F32),
            pltpu.VMEM((8, qw), F32),
            pltpu.VMEM((8, qw), F32),
            pltpu.VMEM((8, 2 * qw), F32),
            pltpu.VMEM((TB, TB), BF16),
            pltpu.VMEM((TB, TB), BF16),
        ],
        compiler_params=_cparams(("arbitrary", "arbitrary")),
    )(qkv, qkv, qkv, z, b_rows, a_rows, conv_w, conv_w, conv_w, alog_b, dtb_b, norm_w)


def _ssd_kernel(x_ref, b_ref, c_ref, z_ref, dt_ref, wx_ref, wb_ref, wc_ref, bx_ref, bb_ref, bc_ref,
                alog_ref, dtb_ref, dexp_ref, nw_ref, o_ref, h_ref, tx_ref, tb_ref, tc_ref, u_ref, e_ref):
    t = pl.program_id(1)

    @pl.when(t == 0)
    def _():
        h_ref[...] = jnp.zeros_like(h_ref)
        tx_ref[...] = jnp.zeros_like(tx_ref)
        tb_ref[...] = jnp.zeros_like(tb_ref)
        tc_ref[...] = jnp.zeros_like(tc_ref)
        _fill_chunk_matrices(u_ref, e_ref)

    xh = _conv_silu(x_ref, tx_ref, wx_ref, bx_ref)
    bm = _conv_silu(b_ref, tb_ref, wb_ref, bb_ref)
    cm = _conv_silu(c_ref, tc_ref, wc_ref, bc_ref)

    nrep = TB // 128
    dt = _softplus(dt_ref[...] + _tile_lanes(dtb_ref[...], nrep))
    pos = t * TB + lax.broadcasted_iota(jnp.int32, (SSD_REP, TB), 1)
    dt = jnp.where(pos >= FRONT - N_META, dt, 0.0)
    la = -jnp.exp(_tile_lanes(alog_ref[...], nrep)) * dt
    ac = _dot_sel(la, u_ref[...])
    al = _dot_sel(ac, e_ref[...])
    dtek = dt * jnp.exp(al - ac)

    def col_bcast_heads(rowsv):
        stacked = jnp.concatenate(
            [jnp.broadcast_to(rowsv[r:r + 1], (SSD_HEAD_DIM, TB)) for r in range(SSD_REP)], axis=0)
        return stacked.T

    ac_cb = col_bcast_heads(ac)
    ea_cb = jnp.exp(ac_cb)
    xd = xh * col_bcast_heads(dtek)

    ii = lax.broadcasted_iota(jnp.int32, (CHUNK, 2 * CHUNK), 0)
    jj = lax.broadcasted_iota(jnp.int32, (CHUNK, 2 * CHUNK), 1)
    causal2 = ii >= (jj & (CHUNK - 1))
    li = lax.broadcasted_iota(jnp.int32, (CHUNK, 2 * CHUNK), 1)
    lo_half = li < CHUNK

    state = h_ref[...]
    outs = []
    for c in range(CPB):
        rows = slice(c * CHUNK, (c + 1) * CHUNK)
        cc = cm[rows]
        bc = bm[rows]
        cb = _bdot_nt(cc, bc)
        cb2 = jnp.concatenate([cb, cb], axis=1)
        y_off = _bdot(cc, state) * ea_cb[rows]
        yd = []
        for m in range(SSD_REP // 2):
            lanes = slice(m * 128, (m + 1) * 128)
            rowv = jnp.concatenate([ac[2 * m:2 * m + 1, rows], ac[2 * m + 1:2 * m + 2, rows]], axis=1)
            dtrow = jnp.concatenate([dt[2 * m:2 * m + 1, rows], dt[2 * m + 1:2 * m + 2, rows]], axis=1)
            diff = ac_cb[rows, lanes] - rowv
            seg = jnp.where(causal2, jnp.exp(jnp.where(causal2, diff, 0.0)), 0.0)
            mm = cb2 * seg * dtrow
            x2 = xh[rows, lanes]
            rhs = jnp.concatenate([jnp.where(lo_half, x2, 0.0), jnp.where(lo_half, 0.0, x2)], axis=0)
            yd.append(_bdot(mm, rhs))
        outs.append(jnp.concatenate(yd, axis=1) + y_off)
        ea_last = ea_cb[(c + 1) * CHUNK - 1:(c + 1) * CHUNK, :]
        state = state * ea_last + _bdot_tn(bc, xd[rows])
    h_ref[...] = state

    y = jnp.concatenate(outs, axis=0) + dexp_ref[...] * xh
    y = y * _silu(z_ref[...].astype(F32))
    y = y * lax.rsqrt(jnp.mean(y * y, axis=-1, keepdims=True) + EPS) * nw_ref[...]
    o_ref[...] = y.astype(o_ref.dtype)


def _mamba2_ssd(xbc, z, dt_rows, conv_w, conv_b, alog_b, dtb_b, d_exp, norm_w):
    rows = xbc.shape[0]
    gd = SSD_GROUP_DIM
    nx = SSD_D_INNER // SSD_STATE
    return pl.pallas_call(
        _ssd_kernel,
        grid=(SSD_GROUPS, rows // TB),
        in_specs=[
            pl.BlockSpec((TB, gd), lambda g, t: (t, g)),
            pl.BlockSpec((TB, SSD_STATE), lambda g, t: (t, nx + g)),
            pl.BlockSpec((TB, SSD_STATE), lambda g, t: (t, nx + SSD_GROUPS + g)),
            pl.BlockSpec((TB, gd), lambda g, t: (t, g)),
            pl.BlockSpec((None, SSD_REP, TB), lambda g, t: (g, 0, t)),
            pl.BlockSpec((4, gd), lambda g, t: (0, g)),
            pl.BlockSpec((4, SSD_STATE), lambda g, t: (0, nx + g)),
            pl.BlockSpec((4, SSD_STATE), lambda g, t: (0, nx + SSD_GROUPS + g)),
            pl.BlockSpec((1, gd), lambda g, t: (0, g)),
            pl.BlockSpec((1, SSD_STATE), lambda g, t: (0, nx + g)),
            pl.BlockSpec((1, SSD_STATE), lambda g, t: (0, nx + SSD_GROUPS + g)),
            pl.BlockSpec((None, SSD_REP, 128), lambda g, t: (g, 0, 0)),
            pl.BlockSpec((None, SSD_REP, 128), lambda g, t: (g, 0, 0)),
            pl.BlockSpec((1, gd), lambda g, t: (0, g)),
            pl.BlockSpec((1, gd), lambda g, t: (0, g)),
        ],
        out_specs=pl.BlockSpec((TB, gd), lambda g, t: (t, g)),
        out_shape=jax.ShapeDtypeStruct((rows, SSD_D_INNER), BF16),
        scratch_shapes=[
            pltpu.VMEM((SSD_STATE, gd), F32),
            pltpu.VMEM((8, gd), F32),
            pltpu.VMEM((8, SSD_STATE), F32),
            pltpu.VMEM((8, SSD_STATE), F32),
            pltpu.VMEM((TB, TB), BF16),
            pltpu.VMEM((TB, TB), BF16),
        ],
        compiler_params=_cparams(("arbitrary", "arbitrary")),
    )(xbc, xbc, xbc, z, dt_rows, conv_w, conv_w, conv_w, conv_b, conv_b, conv_b, alog_b, dtb_b, d_exp, norm_w)


def _merge_kernel(ydn_ref, yssd_ref, gd_ref, gs_ref, wdn_ref, wssd_ref, o_ref):
    a = jnp.dot(ydn_ref[...], wdn_ref[...], preferred_element_type=F32)
    b = jnp.dot(yssd_ref[...], wssd_ref[...], preferred_element_type=F32)
    o = _sigmoid(gd_ref[...].astype(F32)) * a + _sigmoid(gs_ref[...].astype(F32)) * b
    o_ref[...] = o.astype(o_ref.dtype)


def _merge(y_dn, y_ssd, gates, w_dn, w_ssd, tn=512):
    ntile = (y_dn.shape[0] - FRONT) // TB
    off = FRONT // TB
    ng = D_MODEL // tn
    return pl.pallas_call(
        _merge_kernel,
        grid=(ng, ntile),
        in_specs=[
            pl.BlockSpec((TB, DN_VAL_DIM), lambda j, i: (i + off, 0)),
            pl.BlockSpec((TB, SSD_D_INNER), lambda j, i: (i + off, 0)),
            pl.BlockSpec((TB, tn), lambda j, i: (i + off, j)),
            pl.BlockSpec((TB, tn), lambda j, i: (i + off, ng + j)),
            pl.BlockSpec((DN_VAL_DIM, tn), lambda j, i: (0, j)),
            pl.BlockSpec((SSD_D_INNER, tn), lambda j, i: (0, j)),
        ],
        out_specs=pl.BlockSpec((TB, tn), lambda j, i: (i, j)),
        out_shape=jax.ShapeDtypeStruct((ntile * TB, D_MODEL), BF16),
        compiler_params=_cparams(("arbitrary", "arbitrary")),
    )(y_dn, y_ssd, gates, gates, w_dn, w_ssd)


def _out_router_kernel(m_ref, wo_ref, x_ref, nw_ref, wr_ref, br_ref,
                       h2_ref, hn_ref, idx_ref, p_ref, rank_ref, cnt_ref, carry_ref, us_ref):
    i = pl.program_id(0)
    rt = x_ref.shape[0]

    @pl.when(i == 0)
    def _():
        carry_ref[...] = jnp.zeros_like(carry_ref)
        jj = lax.broadcasted_iota(jnp.int32, (rt, rt), 0)
        ii = lax.broadcasted_iota(jnp.int32, (rt, rt), 1)
        us_ref[...] = jnp.where(jj < ii, 1.0, 0.0).astype(BF16)

    h2 = x_ref[...] + jnp.dot(m_ref[...], wo_ref[...], preferred_element_type=F32)
    h2_ref[...] = h2
    hn = h2 * lax.rsqrt(jnp.mean(h2 * h2, axis=-1, keepdims=True) + EPS) * nw_ref[...]
    hn_ref[...] = hn

    lg = _dot3(wr_ref[...], hn, nt=True) + br_ref[...]
    eio = lax.broadcasted_iota(jnp.int32, (N_EXPERTS, rt), 0)
    vals, hits, idxs = [], [], []
    cur = lg
    for _ in range(TOP_K):
        m = jnp.max(cur, axis=0, keepdims=True)
        ik = jnp.min(jnp.where(cur == m, eio, N_EXPERTS), axis=0, keepdims=True)
        hit = eio == ik
        vals.append(m)
        idxs.append(ik)
        hits.append(hit)
        cur = jnp.where(hit, -jnp.inf, cur)
    ex = [jnp.exp(vk - vals[0]) for vk in vals]
    den = ex[0] + ex[1] + ex[2] + ex[3]
    idx_ref[...] = jnp.concatenate(idxs, axis=0)
    p_ref[...] = jnp.concatenate([e / den for e in ex], axis=0)

    sel = jnp.zeros((N_EXPERTS, rt), F32)
    for hit in hits:
        sel = sel + jnp.where(hit, 1.0, 0.0)
    before = jnp.dot(sel.astype(BF16), us_ref[...], preferred_element_type=F32) \
        + _tile_lanes(carry_ref[...], rt // 128)
    ranks = [jnp.sum(jnp.where(hit, before, 0.0), axis=0, keepdims=True) for hit in hits]
    rank_ref[...] = jnp.concatenate(ranks, axis=0).astype(jnp.int32)
    carry = carry_ref[...] + jnp.dot(sel.astype(BF16), jnp.ones((rt, 128), BF16), preferred_element_type=F32)
    carry_ref[...] = carry
    cnt_ref[...] = carry


def _out_router(merged, w_out, x, nw, w_router_t, b_router_c):
    n_tok = x.shape[0]
    rt = ROUTER_RT
    tok_spec = pl.BlockSpec((TOP_K, rt), lambda i: (0, i))
    return pl.pallas_call(
        _out_router_kernel,
        grid=(n_tok // rt,),
        in_specs=[
            pl.BlockSpec((rt, D_MODEL), lambda i: (i, 0)),
            pl.BlockSpec((D_MODEL, D_MODEL), lambda i: (0, 0), pipeline_mode=pl.Buffered(1)),
            pl.BlockSpec((rt, D_MODEL), lambda i: (i, 0)),
            pl.BlockSpec((1, D_MODEL), lambda i: (0, 0)),
            pl.BlockSpec((N_EXPERTS, D_MODEL), lambda i: (0, 0)),
            pl.BlockSpec((N_EXPERTS, 1), lambda i: (0, 0)),
        ],
        out_specs=[
            pl.BlockSpec((rt, D_MODEL), lambda i: (i, 0)),
            pl.BlockSpec((rt, D_MODEL), lambda i: (i, 0)),
            tok_spec, tok_spec, tok_spec,
            pl.BlockSpec((N_EXPERTS, 128), lambda i: (0, 0)),
        ],
        out_shape=[
            jax.ShapeDtypeStruct((n_tok, D_MODEL), F32),
            jax.ShapeDtypeStruct((n_tok, D_MODEL), F32),
            jax.ShapeDtypeStruct((TOP_K, n_tok), jnp.int32),
            jax.ShapeDtypeStruct((TOP_K, n_tok), F32),
            jax.ShapeDtypeStruct((TOP_K, n_tok), jnp.int32),
            jax.ShapeDtypeStruct((N_EXPERTS, 128), F32),
        ],
        scratch_shapes=[pltpu.VMEM((N_EXPERTS, 128), F32), pltpu.VMEM((rt, rt), BF16)],
        compiler_params=_cparams(("arbitrary",)),
    )(merged, w_out, x, nw, w_router_t, b_router_c)


def _gather_kernel(nused_ref, tok_ref, src_ref, o_ref, buf_ref, sem):
    b = pl.program_id(0)
    nu = nused_ref[0]

    def issue(blk, slot):
        base = blk * MOE_BM

        def body(r, carry):
            pltpu.make_async_copy(src_ref.at[pl.ds(tok_ref[base + r], 1)], buf_ref.at[slot, pl.ds(r, 1)],
                                  sem.at[slot]).start()
            return carry

        lax.fori_loop(0, MOE_BM, body, 0, unroll=8)

    @pl.when(b == 0)
    def _():
        issue(0, 0)

    @pl.when(b + 1 < nu)
    def _():
        issue(b + 1, (b + 1) % 2)

    @pl.when(b < nu)
    def _():
        slot = b % 2
        pltpu.make_async_copy(src_ref.at[pl.ds(0, MOE_BM)], buf_ref.at[slot], sem.at[slot]).wait()
        o_ref[...] = buf_ref[slot].astype(o_ref.dtype)

    @pl.when(b >= nu)
    def _():
        o_ref[...] = jnp.zeros_like(o_ref)


def _gather_rows(n_used, row_tok, src):
    nb = row_tok.shape[0] // MOE_BM
    return pl.pallas_call(
        _gather_kernel,
        grid_spec=pltpu.PrefetchScalarGridSpec(
            num_scalar_prefetch=2,
            grid=(nb,),
            in_specs=[pl.BlockSpec(memory_space=pl.ANY)],
            out_specs=pl.BlockSpec((MOE_BM, D_MODEL), lambda b, nu, tok: (b, 0)),
            scratch_shapes=[pltpu.VMEM((2, MOE_BM, D_MODEL), F32), pltpu.SemaphoreType.DMA((2,))],
        ),
        out_shape=jax.ShapeDtypeStruct((nb * MOE_BM, D_MODEL), BF16),
        compiler_params=_cparams(("arbitrary",)),
    )(n_used, row_tok, src)


def _expert_row_loop(first_blk, n, cols, src_ref, dst_ref, xbuf, obuf, in_sem, out_sem, compute):
    def rows(i):
        return pl.ds(pl.multiple_of((first_blk + i) * MOE_BM, MOE_BM), MOE_BM)

    def xcopy(i, slot):
        return pltpu.make_async_copy(src_ref.at[rows(i)], xbuf.at[slot], in_sem.at[slot])

    def ocopy(i, slot):
        dst = dst_ref.at[rows(i)] if cols is None else dst_ref.at[rows(i), cols]
        return pltpu.make_async_copy(obuf.at[slot], dst, out_sem.at[slot])

    @pl.when(n > 0)
    def _():
        xcopy(0, 0).start()

        def body(i, carry):
            slot = i % 2

            @pl.when(i + 1 < n)
            def _():
                xcopy(i + 1, 1 - slot).start()

            xcopy(i, slot).wait()

            @pl.when(i >= 2)
            def _():
                ocopy(i - 2, slot).wait()

            obuf[slot] = compute(xbuf[slot]).astype(obuf.dtype)
            ocopy(i, slot).start()
            return carry

        lax.fori_loop(0, n, body, 0)

        @pl.when(n >= 2)
        def _():
            ocopy(n - 2, n % 2).wait()

        ocopy(n - 1, (n - 1) % 2).wait()


def _zero_unused_blocks(n_used, cols, dst_ref, obuf, out_sem):
    nb = dst_ref.shape[0] // MOE_BM
    obuf[0] = jnp.zeros(obuf.shape[1:], obuf.dtype)

    def body(i, carry):
        r = pl.ds(pl.multiple_of(i * MOE_BM, MOE_BM), MOE_BM)
        dst = dst_ref.at[r] if cols is None else dst_ref.at[r, cols]
        cp = pltpu.make_async_copy(obuf.at[0], dst, out_sem.at[0])
        cp.start()
        cp.wait()
        return carry

    lax.fori_loop(n_used, nb, body, 0)


def _expert_up_kernel(first_ref, nblk_ref, nused_ref, xs_ref, wg_ref, wu_ref, bg_ref, bu_ref, act_ref,
                      xbuf, obuf, wgb_ref, wub_ref, in_sem, out_sem):
    e = pl.program_id(0)
    j = pl.program_id(1)
    tn = wgb_ref.shape[1]
    wgb_ref[...] = wg_ref[...].astype(BF16)
    wub_ref[...] = wu_ref[...].astype(BF16)

    def compute(x):
        gt = jnp.dot(x, wgb_ref[...], preferred_element_type=F32) + bg_ref[...]
        up = jnp.dot(x, wub_ref[...], preferred_element_type=F32) + bu_ref[...]
        gt = jnp.minimum(gt, SWIGLU_LIMIT)
        up = jnp.clip(up, -SWIGLU_LIMIT, SWIGLU_LIMIT)
        return gt * _sigmoid(SWIGLU_ALPHA * gt) * (up + 1.0)

    cols = pl.ds(pl.multiple_of(j * tn, tn), tn)
    _expert_row_loop(first_ref[e], nblk_ref[e], cols, xs_ref, act_ref, xbuf, obuf, in_sem, out_sem, compute)

    @pl.when(e == N_EXPERTS - 1)
    def _():
        _zero_unused_blocks(nused_ref[0], cols, act_ref, obuf, out_sem)


def _expert_up(first_blk, n_blk, n_used, xs, w_gate, w_up, b_gate, b_up):
    tn = MOE_TN_UP
    wspec = pl.BlockSpec((None, D_MODEL, tn), lambda e, j, *_: (e, 0, j))
    bspec = pl.BlockSpec((None, 1, tn), lambda e, j, *_: (e, 0, j))
    return pl.pallas_call(
        _expert_up_kernel,
        grid_spec=pltpu.PrefetchScalarGridSpec(
            num_scalar_prefetch=3,
            grid=(N_EXPERTS, D_MODEL // tn),
            in_specs=[pl.BlockSpec(memory_space=pl.ANY), wspec, wspec, bspec, bspec],
            out_specs=pl.BlockSpec(memory_space=pl.ANY),
            scratch_shapes=[pltpu.VMEM((2, MOE_BM, D_MODEL), BF16), pltpu.VMEM((2, MOE_BM, tn), BF16),
                            pltpu.VMEM((D_MODEL, tn), BF16), pltpu.VMEM((D_MODEL, tn), BF16),
                            pltpu.SemaphoreType.DMA((2,)), pltpu.SemaphoreType.DMA((2,))],
        ),
        out_shape=jax.ShapeDtypeStruct(xs.shape, BF16),
        compiler_params=_cparams(("arbitrary", "arbitrary")),
    )(first_blk, n_blk, n_used, xs, w_gate, w_up, b_gate, b_up)


def _expert_down_kernel(first_ref, nblk_ref, nused_ref, act_ref, wd_ref, bd_ref, y_ref,
                        xbuf, obuf, wdb_ref, in_sem, out_sem):
    e = pl.program_id(0)
    j = pl.program_id(1)
    tn = wdb_ref.shape[1]
    wdb_ref[...] = wd_ref[...].astype(BF16)

    def compute(a):
        return jnp.dot(a, wdb_ref[...], preferred_element_type=F32) + bd_ref[...]

    cols = pl.ds(pl.multiple_of(j * tn, tn), tn)
    _expert_row_loop(first_ref[e], nblk_ref[e], cols, act_ref, y_ref, xbuf, obuf, in_sem, out_sem, compute)

    @pl.when(e == N_EXPERTS - 1)
    def _():
        _zero_unused_blocks(nused_ref[0], cols, y_ref, obuf, out_sem)


def _expert_down(first_blk, n_blk, n_used, act, w_down, b_down):
    tn = MOE_TN_DOWN
    return pl.pallas_call(
        _expert_down_kernel,
        grid_spec=pltpu.PrefetchScalarGridSpec(
            num_scalar_prefetch=3,
            grid=(N_EXPERTS, D_MODEL // tn),
            in_specs=[pl.BlockSpec(memory_space=pl.ANY),
                      pl.BlockSpec((None, D_MODEL, tn), lambda e, j, *_: (e, 0, j)),
                      pl.BlockSpec((None, 1, tn), lambda e, j, *_: (e, 0, j))],
            out_specs=pl.BlockSpec(memory_space=pl.ANY),
            scratch_shapes=[pltpu.VMEM((2, MOE_BM, D_MODEL), BF16), pltpu.VMEM((2, MOE_BM, tn), F32),
                            pltpu.VMEM((D_MODEL, tn), BF16),
                            pltpu.SemaphoreType.DMA((2,)), pltpu.SemaphoreType.DMA((2,))],
        ),
        out_shape=jax.ShapeDtypeStruct(act.shape, F32),
        compiler_params=_cparams(("arbitrary", "arbitrary")),
    )(first_blk, n_blk, n_used, act, w_down, b_down)


def _combine_kernel(dest_ref, h2_ref, p_ref, fw_ref, y_ref, o_ref, buf_ref, sem):
    i = pl.program_id(0)
    tt = COMB_TT

    def issue(tile, slot):
        base = tile * (TOP_K * tt)

        def body(r, carry):
            for k in range(TOP_K):
                pltpu.make_async_copy(y_ref.at[pl.ds(dest_ref[base + k * tt + r], 1)],
                                      buf_ref.at[slot, k, pl.ds(r, 1)], sem.at[slot]).start()
            return carry

        lax.fori_loop(0, tt, body, 0, unroll=4)

    @pl.when(i == 0)
    def _():
        issue(0, 0)

    @pl.when(i + 1 < pl.num_programs(0))
    def _():
        issue(i + 1, (i + 1) % 2)

    slot = i % 2
    for k in range(TOP_K):
        pltpu.make_async_copy(y_ref.at[pl.ds(0, tt)], buf_ref.at[slot, k], sem.at[slot]).wait()
    p = p_ref[...]
    acc = h2_ref[...]
    for k in range(TOP_K):
        acc = acc + p[:, k:k + 1] * buf_ref[slot, k]
    o_ref[...] = acc * lax.rsqrt(jnp.mean(acc * acc, axis=-1, keepdims=True) + EPS) * fw_ref[...]


def _combine(dest, h2, p_cols, final_w, y_rows):
    n_tok = h2.shape[0]
    tt = COMB_TT
    return pl.pallas_call(
        _combine_kernel,
        grid_spec=pltpu.PrefetchScalarGridSpec(
            num_scalar_prefetch=1,
            grid=(n_tok // tt,),
            in_specs=[
                pl.BlockSpec((tt, D_MODEL), lambda i, d: (i, 0)),
                pl.BlockSpec((tt, TOP_K), lambda i, d: (i, 0)),
                pl.BlockSpec((1, D_MODEL), lambda i, d: (0, 0)),
                pl.BlockSpec(memory_space=pl.ANY),
            ],
            out_specs=pl.BlockSpec((tt, D_MODEL), lambda i, d: (i, 0)),
            scratch_shapes=[pltpu.VMEM((2, TOP_K, tt, D_MODEL), F32), pltpu.SemaphoreType.DMA((2,))],
        ),
        out_shape=jax.ShapeDtypeStruct((n_tok, D_MODEL), F32),
        compiler_params=_cparams(("arbitrary",)),
    )(dest, h2, p_cols, final_w, y_rows)


def _mixer(x2, meta, norm_w, w_in, dn_conv_w, dn_a_log, dn_dt_bias, dn_norm_w,
           ssd_conv_w, ssd_conv_b, ssd_a_log, ssd_dt_bias, ssd_d, ssd_norm_w, w_proj_dn, w_proj_ssd):
    o_qkv = 0
    o_dnz = o_qkv + DN_CONV_DIM
    o_b = o_dnz + DN_VAL_DIM
    o_a = o_b + DN_V_HEADS
    o_sz = o_a + DN_V_HEADS
    o_xbc = o_sz + SSD_D_INNER
    o_dt = o_xbc + SSD_CONV_DIM
    o_gate = o_dt + SSD_HEADS
    d_in = o_gate + 2 * D_MODEL
    w_small = jnp.concatenate([w_in[:, o_b:o_sz], w_in[:, o_dt:o_gate]], axis=1)

    hn, small = _norm_small(x2, meta, norm_w.reshape(1, D_MODEL), w_small)
    qkv = _matmul(hn, w_in[:, o_qkv:o_dnz].astype(BF16), BF16)
    dn_z = _matmul(hn, w_in[:, o_dnz:o_b].astype(BF16), BF16)
    ssd_z = _matmul(hn, w_in[:, o_sz:o_xbc].astype(BF16), BF16)
    xbc = _matmul(hn, w_in[:, o_xbc:o_dt].astype(BF16), BF16)
    gates = _matmul(hn, w_in[:, o_gate:d_in].astype(BF16), BF16)

    rows = hn.shape[0]
    sm_t = small.T
    dn_steps, dn_vps = DN_QK_HEADS // DN_QPS, 2 * DN_QPS
    b_rows = sm_t[0:DN_V_HEADS].reshape(dn_steps, dn_vps, rows)
    a_rows = sm_t[DN_V_HEADS:2 * DN_V_HEADS].reshape(dn_steps, dn_vps, rows)
    dt_rows = sm_t[2 * DN_V_HEADS:].reshape(SSD_GROUPS, SSD_REP, rows)

    bc = lambda p, g, r: jnp.broadcast_to(p.astype(F32).reshape(g, r, 1), (g, r, 128))
    y_dn = _gated_deltanet(qkv, dn_z, b_rows, a_rows, dn_conv_w,
                           bc(dn_a_log, dn_steps, dn_vps), bc(dn_dt_bias, dn_steps, dn_vps),
                           dn_norm_w.reshape(1, DN_HEAD_DIM))
    y_ssd = _mamba2_ssd(xbc, ssd_z, dt_rows, ssd_conv_w, ssd_conv_b.reshape(1, SSD_CONV_DIM),
                        bc(ssd_a_log, SSD_GROUPS, SSD_REP), bc(ssd_dt_bias, SSD_GROUPS, SSD_REP),
                        jnp.repeat(ssd_d, SSD_HEAD_DIM).reshape(1, SSD_D_INNER),
                        ssd_norm_w.reshape(1, SSD_D_INNER))
    return _merge(y_dn, y_ssd, gates, w_proj_dn.astype(BF16), w_proj_ssd.astype(BF16))


def _moe_and_final(merged, x2, w_out, norm_ffn_w, w_router, b_router, w_gate, b_gate, w_up, b_up,
                   w_down, b_down, final_norm_w):
    n_tok = x2.shape[0]
    h2, hn2, idx, probs, rank, cnt = _out_router(
        merged, w_out.astype(BF16), x2, norm_ffn_w.reshape(1, D_MODEL), w_router.T,
        b_router.reshape(N_EXPERTS, 1))

    counts = cnt[:, 0].astype(jnp.int32)
    padded = (counts + MOE_BM - 1) // MOE_BM * MOE_BM
    pad_end = jnp.cumsum(padded)
    pad_start = pad_end - padded
    hit = idx[:, :, None] == jnp.arange(N_EXPERTS, dtype=jnp.int32)
    dest = jnp.sum(jnp.where(hit, pad_start.astype(jnp.int32), 0), axis=-1) + rank
    nb = -(-(n_tok * TOP_K + N_EXPERTS * (MOE_BM - 1)) // MOE_BM)
    tok_ids = jnp.broadcast_to(jnp.arange(n_tok, dtype=jnp.int32)[None], (TOP_K, n_tok))
    row_tok = jnp.zeros((nb * MOE_BM,), jnp.int32).at[dest.reshape(-1)].set(tok_ids.reshape(-1))
    first_blk = (pad_start // MOE_BM).astype(jnp.int32)
    n_blk = (padded // MOE_BM).astype(jnp.int32)
    n_used = (pad_end[-1:] // MOE_BM).astype(jnp.int32)

    xs = _gather_rows(n_used, row_tok, hn2)
    act = _expert_up(first_blk, n_blk, n_used, xs, w_gate, w_up,
                     b_gate.reshape(N_EXPERTS, 1, D_MODEL), b_up.reshape(N_EXPERTS, 1, D_MODEL))
    y_rows = _expert_down(first_blk, n_blk, n_used, act, w_down, b_down.reshape(N_EXPERTS, 1, D_MODEL))
    dest_t = dest.reshape(TOP_K, n_tok // COMB_TT, COMB_TT).transpose(1, 0, 2).reshape(-1)
    return _combine(dest_t, h2, probs.T, final_norm_w.reshape(1, D_MODEL), y_rows)


def kernel(x, meta_tokens, norm_mix_w, w_in, dn_conv_w, dn_a_log, dn_dt_bias, dn_norm_w, ssd_conv_w, ssd_conv_b, ssd_a_log, ssd_dt_bias, ssd_d, ssd_norm_w, w_proj_dn, w_proj_ssd, w_out, norm_ffn_w, w_router, b_router, w_gate, b_gate, w_up, b_up, w_down, b_down, final_norm_w):
    bsz, seq, d = x.shape
    assert bsz == 1 and d == D_MODEL and seq % TB == 0
    assert norm_mix_w.shape[0] == 1, "single-layer block"
    x2 = x.reshape(seq, d)
    merged = _mixer(x2, meta_tokens.astype(x.dtype), norm_mix_w[0], w_in[0], dn_conv_w[0], dn_a_log[0], dn_dt_bias[0], dn_norm_w[0],
                    ssd_conv_w[0], ssd_conv_b[0], ssd_a_log[0], ssd_dt_bias[0], ssd_d[0], ssd_norm_w[0],
                    w_proj_dn[0], w_proj_ssd[0])
    out = _moe_and_final(merged, x2, w_out[0], norm_ffn_w[0], w_router[0], b_router[0], w_gate[0], b_gate[0],
                         w_up[0], b_up[0], w_down[0], b_down[0], final_norm_w)
    return out.reshape(bsz, seq, d)
```

```python
import functools

import jax
import jax.numpy as jnp
from jax import lax
from jax.experimental import pallas as pl
from jax.experimental.pallas import tpu as pltpu

F32 = jnp.float32
BF16 = jnp.bfloat16

D_MODEL = 2048
N_META = 16
CHUNK = 64
EPS = 1e-6

DN_QK_HEADS = 16
DN_V_HEADS = 32
DN_HEAD_DIM = 128
DN_KEY_DIM = DN_QK_HEADS * DN_HEAD_DIM
DN_VAL_DIM = DN_V_HEADS * DN_HEAD_DIM
DN_CONV_DIM = 2 * DN_KEY_DIM + DN_VAL_DIM

SSD_D_INNER = 2 * D_MODEL
SSD_HEAD_DIM = 64
SSD_HEADS = SSD_D_INNER // SSD_HEAD_DIM
SSD_GROUPS = 8
SSD_REP = SSD_HEADS // SSD_GROUPS
SSD_STATE = 128
SSD_GS = SSD_GROUPS * SSD_STATE
SSD_CONV_DIM = SSD_D_INNER + 2 * SSD_GS
SSD_GROUP_DIM = SSD_D_INNER // SSD_GROUPS

N_EXPERTS = 32
TOP_K = 4
SWIGLU_LIMIT = 7.0
SWIGLU_ALPHA = 1.702

TB = 512
FRONT = TB
CPB = TB // CHUNK
MOE_BM = 256
MOE_TN_UP = 1024
MOE_TN_DOWN = 2048
MOE_XRING = 3
COMB_TT = 256
ROUTER_RT = 512
DN_GROUP = 2
DN_QPS = 4
VMEM_LIMIT = 56 * 1024 * 1024


def _cparams(sem):
    return pltpu.CompilerParams(dimension_semantics=sem, vmem_limit_bytes=VMEM_LIMIT)


def _bdot(a, b):
    return jnp.dot(a.astype(BF16), b.astype(BF16), preferred_element_type=F32)


def _bdot_nt(a, b):
    return lax.dot_general(a.astype(BF16), b.astype(BF16), (((1,), (1,)), ((), ())),
                           preferred_element_type=F32)


def _bdot_tn(a, b):
    return lax.dot_general(a.astype(BF16), b.astype(BF16), (((0,), (0,)), ((), ())),
                           preferred_element_type=F32)


def _split3(x):
    x1 = x.astype(BF16)
    r1 = x - x1.astype(F32)
    x2 = r1.astype(BF16)
    x3 = (r1 - x2.astype(F32)).astype(BF16)
    return x1, x2, x3


def _dot_sel(x, m):
    x1, x2, x3 = _split3(x)
    d = lambda a: jnp.dot(a, m, preferred_element_type=F32)
    return d(x1) + d(x2) + d(x3)


def _dot3(a, b, nt=False):
    a1 = a.astype(BF16)
    a2 = (a - a1.astype(F32)).astype(BF16)
    b1 = b.astype(BF16)
    b2 = (b - b1.astype(F32)).astype(BF16)
    if nt:
        d = lambda p, q: lax.dot_general(p, q, (((1,), (1,)), ((), ())), preferred_element_type=F32)
    else:
        d = lambda p, q: jnp.dot(p, q, preferred_element_type=F32)
    return d(a1, b1) + d(a2, b1) + d(a1, b2)


def _sigmoid(x):
    return 1.0 / (1.0 + jnp.exp(-x))


def _silu(x):
    return x * _sigmoid(x)


def _softplus(x):
    return jnp.maximum(x, 0.0) + jnp.log(1.0 + jnp.exp(-jnp.abs(x)))


def _tile_lanes(p, n):
    return jnp.concatenate([p] * n, axis=1)


def _norm_small_kernel(x_ref, meta_ref, nw_ref, ws_ref, hn_ref, sm_ref):
    i = pl.program_id(0)

    def emit(h):
        y = h * lax.rsqrt(jnp.mean(h * h, axis=-1, keepdims=True) + EPS) * nw_ref[...]
        hn_ref[...] = y.astype(BF16)
        sm_ref[...] = _dot3(y, ws_ref[...])

    @pl.when(i == 0)
    def _():
        emit(jnp.concatenate([jnp.zeros((FRONT - N_META, D_MODEL), F32), meta_ref[...]], axis=0))

    @pl.when(i > 0)
    def _():
        emit(x_ref[...])


def _norm_small(x2, meta, nw, w_small):
    rows = FRONT + x2.shape[0]
    return pl.pallas_call(
        _norm_small_kernel,
        grid=(rows // TB,),
        in_specs=[pl.BlockSpec((TB, D_MODEL), lambda i: (jnp.maximum(i - 1, 0), 0)),
                  pl.BlockSpec((N_META, D_MODEL), lambda i: (0, 0)),
                  pl.BlockSpec((1, D_MODEL), lambda i: (0, 0)),
                  pl.BlockSpec((D_MODEL, 128), lambda i: (0, 0))],
        out_specs=[pl.BlockSpec((TB, D_MODEL), lambda i: (i, 0)),
                   pl.BlockSpec((TB, 128), lambda i: (i, 0))],
        out_shape=[jax.ShapeDtypeStruct((rows, D_MODEL), BF16),
                   jax.ShapeDtypeStruct((rows, 128), F32)],
        compiler_params=_cparams(("arbitrary",)),
    )(x2, meta, nw, w_small)


def _mm_kernel(x_ref, w_ref, o_ref):
    o_ref[...] = jnp.dot(x_ref[...], w_ref[...], preferred_element_type=F32).astype(o_ref.dtype)


def _matmul(x, w, out_dtype, tm=1024, tn=1024):
    m, k = x.shape
    n = w.shape[1]
    return pl.pallas_call(
        _mm_kernel,
        grid=(n // tn, pl.cdiv(m, tm)),
        in_specs=[pl.BlockSpec((tm, k), lambda j, i: (i, 0)),
                  pl.BlockSpec((k, tn), lambda j, i: (0, j))],
        out_specs=pl.BlockSpec((tm, tn), lambda j, i: (i, j)),
        out_shape=jax.ShapeDtypeStruct((m, n), out_dtype),
        compiler_params=_cparams(("arbitrary", "arbitrary")),
    )(x, w)


def _conv_silu(raw_ref, tail_ref, w_ref, b_ref=None):
    x = raw_ref[...].astype(F32)
    rows = x.shape[0]
    ext = jnp.concatenate([tail_ref[...], x], axis=0)
    w = w_ref[...]
    y = x * w[3:4]
    for j in (1, 2, 3):
        y = y + pltpu.roll(ext, j, 0)[8:8 + rows] * w[3 - j:4 - j]
    tail_ref[...] = x[rows - 8:rows]
    if b_ref is not None:
        y = y + b_ref[...]
    return _silu(y)


def _fill_chunk_matrices(u_ref, e_ref):
    jj = lax.broadcasted_iota(jnp.int32, (TB, TB), 0)
    ii = lax.broadcasted_iota(jnp.int32, (TB, TB), 1)
    same = (jj // CHUNK) == (ii // CHUNK)
    u_ref[...] = jnp.where(same & (jj <= ii), 1.0, 0.0).astype(BF16)
    e_ref[...] = jnp.where(same & (jj % CHUNK == CHUNK - 1), 1.0, 0.0).astype(BF16)


def _col_bcast(row, width):
    return jnp.broadcast_to(row, (width, TB)).T


def _dn_kernel(q_ref, k_ref, v_ref, z_ref, b_ref, a_ref, wq_ref, wk_ref, wv_ref, alog_ref, dtb_ref, nw_ref,
               o_ref, s_ref, tq_ref, tk_ref, tv_ref, u_ref, e_ref):
    t = pl.program_id(1)

    @pl.when(t == 0)
    def _():
        s_ref[...] = jnp.zeros_like(s_ref)
        tq_ref[...] = jnp.zeros_like(tq_ref)
        tk_ref[...] = jnp.zeros_like(tk_ref)
        tv_ref[...] = jnp.zeros_like(tv_ref)
        _fill_chunk_matrices(u_ref, e_ref)

    d = DN_HEAD_DIM
    nq = q_ref.shape[1] // d
    q_all = _conv_silu(q_ref, tq_ref, wq_ref)
    k_all = _conv_silu(k_ref, tk_ref, wk_ref)
    v = _conv_silu(v_ref, tv_ref, wv_ref)
    qs, ks = [], []
    for i in range(nq):
        q = q_all[:, i * d:(i + 1) * d]
        k = k_all[:, i * d:(i + 1) * d]
        qs.append(q * (lax.rsqrt(jnp.sum(q * q, axis=-1, keepdims=True) + EPS) * (d ** -0.5)))
        ks.append(k * lax.rsqrt(jnp.sum(k * k, axis=-1, keepdims=True) + EPS))

    beta = _sigmoid(b_ref[...])
    g = -jnp.exp(_tile_lanes(alog_ref[...], TB // 128)) * _softplus(a_ref[...] + _tile_lanes(dtb_ref[...], TB // 128))
    gc = _dot_sel(g, u_ref[...])
    gl = _dot_sel(gc, e_ref[...])
    eg = jnp.exp(gc)
    ek = jnp.exp(gl - gc)

    gw = DN_GROUP * CHUNK
    lg = CHUNK.bit_length() - 1
    ii = lax.broadcasted_iota(jnp.int32, (gw, gw), 0)
    jj = lax.broadcasted_iota(jnp.int32, (gw, gw), 1)
    same = (ii >> lg) == (jj >> lg)
    causal = same & (ii >= jj)
    strict = same & (ii > jj)
    eye = jnp.where(ii == jj, 1.0, 0.0)
    masks = [strict & ((ii >> 1) == (jj >> 1))]
    blk = 2
    while blk < CHUNK:
        sh = blk.bit_length() - 1
        masks.append(((ii >> (sh + 1)) == (jj >> (sh + 1))) & (((ii >> sh) & 1) == 1) & (((jj >> sh) & 1) == 0))
        blk *= 2

    heads = range(2 * nq)
    chunks = range(CPB)
    groups = range(TB // gw)
    rows_of = [slice(c * CHUNK, (c + 1) * CHUNK) for c in chunks]
    grows = [slice(s * gw, (s + 1) * gw) for s in groups]
    kk = [[_bdot_nt(ks[i][grows[s]], ks[i][grows[s]]) for s in groups] for i in range(nq)]
    qk = [[_bdot_nt(qs[i][grows[s]], ks[i][grows[s]]) for s in groups] for i in range(nq)]

    eg_cb, rhs_all, qd_all, kd_all = [], [], [], []
    dec, a = {}, {}
    for hh in heads:
        q, k = qs[hh // 2], ks[hh // 2]
        gc_cb = _col_bcast(gc[hh:hh + 1], max(gw, DN_HEAD_DIM))
        beta_cb = _col_bcast(beta[hh:hh + 1], max(gw, DN_HEAD_DIM))
        eg_cb.append(_col_bcast(eg[hh:hh + 1], DN_HEAD_DIM))
        ek_cb = _col_bcast(ek[hh:hh + 1], DN_HEAD_DIM)
        beta_h = beta_cb[:, :DN_HEAD_DIM]
        vh = v[:, hh * DN_HEAD_DIM:(hh + 1) * DN_HEAD_DIM]
        rhs_all.append(jnp.concatenate([vh * beta_h, k * (beta_h * eg_cb[hh])], axis=1))
        qd_all.append(q * eg_cb[hh])
        kd_all.append(k * ek_cb)
        for s in groups:
            diff = gc_cb[grows[s], :gw] - gc[hh:hh + 1, s * gw:(s + 1) * gw]
            dec[hh, s] = jnp.where(causal, jnp.exp(jnp.where(causal, diff, 0.0)), 0.0)
            a[hh, s] = jnp.where(strict, beta_cb[grows[s], :gw] * kk[hh // 2][s] * dec[hh, s], 0.0)
    inst = [(hh, s) for s in groups for hh in heads]

    tinv = {i: eye - jnp.where(masks[0], a[i], 0.0) for i in inst}
    for m in masks[1:]:
        x = {i: _bdot(jnp.where(m, a[i], 0.0), tinv[i]) for i in inst}
        tinv = {i: tinv[i] - _bdot(tinv[i], x[i]) for i in inst}
    uw_g = {(hh, s): _bdot(tinv[hh, s], rhs_all[hh][grows[s]]) for hh, s in inst}
    qo_g = {(hh, s): _bdot(qk[hh // 2][s] * dec[hh, s], uw_g[hh, s]) for hh, s in inst}
    in_group = [slice((c % DN_GROUP) * CHUNK, (c % DN_GROUP + 1) * CHUNK) for c in chunks]
    uw = {(hh, c): uw_g[hh, c // DN_GROUP][in_group[c]] for c in chunks for hh in heads}
    qo = {(hh, c): qo_g[hh, c // DN_GROUP][in_group[c]] for c in chunks for hh in heads}
    nk = {(hh, c): _bdot_tn(uw[hh, c], kd_all[hh][rows_of[c]]) for c in chunks for hh in heads}

    state_t = [s_ref[hh] for hh in heads]
    before = {}
    for c in chunks:
        for hh in heads:
            before[hh, c] = state_t[hh]
            g_end = eg_cb[hh][(c + 1) * CHUNK - 1:(c + 1) * CHUNK, :]
            state_t[hh] = (state_t[hh] * g_end - _bdot(state_t[hh], nk[hh, c][DN_HEAD_DIM:])
                           + nk[hh, c][:DN_HEAD_DIM])
    nw = nw_ref[...]
    for hh in heads:
        s_ref[hh] = state_t[hh]
        outs = [_bdot_nt(qd_all[hh][rows_of[c]] - qo[hh, c][:, DN_HEAD_DIM:], before[hh, c])
                + qo[hh, c][:, :DN_HEAD_DIM] for c in chunks]
        o = jnp.concatenate(outs, axis=0)
        zh = z_ref[:, hh * DN_HEAD_DIM:(hh + 1) * DN_HEAD_DIM].astype(F32)
        o = o * lax.rsqrt(jnp.mean(o * o, axis=-1, keepdims=True) + EPS) * nw * _silu(zh)
        o_ref[:, hh * DN_HEAD_DIM:(hh + 1) * DN_HEAD_DIM] = o.astype(o_ref.dtype)


def _gated_deltanet(qkv, z, b_rows, a_rows, conv_w, alog_b, dtb_b, norm_w):
    rows = qkv.shape[0]
    nstep = DN_QK_HEADS // DN_QPS
    qw = DN_QPS * DN_HEAD_DIM
    return pl.pallas_call(
        _dn_kernel,
        grid=(nstep, rows // TB),
        in_specs=[
            pl.BlockSpec((TB, qw), lambda j, t: (t, j)),
            pl.BlockSpec((TB, qw), lambda j, t: (t, nstep + j)),
            pl.BlockSpec((TB, 2 * qw), lambda j, t: (t, nstep + j)),
            pl.BlockSpec((TB, 2 * qw), lambda j, t: (t, j)),
            pl.BlockSpec((None, 2 * DN_QPS, TB), lambda j, t: (j, 0, t)),
            pl.BlockSpec((None, 2 * DN_QPS, TB), lambda j, t: (j, 0, t)),
            pl.BlockSpec((4, qw), lambda j, t: (0, j)),
            pl.BlockSpec((4, qw), lambda j, t: (0, nstep + j)),
            pl.BlockSpec((4, 2 * qw), lambda j, t: (0, nstep + j)),
            pl.BlockSpec((None, 2 * DN_QPS, 128), lambda j, t: (j, 0, 0)),
            pl.BlockSpec((None, 2 * DN_QPS, 128), lambda j, t: (j, 0, 0)),
            pl.BlockSpec((1, DN_HEAD_DIM), lambda j, t: (0, 0)),
        ],
        out_specs=pl.BlockSpec((TB, 2 * qw), lambda j, t: (t, j)),
        out_shape=jax.ShapeDtypeStruct((rows, DN_VAL_DIM), BF16),
        scratch_shapes=[
            pltpu.VMEM((2 * DN_QPS, DN_HEAD_DIM, DN_HEAD_DIM), F32),
            pltpu.VMEM((8, qw), F32),
            pltpu.VMEM((8, qw), F32),
            pltpu.VMEM((8, 2 * qw), F32),
            pltpu.VMEM((TB, TB), BF16),
            pltpu.VMEM((TB, TB), BF16),
        ],
        compiler_params=_cparams(("arbitrary", "arbitrary")),
    )(qkv, qkv, qkv, z, b_rows, a_rows, conv_w, conv_w, conv_w, alog_b, dtb_b, norm_w)


def _ssd_kernel(x_ref, b_ref, c_ref, z_ref, dt_ref, wx_ref, wb_ref, wc_ref, bx_ref, bb_ref, bc_ref,
                alog_ref, dtb_ref, dexp_ref, nw_ref, o_ref, h_ref, tx_ref, tb_ref, tc_ref, u_ref, e_ref):
    t = pl.program_id(1)

    @pl.when(t == 0)
    def _():
        h_ref[...] = jnp.zeros_like(h_ref)
        tx_ref[...] = jnp.zeros_like(tx_ref)
        tb_ref[...] = jnp.zeros_like(tb_ref)
        tc_ref[...] = jnp.zeros_like(tc_ref)
        _fill_chunk_matrices(u_ref, e_ref)

    xh = _conv_silu(x_ref, tx_ref, wx_ref, bx_ref)
    bm = _conv_silu(b_ref, tb_ref, wb_ref, bb_ref)
    cm = _conv_silu(c_ref, tc_ref, wc_ref, bc_ref)

    nrep = TB // 128
    dt = _softplus(dt_ref[...] + _tile_lanes(dtb_ref[...], nrep))
    pos = t * TB + lax.broadcasted_iota(jnp.int32, (SSD_REP, TB), 1)
    dt = jnp.where(pos >= FRONT - N_META, dt, 0.0)
    la = -jnp.exp(_tile_lanes(alog_ref[...], nrep)) * dt
    ac = _dot_sel(la, u_ref[...])
    al = _dot_sel(ac, e_ref[...])
    dtek = dt * jnp.exp(al - ac)

    def col_bcast_heads(rowsv):
        stacked = jnp.concatenate(
            [jnp.broadcast_to(rowsv[r:r + 1], (SSD_HEAD_DIM, TB)) for r in range(SSD_REP)], axis=0)
        return stacked.T

    ac_cb = col_bcast_heads(ac)
    ea_cb = jnp.exp(ac_cb)
    xd = xh * col_bcast_heads(dtek)

    ii = lax.broadcasted_iota(jnp.int32, (CHUNK, 2 * CHUNK), 0)
    jj = lax.broadcasted_iota(jnp.int32, (CHUNK, 2 * CHUNK), 1)
    causal2 = ii >= (jj & (CHUNK - 1))
    li = lax.broadcasted_iota(jnp.int32, (CHUNK, 2 * CHUNK), 1)
    lo_half = li < CHUNK

    state = h_ref[...]
    outs = []
    for c in range(CPB):
        rows = slice(c * CHUNK, (c + 1) * CHUNK)
        cc = cm[rows]
        bc = bm[rows]
        cb = _bdot_nt(cc, bc)
        cb2 = jnp.concatenate([cb, cb], axis=1)
        y_off = _bdot(cc, state) * ea_cb[rows]
        yd = []
        for m in range(SSD_REP // 2):
            lanes = slice(m * 128, (m + 1) * 128)
            rowv = jnp.concatenate([ac[2 * m:2 * m + 1, rows], ac[2 * m + 1:2 * m + 2, rows]], axis=1)
            dtrow = jnp.concatenate([dt[2 * m:2 * m + 1, rows], dt[2 * m + 1:2 * m + 2, rows]], axis=1)
            diff = ac_cb[rows, lanes] - rowv
            seg = jnp.where(causal2, jnp.exp(jnp.where(causal2, diff, 0.0)), 0.0)
            mm = cb2 * seg * dtrow
            x2 = xh[rows, lanes]
            rhs = jnp.concatenate([jnp.where(lo_half, x2, 0.0), jnp.where(lo_half, 0.0, x2)], axis=0)
            yd.append(_bdot(mm, rhs))
        outs.append(jnp.concatenate(yd, axis=1) + y_off)
        ea_last = ea_cb[(c + 1) * CHUNK - 1:(c + 1) * CHUNK, :]
        state = state * ea_last + _bdot_tn(bc, xd[rows])
    h_ref[...] = state

    y = jnp.concatenate(outs, axis=0) + dexp_ref[...] * xh
    y = y * _silu(z_ref[...].astype(F32))
    y = y * lax.rsqrt(jnp.mean(y * y, axis=-1, keepdims=True) + EPS) * nw_ref[...]
    o_ref[...] = y.astype(o_ref.dtype)


def _mamba2_ssd(xbc, z, dt_rows, conv_w, conv_b, alog_b, dtb_b, d_exp, norm_w):
    rows = xbc.shape[0]
    gd = SSD_GROUP_DIM
    nx = SSD_D_INNER // SSD_STATE
    return pl.pallas_call(
        _ssd_kernel,
        grid=(SSD_GROUPS, rows // TB),
        in_specs=[
            pl.BlockSpec((TB, gd), lambda g, t: (t, g)),
            pl.BlockSpec((TB, SSD_STATE), lambda g, t: (t, nx + g)),
            pl.BlockSpec((TB, SSD_STATE), lambda g, t: (t, nx + SSD_GROUPS + g)),
            pl.BlockSpec((TB, gd), lambda g, t: (t, g)),
            pl.BlockSpec((None, SSD_REP, TB), lambda g, t: (g, 0, t)),
            pl.BlockSpec((4, gd), lambda g, t: (0, g)),
            pl.BlockSpec((4, SSD_STATE), lambda g, t: (0, nx + g)),
            pl.BlockSpec((4, SSD_STATE), lambda g, t: (0, nx + SSD_GROUPS + g)),
            pl.BlockSpec((1, gd), lambda g, t: (0, g)),
            pl.BlockSpec((1, SSD_STATE), lambda g, t: (0, nx + g)),
            pl.BlockSpec((1, SSD_STATE), lambda g, t: (0, nx + SSD_GROUPS + g)),
            pl.BlockSpec((None, SSD_REP, 128), lambda g, t: (g, 0, 0)),
            pl.BlockSpec((None, SSD_REP, 128), lambda g, t: (g, 0, 0)),
            pl.BlockSpec((1, gd), lambda g, t: (0, g)),
            pl.BlockSpec((1, gd), lambda g, t: (0, g)),
        ],
        out_specs=pl.BlockSpec((TB, gd), lambda g, t: (t, g)),
        out_shape=jax.ShapeDtypeStruct((rows, SSD_D_INNER), BF16),
        scratch_shapes=[
            pltpu.VMEM((SSD_STATE, gd), F32),
            pltpu.VMEM((8, gd), F32),
            pltpu.VMEM((8, SSD_STATE), F32),
            pltpu.VMEM((8, SSD_STATE), F32),
            pltpu.VMEM((TB, TB), BF16),
            pltpu.VMEM((TB, TB), BF16),
        ],
        compiler_params=_cparams(("arbitrary", "arbitrary")),
    )(xbc, xbc, xbc, z, dt_rows, conv_w, conv_w, conv_w, conv_b, conv_b, conv_b, alog_b, dtb_b, d_exp, norm_w)


def _merge_kernel(ydn_ref, yssd_ref, gd_ref, gs_ref, wdn_ref, wssd_ref, o_ref):
    a = jnp.dot(ydn_ref[...], wdn_ref[...], preferred_element_type=F32)
    b = jnp.dot(yssd_ref[...], wssd_ref[...], preferred_element_type=F32)
    o = _sigmoid(gd_ref[...].astype(F32)) * a + _sigmoid(gs_ref[...].astype(F32)) * b
    o_ref[...] = o.astype(o_ref.dtype)


def _merge(y_dn, y_ssd, gates, w_dn, w_ssd, tn=512):
    ntile = (y_dn.shape[0] - FRONT) // TB
    off = FRONT // TB
    ng = D_MODEL // tn
    return pl.pallas_call(
        _merge_kernel,
        grid=(ng, ntile),
        in_specs=[
            pl.BlockSpec((TB, DN_VAL_DIM), lambda j, i: (i + off, 0)),
            pl.BlockSpec((TB, SSD_D_INNER), lambda j, i: (i + off, 0)),
            pl.BlockSpec((TB, tn), lambda j, i: (i + off, j)),
            pl.BlockSpec((TB, tn), lambda j, i: (i + off, ng + j)),
            pl.BlockSpec((DN_VAL_DIM, tn), lambda j, i: (0, j)),
            pl.BlockSpec((SSD_D_INNER, tn), lambda j, i: (0, j)),
        ],
        out_specs=pl.BlockSpec((TB, tn), lambda j, i: (i, j)),
        out_shape=jax.ShapeDtypeStruct((ntile * TB, D_MODEL), BF16),
        compiler_params=_cparams(("arbitrary", "arbitrary")),
    )(y_dn, y_ssd, gates, gates, w_dn, w_ssd)


def _out_router_kernel(m_ref, wo_ref, x_ref, nw_ref, wr_ref, br_ref,
                       h2_ref, hn_ref, idx_ref, p_ref, rank_ref, cnt_ref, carry_ref, us_ref):
    i = pl.program_id(0)
    rt = x_ref.shape[0]

    @pl.when(i == 0)
    def _():
        carry_ref[...] = jnp.zeros_like(carry_ref)
        jj = lax.broadcasted_iota(jnp.int32, (rt, rt), 0)
        ii = lax.broadcasted_iota(jnp.int32, (rt, rt), 1)
        us_ref[...] = jnp.where(jj < ii, 1.0, 0.0).astype(BF16)

    h2 = x_ref[...] + jnp.dot(m_ref[...], wo_ref[...], preferred_element_type=F32)
    h2_ref[...] = h2
    hn = h2 * lax.rsqrt(jnp.mean(h2 * h2, axis=-1, keepdims=True) + EPS) * nw_ref[...]
    hn_ref[...] = hn

    lg = _dot3(wr_ref[...], hn, nt=True) + br_ref[...]
    eio = lax.broadcasted_iota(jnp.int32, (N_EXPERTS, rt), 0)
    vals, hits, idxs = [], [], []
    cur = lg
    for _ in range(TOP_K):
        m = jnp.max(cur, axis=0, keepdims=True)
        ik = jnp.min(jnp.where(cur == m, eio, N_EXPERTS), axis=0, keepdims=True)
        hit = eio == ik
        vals.append(m)
        idxs.append(ik)
        hits.append(hit)
        cur = jnp.where(hit, -jnp.inf, cur)
    ex = [jnp.exp(vk - vals[0]) for vk in vals]
    den = ex[0] + ex[1] + ex[2] + ex[3]
    idx_ref[...] = jnp.concatenate(idxs, axis=0)
    p_ref[...] = jnp.concatenate([e / den for e in ex], axis=0)

    sel = jnp.zeros((N_EXPERTS, rt), F32)
    for hit in hits:
        sel = sel + jnp.where(hit, 1.0, 0.0)
    before = jnp.dot(sel.astype(BF16), us_ref[...], preferred_element_type=F32) \
        + _tile_lanes(carry_ref[...], rt // 128)
    ranks = [jnp.sum(jnp.where(hit, before, 0.0), axis=0, keepdims=True) for hit in hits]
    rank_ref[...] = jnp.concatenate(ranks, axis=0).astype(jnp.int32)
    carry = carry_ref[...] + jnp.dot(sel.astype(BF16), jnp.ones((rt, 128), BF16), preferred_element_type=F32)
    carry_ref[...] = carry
    cnt_ref[...] = carry


def _out_router(merged, w_out, x, nw, w_router_t, b_router_c):
    n_tok = x.shape[0]
    rt = ROUTER_RT
    tok_spec = pl.BlockSpec((TOP_K, rt), lambda i: (0, i))
    return pl.pallas_call(
        _out_router_kernel,
        grid=(n_tok // rt,),
        in_specs=[
            pl.BlockSpec((rt, D_MODEL), lambda i: (i, 0)),
            pl.BlockSpec((D_MODEL, D_MODEL), lambda i: (0, 0), pipeline_mode=pl.Buffered(1)),
            pl.BlockSpec((rt, D_MODEL), lambda i: (i, 0)),
            pl.BlockSpec((1, D_MODEL), lambda i: (0, 0)),
            pl.BlockSpec((N_EXPERTS, D_MODEL), lambda i: (0, 0)),
            pl.BlockSpec((N_EXPERTS, 1), lambda i: (0, 0)),
        ],
        out_specs=[
            pl.BlockSpec((rt, D_MODEL), lambda i: (i, 0)),
            pl.BlockSpec((rt, D_MODEL), lambda i: (i, 0)),
            tok_spec, tok_spec, tok_spec,
            pl.BlockSpec((N_EXPERTS, 128), lambda i: (0, 0)),
        ],
        out_shape=[
            jax.ShapeDtypeStruct((n_tok, D_MODEL), F32),
            jax.ShapeDtypeStruct((n_tok, D_MODEL), F32),
            jax.ShapeDtypeStruct((TOP_K, n_tok), jnp.int32),
            jax.ShapeDtypeStruct((TOP_K, n_tok), F32),
            jax.ShapeDtypeStruct((TOP_K, n_tok), jnp.int32),
            jax.ShapeDtypeStruct((N_EXPERTS, 128), F32),
        ],
        scratch_shapes=[pltpu.VMEM((N_EXPERTS, 128), F32), pltpu.VMEM((rt, rt), BF16)],
        compiler_params=_cparams(("arbitrary",)),
    )(merged, w_out, x, nw, w_router_t, b_router_c)


def _gather_kernel(nused_ref, tok_ref, src_ref, o_ref, buf_ref, sem):
    b = pl.program_id(0)
    nu = nused_ref[0]

    def issue(blk, slot):
        base = blk * MOE_BM

        def body(g, carry):
            for s in range(8):
                tok = tok_ref[base + g * 8 + s]
                pltpu.make_async_copy(src_ref.at[tok >> 3, pl.ds(tok & 7, 1)],
                                      buf_ref.at[slot, g, pl.ds(s, 1)], sem.at[slot]).start()
            return carry

        lax.fori_loop(0, MOE_BM // 8, body, 0)

    @pl.when(b == 0)
    def _():
        issue(0, 0)

    @pl.when(b + 1 < nu)
    def _():
        issue(b + 1, (b + 1) % 2)

    @pl.when(b < nu)
    def _():
        slot = b % 2
        pltpu.make_async_copy(src_ref.at[pl.ds(0, MOE_BM // 8)], buf_ref.at[slot], sem.at[slot]).wait()
        o_ref[...] = buf_ref[slot].reshape(MOE_BM, D_MODEL).astype(o_ref.dtype)

    @pl.when(b >= nu)
    def _():
        o_ref[...] = jnp.zeros_like(o_ref)


def _gather_rows(n_used, row_tok, src):
    nb = row_tok.shape[0] // MOE_BM
    return pl.pallas_call(
        _gather_kernel,
        grid_spec=pltpu.PrefetchScalarGridSpec(
            num_scalar_prefetch=2,
            grid=(nb,),
            in_specs=[pl.BlockSpec(memory_space=pl.ANY)],
            out_specs=pl.BlockSpec((MOE_BM, D_MODEL), lambda b, nu, tok: (b, 0)),
            scratch_shapes=[pltpu.VMEM((2, MOE_BM // 8, 8, D_MODEL), F32), pltpu.SemaphoreType.DMA((2,))],
        ),
        out_shape=jax.ShapeDtypeStruct((nb * MOE_BM, D_MODEL), BF16),
        compiler_params=_cparams(("arbitrary",)),
    )(n_used, row_tok, src.reshape(src.shape[0] // 8, 8, D_MODEL))


def _expert_row_loop(first_blk, n, cols, src_ref, dst_ref, xbuf, obuf, in_sem, out_sem, compute):
    depth = xbuf.shape[0]

    def rows(i):
        return pl.ds(pl.multiple_of((first_blk + i) * MOE_BM, MOE_BM), MOE_BM)

    def xcopy(i):
        slot = i % depth
        return pltpu.make_async_copy(src_ref.at[rows(i)], xbuf.at[slot], in_sem.at[slot])

    def ocopy(i, slot):
        dst = dst_ref.at[rows(i)] if cols is None else dst_ref.at[rows(i), cols]
        return pltpu.make_async_copy(obuf.at[slot], dst, out_sem.at[slot])

    @pl.when(n > 0)
    def _():
        for p in range(depth - 1):
            @pl.when(p < n)
            def _(p=p):
                xcopy(p).start(priority=1)

        def body(i, carry):
            slot = i % 2

            @pl.when(i + depth - 1 < n)
            def _():
                xcopy(i + depth - 1).start(priority=1)

            xcopy(i).wait()

            @pl.when(i >= 2)
            def _():
                ocopy(i - 2, slot).wait()

            obuf[slot] = compute(xbuf[i % depth]).astype(obuf.dtype)
            ocopy(i, slot).start()
            return carry

        lax.fori_loop(0, n, body, 0)

        @pl.when(n >= 2)
        def _():
            ocopy(n - 2, n % 2).wait()

        ocopy(n - 1, (n - 1) % 2).wait()


def _zero_unused_blocks(n_used, cols, dst_ref, obuf, out_sem):
    nb = dst_ref.shape[0] // MOE_BM
    obuf[0] = jnp.zeros(obuf.shape[1:], obuf.dtype)

    def body(i, carry):
        r = pl.ds(pl.multiple_of(i * MOE_BM, MOE_BM), MOE_BM)
        dst = dst_ref.at[r] if cols is None else dst_ref.at[r, cols]
        cp = pltpu.make_async_copy(obuf.at[0], dst, out_sem.at[0])
        cp.start()
        cp.wait()
        return carry

    lax.fori_loop(n_used, nb, body, 0)


def _expert_up_kernel(first_ref, nblk_ref, nused_ref, xs_ref, wg_ref, wu_ref, bg_ref, bu_ref, act_ref,
                      xbuf, obuf, wgb_ref, wub_ref, in_sem, out_sem):
    e = pl.program_id(0)
    j = pl.program_id(1)
    tn = wgb_ref.shape[1]
    wgb_ref[...] = wg_ref[...].astype(BF16)
    wub_ref[...] = wu_ref[...].astype(BF16)

    def compute(x):
        gt = jnp.dot(x, wgb_ref[...], preferred_element_type=F32) + bg_ref[...]
        up = jnp.dot(x, wub_ref[...], preferred_element_type=F32) + bu_ref[...]
        gt = jnp.minimum(gt, SWIGLU_LIMIT)
        up = jnp.clip(up, -SWIGLU_LIMIT, SWIGLU_LIMIT)
        return gt * _sigmoid(SWIGLU_ALPHA * gt) * (up + 1.0)

    cols = pl.ds(pl.multiple_of(j * tn, tn), tn)
    _expert_row_loop(first_ref[e], nblk_ref[e], cols, xs_ref, act_ref, xbuf, obuf, in_sem, out_sem, compute)

    @pl.when(e == N_EXPERTS - 1)
    def _():
        _zero_unused_blocks(nused_ref[0], cols, act_ref, obuf, out_sem)


def _expert_up(first_blk, n_blk, n_used, xs, w_gate, w_up, b_gate, b_up):
    tn = MOE_TN_UP
    wspec = pl.BlockSpec((None, D_MODEL, tn), lambda e, j, *_: (e, 0, j))
    bspec = pl.BlockSpec((None, 1, tn), lambda e, j, *_: (e, 0, j))
    return pl.pallas_call(
        _expert_up_kernel,
        grid_spec=pltpu.PrefetchScalarGridSpec(
            num_scalar_prefetch=3,
            grid=(N_EXPERTS, D_MODEL // tn),
            in_specs=[pl.BlockSpec(memory_space=pl.ANY), wspec, wspec, bspec, bspec],
            out_specs=pl.BlockSpec(memory_space=pl.ANY),
            scratch_shapes=[pltpu.VMEM((MOE_XRING, MOE_BM, D_MODEL), BF16), pltpu.VMEM((2, MOE_BM, tn),BF16),
                            pltpu.VMEM((D_MODEL, tn), BF16), pltpu.VMEM((D_MODEL, tn), BF16),
                            pltpu.SemaphoreType.DMA((MOE_XRING,)), pltpu.SemaphoreType.DMA((2,))],
        ),
        out_shape=jax.ShapeDtypeStruct(xs.shape, BF16),
        compiler_params=_cparams(("arbitrary", "arbitrary")),
    )(first_blk, n_blk, n_used, xs, w_gate, w_up, b_gate, b_up)


def _expert_down_kernel(first_ref, nblk_ref, nused_ref, act_ref, wd_ref, bd_ref, y_ref,
                        xbuf, obuf, wdb_ref, in_sem, out_sem):
    e = pl.program_id(0)
    j = pl.program_id(1)
    tn = wdb_ref.shape[1]
    wdb_ref[...] = wd_ref[...].astype(BF16)

    def compute(a):
        return jnp.dot(a, wdb_ref[...], preferred_element_type=F32) + bd_ref[...]

    cols = pl.ds(pl.multiple_of(j * tn, tn), tn)
    _expert_row_loop(first_ref[e], nblk_ref[e], cols, act_ref, y_ref, xbuf, obuf, in_sem, out_sem, compute)

    @pl.when(e == N_EXPERTS - 1)
    def _():
        _zero_unused_blocks(nused_ref[0], cols, y_ref, obuf, out_sem)


def _expert_down(first_blk, n_blk, n_used, act, w_down, b_down):
    tn = MOE_TN_DOWN
    return pl.pallas_call(
        _expert_down_kernel,
        grid_spec=pltpu.PrefetchScalarGridSpec(
            num_scalar_prefetch=3,
            grid=(N_EXPERTS, D_MODEL // tn),
            in_specs=[pl.BlockSpec(memory_space=pl.ANY),
                      pl.BlockSpec((None, D_MODEL, tn), lambda e, j, *_: (e, 0, j)),
                      pl.BlockSpec((None, 1, tn), lambda e, j, *_: (e, 0, j))],
            out_specs=pl.BlockSpec(memory_space=pl.ANY),
            scratch_shapes=[pltpu.VMEM((MOE_XRING, MOE_BM, D_MODEL), BF16), pltpu.VMEM((2, MOE_BM, tn),F32),
                            pltpu.VMEM((D_MODEL, tn), BF16),
                            pltpu.SemaphoreType.DMA((MOE_XRING,)), pltpu.SemaphoreType.DMA((2,))],
        ),
        out_shape=jax.ShapeDtypeStruct(act.shape, F32),
        compiler_params=_cparams(("arbitrary", "arbitrary")),
    )(first_blk, n_blk, n_used, act, w_down, b_down)


def _combine_kernel(dest_ref, h2_ref, p_ref, fw_ref, y_ref, o_ref, buf_ref, sem):
    i = pl.program_id(0)
    tt = COMB_TT

    def issue(tile, slot):
        base = tile * (TOP_K * tt)

        def body(r, carry):
            for k in range(TOP_K):
                pltpu.make_async_copy(y_ref.at[pl.ds(dest_ref[base + k * tt + r], 1)],
                                      buf_ref.at[slot, k, pl.ds(r, 1)], sem.at[slot]).start()
            return carry

        lax.fori_loop(0, tt, body, 0, unroll=4)

    @pl.when(i == 0)
    def _():
        issue(0, 0)

    @pl.when(i + 1 < pl.num_programs(0))
    def _():
        issue(i + 1, (i + 1) % 2)

    slot = i % 2
    for k in range(TOP_K):
        pltpu.make_async_copy(y_ref.at[pl.ds(0, tt)], buf_ref.at[slot, k], sem.at[slot]).wait()
    p = p_ref[...]
    acc = h2_ref[...]
    for k in range(TOP_K):
        acc = acc + p[:, k:k + 1] * buf_ref[slot, k]
    o_ref[...] = acc * lax.rsqrt(jnp.mean(acc * acc, axis=-1, keepdims=True) + EPS) * fw_ref[...]


def _combine(dest, h2, p_cols, final_w, y_rows):
    n_tok = h2.shape[0]
    tt = COMB_TT
    return pl.pallas_call(
        _combine_kernel,
        grid_spec=pltpu.PrefetchScalarGridSpec(
            num_scalar_prefetch=1,
            grid=(n_tok // tt,),
            in_specs=[
                pl.BlockSpec((tt, D_MODEL), lambda i, d: (i, 0)),
                pl.BlockSpec((tt, TOP_K), lambda i, d: (i, 0)),
                pl.BlockSpec((1, D_MODEL), lambda i, d: (0, 0)),
                pl.BlockSpec(memory_space=pl.ANY),
            ],
            out_specs=pl.BlockSpec((tt, D_MODEL), lambda i, d: (i, 0)),
            scratch_shapes=[pltpu.VMEM((2, TOP_K, tt, D_MODEL), F32), pltpu.SemaphoreType.DMA((2,))],
        ),
        out_shape=jax.ShapeDtypeStruct((n_tok, D_MODEL), F32),
        compiler_params=_cparams(("arbitrary",)),
    )(dest, h2, p_cols, final_w, y_rows)


def _mixer(x2, meta, norm_w, w_in, dn_conv_w, dn_a_log, dn_dt_bias, dn_norm_w,
           ssd_conv_w, ssd_conv_b, ssd_a_log, ssd_dt_bias, ssd_d, ssd_norm_w, w_proj_dn, w_proj_ssd):
    o_qkv = 0
    o_dnz = o_qkv + DN_CONV_DIM
    o_b = o_dnz + DN_VAL_DIM
    o_a = o_b + DN_V_HEADS
    o_sz = o_a + DN_V_HEADS
    o_xbc = o_sz + SSD_D_INNER
    o_dt = o_xbc + SSD_CONV_DIM
    o_gate = o_dt + SSD_HEADS
    d_in = o_gate + 2 * D_MODEL
    w_small = jnp.concatenate([w_in[:, o_b:o_sz], w_in[:, o_dt:o_gate]], axis=1)

    hn, small = _norm_small(x2, meta, norm_w.reshape(1, D_MODEL), w_small)
    qkv = _matmul(hn, w_in[:, o_qkv:o_dnz].astype(BF16), BF16)
    dn_z = _matmul(hn, w_in[:, o_dnz:o_b].astype(BF16), BF16)
    ssd_z = _matmul(hn, w_in[:, o_sz:o_xbc].astype(BF16), BF16)
    xbc = _matmul(hn, w_in[:, o_xbc:o_dt].astype(BF16), BF16)
    gates = _matmul(hn, w_in[:, o_gate:d_in].astype(BF16), BF16)

    rows = hn.shape[0]
    sm_t = small.T
    dn_steps, dn_vps = DN_QK_HEADS // DN_QPS, 2 * DN_QPS
    b_rows = sm_t[0:DN_V_HEADS].reshape(dn_steps, dn_vps, rows)
    a_rows = sm_t[DN_V_HEADS:2 * DN_V_HEADS].reshape(dn_steps, dn_vps, rows)
    dt_rows = sm_t[2 * DN_V_HEADS:].reshape(SSD_GROUPS, SSD_REP, rows)

    bc = lambda p, g, r: jnp.broadcast_to(p.astype(F32).reshape(g, r, 1), (g, r, 128))
    y_dn = _gated_deltanet(qkv, dn_z, b_rows, a_rows, dn_conv_w,
                           bc(dn_a_log, dn_steps, dn_vps), bc(dn_dt_bias, dn_steps, dn_vps),
                           dn_norm_w.reshape(1, DN_HEAD_DIM))
    y_ssd = _mamba2_ssd(xbc, ssd_z, dt_rows, ssd_conv_w, ssd_conv_b.reshape(1, SSD_CONV_DIM),
                        bc(ssd_a_log, SSD_GROUPS, SSD_REP), bc(ssd_dt_bias, SSD_GROUPS, SSD_REP),
                        jnp.repeat(ssd_d, SSD_HEAD_DIM).reshape(1, SSD_D_INNER),
                        ssd_norm_w.reshape(1, SSD_D_INNER))
    return _merge(y_dn, y_ssd, gates, w_proj_dn.astype(BF16), w_proj_ssd.astype(BF16))


def _moe_and_final(merged, x2, w_out, norm_ffn_w, w_router, b_router, w_gate, b_gate, w_up, b_up,
                   w_down, b_down, final_norm_w):
    n_tok = x2.shape[0]
    h2, hn2, idx, probs, rank, cnt = _out_router(
        merged, w_out.astype(BF16), x2, norm_ffn_w.reshape(1, D_MODEL), w_router.T,
        b_router.reshape(N_EXPERTS, 1))

    counts = cnt[:, 0].astype(jnp.int32)
    padded = (counts + MOE_BM - 1) // MOE_BM * MOE_BM
    pad_end = jnp.cumsum(padded)
    pad_start = pad_end - padded
    hit = idx[:, :, None] == jnp.arange(N_EXPERTS, dtype=jnp.int32)
    dest = jnp.sum(jnp.where(hit, pad_start.astype(jnp.int32), 0), axis=-1) + rank
    nb = -(-(n_tok * TOP_K + N_EXPERTS * (MOE_BM - 1)) // MOE_BM)
    tok_ids = jnp.broadcast_to(jnp.arange(n_tok, dtype=jnp.int32)[None], (TOP_K, n_tok))
    row_tok = jnp.zeros((nb * MOE_BM,), jnp.int32).at[dest.reshape(-1)].set(tok_ids.reshape(-1))
    first_blk = (pad_start // MOE_BM).astype(jnp.int32)
    n_blk = (padded // MOE_BM).astype(jnp.int32)
    n_used = (pad_end[-1:] // MOE_BM).astype(jnp.int32)

    xs = _gather_rows(n_used, row_tok, hn2)
    act = _expert_up(first_blk, n_blk, n_used, xs, w_gate, w_up,
                     b_gate.reshape(N_EXPERTS, 1, D_MODEL), b_up.reshape(N_EXPERTS, 1, D_MODEL))
    y_rows = _expert_down(first_blk, n_blk, n_used, act, w_down, b_down.reshape(N_EXPERTS, 1, D_MODEL))
    dest_t = dest.reshape(TOP_K, n_tok // COMB_TT, COMB_TT).transpose(1, 0, 2).reshape(-1)
    return _combine(dest_t, h2, probs.T, final_norm_w.reshape(1, D_MODEL), y_rows)


def kernel(x, meta_tokens, norm_mix_w, w_in, dn_conv_w, dn_a_log, dn_dt_bias, dn_norm_w, ssd_conv_w, ssd_conv_b, ssd_a_log, ssd_dt_bias, ssd_d, ssd_norm_w, w_proj_dn, w_proj_ssd, w_out, norm_ffn_w, w_router, b_router, w_gate, b_gate, w_up, b_up, w_down, b_down, final_norm_w):
    bsz, seq, d = x.shape
    assert bsz == 1 and d == D_MODEL and seq % TB == 0
    assert norm_mix_w.shape[0] == 1, "single-layer block"
    x2 = x.reshape(seq, d)
    merged = _mixer(x2, meta_tokens.astype(x.dtype), norm_mix_w[0], w_in[0], dn_conv_w[0], dn_a_log[0], dn_dt_bias[0], dn_norm_w[0],
                    ssd_conv_w[0], ssd_conv_b[0], ssd_a_log[0], ssd_dt_bias[0], ssd_d[0], ssd_norm_w[0],
                    w_proj_dn[0], w_proj_ssd[0])
    out = _moe_and_final(merged, x2, w_out[0], norm_ffn_w[0], w_router[0], b_router[0], w_gate[0], b_gate[0],
                         w_up[0], b_up[0], w_down[0], b_down[0], final_norm_w)
    return out.reshape(bsz, seq, d)
```

```python
import functools

import jax
import jax.numpy as jnp
from jax import lax
from jax.experimental import pallas as pl
from jax.experimental.pallas import tpu as pltpu

F32 = jnp.float32
BF16 = jnp.bfloat16

D_MODEL = 2048
N_META = 16
CHUNK = 64
EPS = 1e-6

DN_QK_HEADS = 16
DN_V_HEADS = 32
DN_HEAD_DIM = 128
DN_KEY_DIM = DN_QK_HEADS * DN_HEAD_DIM
DN_VAL_DIM = DN_V_HEADS * DN_HEAD_DIM
DN_CONV_DIM = 2 * DN_KEY_DIM + DN_VAL_DIM

SSD_D_INNER = 2 * D_MODEL
SSD_HEAD_DIM = 64
SSD_HEADS = SSD_D_INNER // SSD_HEAD_DIM
SSD_GROUPS = 8
SSD_REP = SSD_HEADS // SSD_GROUPS
SSD_STATE = 128
SSD_GS = SSD_GROUPS * SSD_STATE
SSD_CONV_DIM = SSD_D_INNER + 2 * SSD_GS
SSD_GROUP_DIM = SSD_D_INNER // SSD_GROUPS

N_EXPERTS = 32
TOP_K = 4
SWIGLU_LIMIT = 7.0
SWIGLU_ALPHA = 1.702

TB = 512
FRONT = TB
CPB = TB // CHUNK
MOE_BM = 256
MOE_TN_UP = 1024
MOE_TN_DOWN = 2048
COMB_TT = 256
ROUTER_RT = 512
DN_GROUP = 2
DN_QPS = 4
VMEM_LIMIT = 56 * 1024 * 1024


def _cparams(sem):
    return pltpu.CompilerParams(dimension_semantics=sem, vmem_limit_bytes=VMEM_LIMIT)


def _bdot(a, b):
    return jnp.dot(a.astype(BF16), b.astype(BF16), preferred_element_type=F32)


def _bdot_nt(a, b):
    return lax.dot_general(a.astype(BF16), b.astype(BF16), (((1,), (1,)), ((), ())),
                           preferred_element_type=F32)


def _bdot_tn(a, b):
    return lax.dot_general(a.astype(BF16), b.astype(BF16), (((0,), (0,)), ((), ())),
                           preferred_element_type=F32)


def _split3(x):
    x1 = x.astype(BF16)
    r1 = x - x1.astype(F32)
    x2 = r1.astype(BF16)
    x3 = (r1 - x2.astype(F32)).astype(BF16)
    return x1, x2, x3


def _dot_sel(x, m):
    x1, x2, x3 = _split3(x)
    d = lambda a: jnp.dot(a, m, preferred_element_type=F32)
    return d(x1) + d(x2) + d(x3)


def _dot3(a, b, nt=False):
    a1 = a.astype(BF16)
    a2 = (a - a1.astype(F32)).astype(BF16)
    b1 = b.astype(BF16)
    b2 = (b - b1.astype(F32)).astype(BF16)
    if nt:
        d = lambda p, q: lax.dot_general(p, q, (((1,), (1,)), ((), ())), preferred_element_type=F32)
    else:
        d = lambda p, q: jnp.dot(p, q, preferred_element_type=F32)
    return d(a1, b1) + d(a2, b1) + d(a1, b2)


def _sigmoid(x):
    return 1.0 / (1.0 + jnp.exp(-x))


def _silu(x):
    return x * _sigmoid(x)


def _softplus(x):
    return jnp.maximum(x, 0.0) + jnp.log(1.0 + jnp.exp(-jnp.abs(x)))


def _tile_lanes(p, n):
    return jnp.concatenate([p] * n, axis=1)


def _norm_small_kernel(x_ref, meta_ref, nw_ref, ws_ref, hn_ref, sm_ref):
    i = pl.program_id(0)

    def emit(h):
        y = h * lax.rsqrt(jnp.mean(h * h, axis=-1, keepdims=True) + EPS) * nw_ref[...]
        hn_ref[...] = y.astype(BF16)
        sm_ref[...] = _dot3(y, ws_ref[...])

    @pl.when(i == 0)
    def _():
        emit(jnp.concatenate([jnp.zeros((FRONT - N_META, D_MODEL), F32), meta_ref[...]], axis=0))

    @pl.when(i > 0)
    def _():
        emit(x_ref[...])


def _norm_small(x2, meta, nw, w_small):
    rows = FRONT + x2.shape[0]
    return pl.pallas_call(
        _norm_small_kernel,
        grid=(rows // TB,),
        in_specs=[pl.BlockSpec((TB, D_MODEL), lambda i: (jnp.maximum(i - 1, 0), 0)),
                  pl.BlockSpec((N_META, D_MODEL), lambda i: (0, 0)),
                  pl.BlockSpec((1, D_MODEL), lambda i: (0, 0)),
                  pl.BlockSpec((D_MODEL, 128), lambda i: (0, 0))],
        out_specs=[pl.BlockSpec((TB, D_MODEL), lambda i: (i, 0)),
                   pl.BlockSpec((TB, 128), lambda i: (i, 0))],
        out_shape=[jax.ShapeDtypeStruct((rows, D_MODEL), BF16),
                   jax.ShapeDtypeStruct((rows, 128), F32)],
        compiler_params=_cparams(("arbitrary",)),
    )(x2, meta, nw, w_small)


def _mm_kernel(x_ref, w_ref, o_ref):
    o_ref[...] = jnp.dot(x_ref[...], w_ref[...], preferred_element_type=F32).astype(o_ref.dtype)


def _matmul(x, w, out_dtype, tm=1024, tn=1024):
    m, k = x.shape
    n = w.shape[1]
    return pl.pallas_call(
        _mm_kernel,
        grid=(n // tn, pl.cdiv(m, tm)),
        in_specs=[pl.BlockSpec((tm, k), lambda j, i: (i, 0)),
                  pl.BlockSpec((k, tn), lambda j, i: (0, j))],
        out_specs=pl.BlockSpec((tm, tn), lambda j, i: (i, j)),
        out_shape=jax.ShapeDtypeStruct((m, n), out_dtype),
        compiler_params=_cparams(("arbitrary", "arbitrary")),
    )(x, w)


def _conv_silu(raw_ref, tail_ref, w_ref, b_ref=None):
    x = raw_ref[...].astype(F32)
    rows = x.shape[0]
    ext = jnp.concatenate([tail_ref[...], x], axis=0)
    w = w_ref[...]
    y = x * w[3:4]
    for j in (1, 2, 3):
        y = y + pltpu.roll(ext, j, 0)[8:8 + rows] * w[3 - j:4 - j]
    tail_ref[...] = x[rows - 8:rows]
    if b_ref is not None:
        y = y + b_ref[...]
    return _silu(y)


def _fill_chunk_matrices(u_ref, e_ref):
    jj = lax.broadcasted_iota(jnp.int32, (TB, TB), 0)
    ii = lax.broadcasted_iota(jnp.int32, (TB, TB), 1)
    same = (jj // CHUNK) == (ii // CHUNK)
    u_ref[...] = jnp.where(same & (jj <= ii), 1.0, 0.0).astype(BF16)
    e_ref[...] = jnp.where(same & (jj % CHUNK == CHUNK - 1), 1.0, 0.0).astype(BF16)


def _col_bcast(row, width):
    return jnp.broadcast_to(row, (width, TB)).T


def _dn_kernel(q_ref, k_ref, v_ref, z_ref, b_ref, a_ref, wq_ref, wk_ref, wv_ref, alog_ref, dtb_ref, nw_ref,
               o_ref, s_ref, tq_ref, tk_ref, tv_ref, u_ref, e_ref):
    t = pl.program_id(1)

    @pl.when(t == 0)
    def _():
        s_ref[...] = jnp.zeros_like(s_ref)
        tq_ref[...] = jnp.zeros_like(tq_ref)
        tk_ref[...] = jnp.zeros_like(tk_ref)
        tv_ref[...] = jnp.zeros_like(tv_ref)
        _fill_chunk_matrices(u_ref, e_ref)

    d = DN_HEAD_DIM
    nq = q_ref.shape[1] // d
    q_all = _conv_silu(q_ref, tq_ref, wq_ref)
    k_all = _conv_silu(k_ref, tk_ref, wk_ref)
    v = _conv_silu(v_ref, tv_ref, wv_ref)
    qs, ks = [], []
    for i in range(nq):
        q = q_all[:, i * d:(i + 1) * d]
        k = k_all[:, i * d:(i + 1) * d]
        qs.append(q * (lax.rsqrt(jnp.sum(q * q, axis=-1, keepdims=True) + EPS) * (d ** -0.5)))
        ks.append(k * lax.rsqrt(jnp.sum(k * k, axis=-1, keepdims=True) + EPS))

    beta = _sigmoid(b_ref[...])
    g = -jnp.exp(_tile_lanes(alog_ref[...], TB // 128)) * _softplus(a_ref[...] + _tile_lanes(dtb_ref[...], TB // 128))
    gc = _dot_sel(g, u_ref[...])
    gl = _dot_sel(gc, e_ref[...])
    eg = jnp.exp(gc)
    ek = jnp.exp(gl - gc)

    gw = DN_GROUP * CHUNK
    lg = CHUNK.bit_length() - 1
    ii = lax.broadcasted_iota(jnp.int32, (gw, gw), 0)
    jj = lax.broadcasted_iota(jnp.int32, (gw, gw), 1)
    same = (ii >> lg) == (jj >> lg)
    causal = same & (ii >= jj)
    strict = same & (ii > jj)
    eye = jnp.where(ii == jj, 1.0, 0.0)
    masks = [strict & ((ii >> 1) == (jj >> 1))]
    blk = 2
    while blk < CHUNK:
        sh = blk.bit_length() - 1
        masks.append(((ii >> (sh + 1)) == (jj >> (sh + 1))) & (((ii >> sh) & 1) == 1) & (((jj >> sh) & 1) == 0))
        blk *= 2

    heads = range(2 * nq)
    chunks = range(CPB)
    groups = range(TB // gw)
    rows_of = [slice(c * CHUNK, (c + 1) * CHUNK) for c in chunks]
    grows = [slice(s * gw, (s + 1) * gw) for s in groups]
    kk = [[_bdot_nt(ks[i][grows[s]], ks[i][grows[s]]) for s in groups] for i in range(nq)]
    qk = [[_bdot_nt(qs[i][grows[s]], ks[i][grows[s]]) for s in groups] for i in range(nq)]

    eg_cb, rhs_all, qd_all, kd_all = [], [], [], []
    dec, a = {}, {}
    for hh in heads:
        q, k = qs[hh // 2], ks[hh // 2]
        gc_cb = _col_bcast(gc[hh:hh + 1], max(gw, DN_HEAD_DIM))
        beta_cb = _col_bcast(beta[hh:hh + 1], max(gw, DN_HEAD_DIM))
        eg_cb.append(_col_bcast(eg[hh:hh + 1], DN_HEAD_DIM))
        ek_cb = _col_bcast(ek[hh:hh + 1], DN_HEAD_DIM)
        beta_h = beta_cb[:, :DN_HEAD_DIM]
        vh = v[:, hh * DN_HEAD_DIM:(hh + 1) * DN_HEAD_DIM]
        rhs_all.append(jnp.concatenate([vh * beta_h, k * (beta_h * eg_cb[hh])], axis=1))
        qd_all.append(q * eg_cb[hh])
        kd_all.append(k * ek_cb)
        for s in groups:
            diff = gc_cb[grows[s], :gw] - gc[hh:hh + 1, s * gw:(s + 1) * gw]
            dec[hh, s] = jnp.where(causal, jnp.exp(jnp.where(causal, diff, 0.0)), 0.0)
            a[hh, s] = jnp.where(strict, beta_cb[grows[s], :gw] * kk[hh // 2][s] * dec[hh, s], 0.0)
    inst = [(hh, s) for s in groups for hh in heads]

    tinv = {i: eye - jnp.where(masks[0], a[i], 0.0) for i in inst}
    for m in masks[1:]:
        x = {i: _bdot(jnp.where(m, a[i], 0.0), tinv[i]) for i in inst}
        tinv = {i: tinv[i] - _bdot(tinv[i], x[i]) for i in inst}
    uw_g = {(hh, s): _bdot(tinv[hh, s], rhs_all[hh][grows[s]]) for hh, s in inst}
    qo_g = {(hh, s): _bdot(qk[hh // 2][s] * dec[hh, s], uw_g[hh, s]) for hh, s in inst}
    in_group = [slice((c % DN_GROUP) * CHUNK, (c % DN_GROUP + 1) * CHUNK) for c in chunks]
    uw = {(hh, c): uw_g[hh, c // DN_GROUP][in_group[c]] for c in chunks for hh in heads}
    qo = {(hh, c): qo_g[hh, c // DN_GROUP][in_group[c]] for c in chunks for hh in heads}
    nk = {(hh, c): _bdot_tn(uw[hh, c], kd_all[hh][rows_of[c]]) for c in chunks for hh in heads}

    state_t = [s_ref[hh] for hh in heads]
    before = {}
    for c in chunks:
        for hh in heads:
            before[hh, c] = state_t[hh]
            g_end = eg_cb[hh][(c + 1) * CHUNK - 1:(c + 1) * CHUNK, :]
            state_t[hh] = (state_t[hh] * g_end - _bdot(state_t[hh], nk[hh, c][DN_HEAD_DIM:])
                           + nk[hh, c][:DN_HEAD_DIM])
    nw = nw_ref[...]
    for hh in heads:
        s_ref[hh] = state_t[hh]
        outs = [_bdot_nt(qd_all[hh][rows_of[c]] - qo[hh, c][:, DN_HEAD_DIM:], before[hh, c])
                + qo[hh, c][:, :DN_HEAD_DIM] for c in chunks]
        o = jnp.concatenate(outs, axis=0)
        zh = z_ref[:, hh * DN_HEAD_DIM:(hh + 1) * DN_HEAD_DIM].astype(F32)
        o = o * lax.rsqrt(jnp.mean(o * o, axis=-1, keepdims=True) + EPS) * nw * _silu(zh)
        o_ref[:, hh * DN_HEAD_DIM:(hh + 1) * DN_HEAD_DIM] = o.astype(o_ref.dtype)


def _gated_deltanet(qkv, z, b_rows, a_rows, conv_w, alog_b, dtb_b, norm_w):
    rows = qkv.shape[0]
    nstep = DN_QK_HEADS // DN_QPS
    qw = DN_QPS * DN_HEAD_DIM
    return pl.pallas_call(
        _dn_kernel,
        grid=(nstep, rows // TB),
        in_specs=[
            pl.BlockSpec((TB, qw), lambda j, t: (t, j)),
            pl.BlockSpec((TB, qw), lambda j, t: (t, nstep + j)),
            pl.BlockSpec((TB, 2 * qw), lambda j, t: (t, nstep + j)),
            pl.BlockSpec((TB, 2 * qw), lambda j, t: (t, j)),
            pl.BlockSpec((None, 2 * DN_QPS, TB), lambda j, t: (j, 0, t)),
            pl.BlockSpec((None, 2 * DN_QPS, TB), lambda j, t: (j, 0, t)),
            pl.BlockSpec((4, qw), lambda j, t: (0, j)),
            pl.BlockSpec((4, qw), lambda j, t: (0, nstep + j)),
            pl.BlockSpec((4, 2 * qw), lambda j, t: (0, nstep + j)),
            pl.BlockSpec((None, 2 * DN_QPS, 128), lambda j, t: (j, 0, 0)),
            pl.BlockSpec((None, 2 * DN_QPS, 128), lambda j, t: (j, 0, 0)),
            pl.BlockSpec((1, DN_HEAD_DIM), lambda j, t: (0, 0)),
        ],
        out_specs=pl.BlockSpec((TB, 2 * qw), lambda j, t: (t, j)),
        out_shape=jax.ShapeDtypeStruct((rows, DN_VAL_DIM), BF16),
        scratch_shapes=[
            pltpu.VMEM((2 * DN_QPS, DN_HEAD_DIM, DN_HEAD_DIM), F32),
            pltpu.VMEM((8, qw), F32),
            pltpu.VMEM((8, qw), F32),
            pltpu.VMEM((8, 2 * qw), F32),
            pltpu.VMEM((TB, TB), BF16),
            pltpu.VMEM((TB, TB), BF16),
        ],
        compiler_params=_cparams(("arbitrary", "arbitrary")),
    )(qkv, qkv, qkv, z, b_rows, a_rows, conv_w, conv_w, conv_w, alog_b, dtb_b, norm_w)


def _ssd_kernel(x_ref, b_ref, c_ref, z_ref, dt_ref, wx_ref, wb_ref, wc_ref, bx_ref, bb_ref, bc_ref,
                alog_ref, dtb_ref, dexp_ref, nw_ref, o_ref, h_ref, tx_ref, tb_ref, tc_ref, u_ref, e_ref):
    t = pl.program_id(1)

    @pl.when(t == 0)
    def _():
        h_ref[...] = jnp.zeros_like(h_ref)
        tx_ref[...] = jnp.zeros_like(tx_ref)
        tb_ref[...] = jnp.zeros_like(tb_ref)
        tc_ref[...] = jnp.zeros_like(tc_ref)
        _fill_chunk_matrices(u_ref, e_ref)

    xh = _conv_silu(x_ref, tx_ref, wx_ref, bx_ref)
    bm = _conv_silu(b_ref, tb_ref, wb_ref, bb_ref)
    cm = _conv_silu(c_ref, tc_ref, wc_ref, bc_ref)

    nrep = TB // 128
    dt = _softplus(dt_ref[...] + _tile_lanes(dtb_ref[...], nrep))
    pos = t * TB + lax.broadcasted_iota(jnp.int32, (SSD_REP, TB), 1)
    dt = jnp.where(pos >= FRONT - N_META, dt, 0.0)
    la = -jnp.exp(_tile_lanes(alog_ref[...], nrep)) * dt
    ac = _dot_sel(la, u_ref[...])
    al = _dot_sel(ac, e_ref[...])
    dtek = dt * jnp.exp(al - ac)

    def col_bcast_heads(rowsv):
        stacked = jnp.concatenate(
            [jnp.broadcast_to(rowsv[r:r + 1], (SSD_HEAD_DIM, TB)) for r in range(SSD_REP)], axis=0)
        return stacked.T

    ac_cb = col_bcast_heads(ac)
    ea_cb = jnp.exp(ac_cb)
    xd = xh * col_bcast_heads(dtek)

    ii = lax.broadcasted_iota(jnp.int32, (CHUNK, 2 * CHUNK), 0)
    jj = lax.broadcasted_iota(jnp.int32, (CHUNK, 2 * CHUNK), 1)
    causal2 = ii >= (jj & (CHUNK - 1))
    li = lax.broadcasted_iota(jnp.int32, (CHUNK, 2 * CHUNK), 1)
    lo_half = li < CHUNK

    state = h_ref[...]
    outs = []
    for c in range(CPB):
        rows = slice(c * CHUNK, (c + 1) * CHUNK)
        cc = cm[rows]
        bc = bm[rows]
        cb = _bdot_nt(cc, bc)
        cb2 = jnp.concatenate([cb, cb], axis=1)
        y_off = _bdot(cc, state) * ea_cb[rows]
        yd = []
        for m in range(SSD_REP // 2):
            lanes = slice(m * 128, (m + 1) * 128)
            rowv = jnp.concatenate([ac[2 * m:2 * m + 1, rows], ac[2 * m + 1:2 * m + 2, rows]], axis=1)
            dtrow = jnp.concatenate([dt[2 * m:2 * m + 1, rows], dt[2 * m + 1:2 * m + 2, rows]], axis=1)
            diff = ac_cb[rows, lanes] - rowv
            seg = jnp.where(causal2, jnp.exp(jnp.where(causal2, diff, 0.0)), 0.0)
            mm = cb2 * seg * dtrow
            x2 = xh[rows, lanes]
            rhs = jnp.concatenate([jnp.where(lo_half, x2, 0.0), jnp.where(lo_half, 0.0, x2)], axis=0)
            yd.append(_bdot(mm, rhs))
        outs.append(jnp.concatenate(yd, axis=1) + y_off)
        ea_last = ea_cb[(c + 1) * CHUNK - 1:(c + 1) * CHUNK, :]
        state = state * ea_last + _bdot_tn(bc, xd[rows])
    h_ref[...] = state

    y = jnp.concatenate(outs, axis=0) + dexp_ref[...] * xh
    y = y * _silu(z_ref[...].astype(F32))
    y = y * lax.rsqrt(jnp.mean(y * y, axis=-1, keepdims=True) + EPS) * nw_ref[...]
    o_ref[...] = y.astype(o_ref.dtype)


def _mamba2_ssd(xbc, z, dt_rows, conv_w, conv_b, alog_b, dtb_b, d_exp, norm_w):
    rows = xbc.shape[0]
    gd = SSD_GROUP_DIM
    nx = SSD_D_INNER // SSD_STATE
    return pl.pallas_call(
        _ssd_kernel,
        grid=(SSD_GROUPS, rows // TB),
        in_specs=[
            pl.BlockSpec((TB, gd), lambda g, t: (t, g)),
            pl.BlockSpec((TB, SSD_STATE), lambda g, t: (t, nx + g)),
            pl.BlockSpec((TB, SSD_STATE), lambda g, t: (t, nx + SSD_GROUPS + g)),
            pl.BlockSpec((TB, gd), lambda g, t: (t, g)),
            pl.BlockSpec((None, SSD_REP, TB), lambda g, t: (g, 0, t)),
            pl.BlockSpec((4, gd), lambda g, t: (0, g)),
            pl.BlockSpec((4, SSD_STATE), lambda g, t: (0, nx + g)),
            pl.BlockSpec((4, SSD_STATE), lambda g, t: (0, nx + SSD_GROUPS + g)),
            pl.BlockSpec((1, gd), lambda g, t: (0, g)),
            pl.BlockSpec((1, SSD_STATE), lambda g, t: (0, nx + g)),
            pl.BlockSpec((1, SSD_STATE), lambda g, t: (0, nx + SSD_GROUPS + g)),
            pl.BlockSpec((None, SSD_REP, 128), lambda g, t: (g, 0, 0)),
            pl.BlockSpec((None, SSD_REP, 128), lambda g, t: (g, 0, 0)),
            pl.BlockSpec((1, gd), lambda g, t: (0, g)),
            pl.BlockSpec((1, gd), lambda g, t: (0, g)),
        ],
        out_specs=pl.BlockSpec((TB, gd), lambda g, t: (t, g)),
        out_shape=jax.ShapeDtypeStruct((rows, SSD_D_INNER), BF16),
        scratch_shapes=[
            pltpu.VMEM((SSD_STATE, gd), F32),
            pltpu.VMEM((8, gd), F32),
            pltpu.VMEM((8, SSD_STATE), F32),
            pltpu.VMEM((8, SSD_STATE), F32),
            pltpu.VMEM((TB, TB), BF16),
            pltpu.VMEM((TB, TB), BF16),
        ],
        compiler_params=_cparams(("arbitrary", "arbitrary")),
    )(xbc, xbc, xbc, z, dt_rows, conv_w, conv_w, conv_w, conv_b, conv_b, conv_b, alog_b, dtb_b, d_exp, norm_w)


def _merge_kernel(ydn_ref, yssd_ref, gd_ref, gs_ref, wdn_ref, wssd_ref, o_ref):
    a = jnp.dot(ydn_ref[...], wdn_ref[...], preferred_element_type=F32)
    b = jnp.dot(yssd_ref[...], wssd_ref[...], preferred_element_type=F32)
    o = _sigmoid(gd_ref[...].astype(F32)) * a + _sigmoid(gs_ref[...].astype(F32)) * b
    o_ref[...] = o.astype(o_ref.dtype)


def _merge(y_dn, y_ssd, gates, w_dn, w_ssd, tn=512):
    ntile = (y_dn.shape[0] - FRONT) // TB
    off = FRONT // TB
    ng = D_MODEL // tn
    return pl.pallas_call(
        _merge_kernel,
        grid=(ng, ntile),
        in_specs=[
            pl.BlockSpec((TB, DN_VAL_DIM), lambda j, i: (i + off, 0)),
            pl.BlockSpec((TB, SSD_D_INNER), lambda j, i: (i + off, 0)),
            pl.BlockSpec((TB, tn), lambda j, i: (i + off, j)),
            pl.BlockSpec((TB, tn), lambda j, i: (i + off, ng + j)),
            pl.BlockSpec((DN_VAL_DIM, tn), lambda j, i: (0, j)),
            pl.BlockSpec((SSD_D_INNER, tn), lambda j, i: (0, j)),
        ],
        out_specs=pl.BlockSpec((TB, tn), lambda j, i: (i, j)),
        out_shape=jax.ShapeDtypeStruct((ntile * TB, D_MODEL), BF16),
        compiler_params=_cparams(("arbitrary", "arbitrary")),
    )(y_dn, y_ssd, gates, gates, w_dn, w_ssd)


def _out_router_kernel(m_ref, wo_ref, x_ref, nw_ref, wr_ref, br_ref,
                       h2_ref, hn_ref, idx_ref, p_ref, rank_ref, cnt_ref, carry_ref, us_ref):
    i = pl.program_id(0)
    rt = x_ref.shape[0]

    @pl.when(i == 0)
    def _():
        carry_ref[...] = jnp.zeros_like(carry_ref)
        jj = lax.broadcasted_iota(jnp.int32, (rt, rt), 0)
        ii = lax.broadcasted_iota(jnp.int32, (rt, rt), 1)
        us_ref[...] = jnp.where(jj < ii, 1.0, 0.0).astype(BF16)

    h2 = x_ref[...] + jnp.dot(m_ref[...], wo_ref[...], preferred_element_type=F32)
    h2_ref[...] = h2
    hn = h2 * lax.rsqrt(jnp.mean(h2 * h2, axis=-1, keepdims=True) + EPS) * nw_ref[...]
    hn_ref[...] = hn

    lg = _dot3(wr_ref[...], hn, nt=True) + br_ref[...]
    eio = lax.broadcasted_iota(jnp.int32, (N_EXPERTS, rt), 0)
    vals, hits, idxs = [], [], []
    cur = lg
    for _ in range(TOP_K):
        m = jnp.max(cur, axis=0, keepdims=True)
        ik = jnp.min(jnp.where(cur == m, eio, N_EXPERTS), axis=0, keepdims=True)
        hit = eio == ik
        vals.append(m)
        idxs.append(ik)
        hits.append(hit)
        cur = jnp.where(hit, -jnp.inf, cur)
    ex = [jnp.exp(vk - vals[0]) for vk in vals]
    den = ex[0] + ex[1] + ex[2] + ex[3]
    idx_ref[...] = jnp.concatenate(idxs, axis=0)
    p_ref[...] = jnp.concatenate([e / den for e in ex], axis=0)

    sel = jnp.zeros((N_EXPERTS, rt), F32)
    for hit in hits:
        sel = sel + jnp.where(hit, 1.0, 0.0)
    before = jnp.dot(sel.astype(BF16), us_ref[...], preferred_element_type=F32) \
        + _tile_lanes(carry_ref[...], rt // 128)
    ranks = [jnp.sum(jnp.where(hit, before, 0.0), axis=0, keepdims=True) for hit in hits]
    rank_ref[...] = jnp.concatenate(ranks, axis=0).astype(jnp.int32)
    carry = carry_ref[...] + jnp.dot(sel.astype(BF16), jnp.ones((rt, 128), BF16), preferred_element_type=F32)
    carry_ref[...] = carry
    cnt_ref[...] = carry


def _out_router(merged, w_out, x, nw, w_router_t, b_router_c):
    n_tok = x.shape[0]
    rt = ROUTER_RT
    tok_spec = pl.BlockSpec((TOP_K, rt), lambda i: (0, i))
    return pl.pallas_call(
        _out_router_kernel,
        grid=(n_tok // rt,),
        in_specs=[
            pl.BlockSpec((rt, D_MODEL), lambda i: (i, 0)),
            pl.BlockSpec((D_MODEL, D_MODEL), lambda i: (0, 0), pipeline_mode=pl.Buffered(1)),
            pl.BlockSpec((rt, D_MODEL), lambda i: (i, 0)),
            pl.BlockSpec((1, D_MODEL), lambda i: (0, 0)),
            pl.BlockSpec((N_EXPERTS, D_MODEL), lambda i: (0, 0)),
            pl.BlockSpec((N_EXPERTS, 1), lambda i: (0, 0)),
        ],
        out_specs=[
            pl.BlockSpec((rt, D_MODEL), lambda i: (i, 0)),
            pl.BlockSpec((rt, D_MODEL), lambda i: (i, 0)),
            tok_spec, tok_spec, tok_spec,
            pl.BlockSpec((N_EXPERTS, 128), lambda i: (0, 0)),
        ],
        out_shape=[
            jax.ShapeDtypeStruct((n_tok, D_MODEL), F32),
            jax.ShapeDtypeStruct((n_tok, D_MODEL), F32),
            jax.ShapeDtypeStruct((TOP_K, n_tok), jnp.int32),
            jax.ShapeDtypeStruct((TOP_K, n_tok), F32),
            jax.ShapeDtypeStruct((TOP_K, n_tok), jnp.int32),
            jax.ShapeDtypeStruct((N_EXPERTS, 128), F32),
        ],
        scratch_shapes=[pltpu.VMEM((N_EXPERTS, 128), F32), pltpu.VMEM((rt, rt), BF16)],
        compiler_params=_cparams(("arbitrary",)),
    )(merged, w_out, x, nw, w_router_t, b_router_c)


def _gather_kernel(nused_ref, tok_ref, src_ref, o_ref, buf_ref, sem):
    b = pl.program_id(0)
    nu = nused_ref[0]

    def issue(blk, slot):
        base = blk * MOE_BM

        def body(g, carry):
            for s in range(8):
                tok = tok_ref[base + g * 8 + s]
                pltpu.make_async_copy(src_ref.at[tok >> 3, pl.ds(tok & 7, 1)],
                                      buf_ref.at[slot, g, pl.ds(s, 1)], sem.at[slot]).start()
            return carry

        lax.fori_loop(0, MOE_BM // 8, body, 0)

    @pl.when(b == 0)
    def _():
        issue(0, 0)

    @pl.when(b + 1 < nu)
    def _():
        issue(b + 1, (b + 1) % 2)

    @pl.when(b < nu)
    def _():
        slot = b % 2
        pltpu.make_async_copy(src_ref.at[pl.ds(0, MOE_BM // 8)], buf_ref.at[slot], sem.at[slot]).wait()
        o_ref[...] = buf_ref[slot].reshape(MOE_BM, D_MODEL).astype(o_ref.dtype)

    @pl.when(b >= nu)
    def _():
        o_ref[...] = jnp.zeros_like(o_ref)


def _gather_rows(n_used, row_tok, src):
    nb = row_tok.shape[0] // MOE_BM
    return pl.pallas_call(
        _gather_kernel,
        grid_spec=pltpu.PrefetchScalarGridSpec(
            num_scalar_prefetch=2,
            grid=(nb,),
            in_specs=[pl.BlockSpec(memory_space=pl.ANY)],
            out_specs=pl.BlockSpec((MOE_BM, D_MODEL), lambda b, nu, tok: (b, 0)),
            scratch_shapes=[pltpu.VMEM((2, MOE_BM // 8, 8, D_MODEL), F32), pltpu.SemaphoreType.DMA((2,))],
        ),
        out_shape=jax.ShapeDtypeStruct((nb * MOE_BM, D_MODEL), BF16),
        compiler_params=_cparams(("arbitrary",)),
    )(n_used, row_tok, src.reshape(src.shape[0] // 8, 8, D_MODEL))


def _expert_weights(e_ref, nxt_ref, nused_ref, run_ref, w_hbm, w_stage, w_bf16, wsem):
    j = pl.program_id(0)
    b = pl.program_id(1)
    tn = w_bf16[0].shape[1]
    e = e_ref[b]
    first = jnp.logical_and(b < nused_ref[0],
                            jnp.logical_or(b == 0, e != e_ref[jnp.maximum(b - 1, 0)]))

    def fetch(ee, jj, slot):
        cols = pl.ds(pl.multiple_of(jj * tn, tn), tn)
        return [pltpu.make_async_copy(w.at[ee, pl.ds(0, D_MODEL), cols], st.at[slot], wsem.at[slot])
                for w, st in zip(w_hbm, w_stage)]

    @pl.when(jnp.logical_and(j == 0, b == 0))
    def _():
        run_ref[0] = 0
        for cp in fetch(e, 0, 0):
            cp.start()

    @pl.when(first)
    def _():
        slot = run_ref[0] % 2
        for cp in fetch(e, j, slot):
            cp.wait()
        for st, wb in zip(w_stage, w_bf16):
            wb[...] = st[slot].astype(BF16)
        nxt = nxt_ref[e]
        same_tile = nxt >= 0

        @pl.when(jnp.logical_or(same_tile, j + 1 < pl.num_programs(0)))
        def _():
            for cp in fetch(jnp.where(same_tile, nxt, e_ref[0]), jnp.where(same_tile, j, j + 1), 1 - slot):
                cp.start()

        run_ref[0] = run_ref[0] + 1


def _expert_up_kernel(e_ref, nxt_ref, nused_ref, x_ref, wg_hbm, wu_hbm, bg_ref, bu_ref, o_ref,
                      wg_st, wu_st, wgb_ref, wub_ref, run_ref, wsem):
    b = pl.program_id(1)
    _expert_weights(e_ref, nxt_ref, nused_ref, run_ref, (wg_hbm, wu_hbm), (wg_st, wu_st),
                    (wgb_ref, wub_ref), wsem)

    @pl.when(b < nused_ref[0])
    def _():
        x = x_ref[...]
        gt = jnp.dot(x, wgb_ref[...], preferred_element_type=F32) + bg_ref[...]
        up = jnp.dot(x, wub_ref[...], preferred_element_type=F32) + bu_ref[...]
        gt = jnp.minimum(gt, SWIGLU_LIMIT)
        up = jnp.clip(up, -SWIGLU_LIMIT, SWIGLU_LIMIT)
        o_ref[...] = (gt * _sigmoid(SWIGLU_ALPHA * gt) * (up + 1.0)).astype(o_ref.dtype)

    @pl.when(b >= nused_ref[0])
    def _():
        o_ref[...] = jnp.zeros_like(o_ref)


def _expert_up(block_e, next_e, n_used, xs, w_gate, w_up, b_gate, b_up):
    nb = xs.shape[0] // MOE_BM
    tn = MOE_TN_UP
    bspec = pl.BlockSpec((None, 1, tn), lambda j, b, be, *_: (be[b], 0, j))
    return pl.pallas_call(
        _expert_up_kernel,
        grid_spec=pltpu.PrefetchScalarGridSpec(
            num_scalar_prefetch=3,
            grid=(D_MODEL // tn, nb),
            in_specs=[pl.BlockSpec((MOE_BM, D_MODEL), lambda j, b, *_: (b, 0)),
                      pl.BlockSpec(memory_space=pl.ANY), pl.BlockSpec(memory_space=pl.ANY), bspec, bspec],
            out_specs=pl.BlockSpec((MOE_BM, tn), lambda j, b, *_: (b, j)),
            scratch_shapes=[pltpu.VMEM((2, D_MODEL, tn), F32), pltpu.VMEM((2, D_MODEL, tn), F32),
                            pltpu.VMEM((D_MODEL, tn), BF16), pltpu.VMEM((D_MODEL, tn), BF16),
                            pltpu.SMEM((1,), jnp.int32), pltpu.SemaphoreType.DMA((2,))],
        ),
        out_shape=jax.ShapeDtypeStruct(xs.shape, BF16),
        compiler_params=_cparams(("arbitrary", "arbitrary")),
    )(block_e, next_e, n_used, xs, w_gate, w_up, b_gate, b_up)


def _expert_down_kernel(e_ref, nxt_ref, nused_ref, a_ref, wd_hbm, bd_ref, o_ref, wd_st, wdb_ref, run_ref, wsem):
    b = pl.program_id(1)
    _expert_weights(e_ref, nxt_ref, nused_ref, run_ref, (wd_hbm,), (wd_st,), (wdb_ref,), wsem)

    @pl.when(b < nused_ref[0])
    def _():
        o_ref[...] = jnp.dot(a_ref[...], wdb_ref[...], preferred_element_type=F32) + bd_ref[...]

    @pl.when(b >= nused_ref[0])
    def _():
        o_ref[...] = jnp.zeros_like(o_ref)


def _expert_down(block_e, next_e, n_used, act, w_down, b_down):
    nb = act.shape[0] // MOE_BM
    tn = MOE_TN_DOWN
    return pl.pallas_call(
        _expert_down_kernel,
        grid_spec=pltpu.PrefetchScalarGridSpec(
            num_scalar_prefetch=3,
            grid=(D_MODEL // tn, nb),
            in_specs=[pl.BlockSpec((MOE_BM, D_MODEL), lambda j, b, *_: (b, 0)),
                      pl.BlockSpec(memory_space=pl.ANY),
                      pl.BlockSpec((None, 1, tn), lambda j, b, be, *_: (be[b], 0, j))],
            out_specs=pl.BlockSpec((MOE_BM, tn), lambda j, b, *_: (b, j)),
            scratch_shapes=[pltpu.VMEM((2, D_MODEL, tn), F32), pltpu.VMEM((D_MODEL, tn), BF16),
                            pltpu.SMEM((1,), jnp.int32), pltpu.SemaphoreType.DMA((2,))],
        ),
        out_shape=jax.ShapeDtypeStruct(act.shape, F32),
        compiler_params=_cparams(("arbitrary", "arbitrary")),
    )(block_e, next_e, n_used, act, w_down, b_down)


def _combine_kernel(dest_ref, h2_ref, p_ref, fw_ref, y_ref, o_ref, buf_ref, sem):
    i = pl.program_id(0)
    tt = COMB_TT

    def issue(tile, slot):
        base = tile * (TOP_K * tt)

        def body(r, carry):
            for k in range(TOP_K):
                pltpu.make_async_copy(y_ref.at[pl.ds(dest_ref[base + k * tt + r], 1)],
                                      buf_ref.at[slot, k, pl.ds(r, 1)], sem.at[slot]).start()
            return carry

        lax.fori_loop(0, tt, body, 0, unroll=4)

    @pl.when(i == 0)
    def _():
        issue(0, 0)

    @pl.when(i + 1 < pl.num_programs(0))
    def _():
        issue(i + 1, (i + 1) % 2)

    slot = i % 2
    for k in range(TOP_K):
        pltpu.make_async_copy(y_ref.at[pl.ds(0, tt)], buf_ref.at[slot, k], sem.at[slot]).wait()
    p = p_ref[...]
    acc = h2_ref[...]
    for k in range(TOP_K):
        acc = acc + p[:, k:k + 1] * buf_ref[slot, k]
    o_ref[...] = acc * lax.rsqrt(jnp.mean(acc * acc, axis=-1, keepdims=True) + EPS) * fw_ref[...]


def _combine(dest, h2, p_cols, final_w, y_rows):
    n_tok = h2.shape[0]
    tt = COMB_TT
    return pl.pallas_call(
        _combine_kernel,
        grid_spec=pltpu.PrefetchScalarGridSpec(
            num_scalar_prefetch=1,
            grid=(n_tok // tt,),
            in_specs=[
                pl.BlockSpec((tt, D_MODEL), lambda i, d: (i, 0)),
                pl.BlockSpec((tt, TOP_K), lambda i, d: (i, 0)),
                pl.BlockSpec((1, D_MODEL), lambda i, d: (0, 0)),
                pl.BlockSpec(memory_space=pl.ANY),
            ],
            out_specs=pl.BlockSpec((tt, D_MODEL), lambda i, d: (i, 0)),
            scratch_shapes=[pltpu.VMEM((2, TOP_K, tt, D_MODEL), F32), pltpu.SemaphoreType.DMA((2,))],
        ),
        out_shape=jax.ShapeDtypeStruct((n_tok, D_MODEL), F32),
        compiler_params=_cparams(("arbitrary",)),
    )(dest, h2, p_cols, final_w, y_rows)


def _mixer(x2, meta, norm_w, w_in, dn_conv_w, dn_a_log, dn_dt_bias, dn_norm_w,
           ssd_conv_w, ssd_conv_b, ssd_a_log, ssd_dt_bias, ssd_d, ssd_norm_w, w_proj_dn, w_proj_ssd):
    o_qkv = 0
    o_dnz = o_qkv + DN_CONV_DIM
    o_b = o_dnz + DN_VAL_DIM
    o_a = o_b + DN_V_HEADS
    o_sz = o_a + DN_V_HEADS
    o_xbc = o_sz + SSD_D_INNER
    o_dt = o_xbc + SSD_CONV_DIM
    o_gate = o_dt + SSD_HEADS
    d_in = o_gate + 2 * D_MODEL
    w_small = jnp.concatenate([w_in[:, o_b:o_sz], w_in[:, o_dt:o_gate]], axis=1)

    hn, small = _norm_small(x2, meta, norm_w.reshape(1, D_MODEL), w_small)
    qkv = _matmul(hn, w_in[:, o_qkv:o_dnz].astype(BF16), BF16)
    dn_z = _matmul(hn, w_in[:, o_dnz:o_b].astype(BF16), BF16)
    ssd_z = _matmul(hn, w_in[:, o_sz:o_xbc].astype(BF16), BF16)
    xbc = _matmul(hn, w_in[:, o_xbc:o_dt].astype(BF16), BF16)
    gates = _matmul(hn, w_in[:, o_gate:d_in].astype(BF16), BF16)

    rows = hn.shape[0]
    sm_t = small.T
    dn_steps, dn_vps = DN_QK_HEADS // DN_QPS, 2 * DN_QPS
    b_rows = sm_t[0:DN_V_HEADS].reshape(dn_steps, dn_vps, rows)
    a_rows = sm_t[DN_V_HEADS:2 * DN_V_HEADS].reshape(dn_steps, dn_vps, rows)
    dt_rows = sm_t[2 * DN_V_HEADS:].reshape(SSD_GROUPS, SSD_REP, rows)

    bc = lambda p, g, r: jnp.broadcast_to(p.astype(F32).reshape(g, r, 1), (g, r, 128))
    y_dn = _gated_deltanet(qkv, dn_z, b_rows, a_rows, dn_conv_w,
                           bc(dn_a_log, dn_steps, dn_vps), bc(dn_dt_bias, dn_steps, dn_vps),
                           dn_norm_w.reshape(1, DN_HEAD_DIM))
    y_ssd = _mamba2_ssd(xbc, ssd_z, dt_rows, ssd_conv_w, ssd_conv_b.reshape(1, SSD_CONV_DIM),
                        bc(ssd_a_log, SSD_GROUPS, SSD_REP), bc(ssd_dt_bias, SSD_GROUPS, SSD_REP),
                        jnp.repeat(ssd_d, SSD_HEAD_DIM).reshape(1, SSD_D_INNER),
                        ssd_norm_w.reshape(1, SSD_D_INNER))
    return _merge(y_dn, y_ssd, gates, w_proj_dn.astype(BF16), w_proj_ssd.astype(BF16))


def _moe_and_final(merged, x2, w_out, norm_ffn_w, w_router, b_router, w_gate, b_gate, w_up, b_up,
                   w_down, b_down, final_norm_w):
    n_tok = x2.shape[0]
    h2, hn2, idx, probs, rank, cnt = _out_router(
        merged, w_out.astype(BF16), x2, norm_ffn_w.reshape(1, D_MODEL), w_router.T,
        b_router.reshape(N_EXPERTS, 1))

    counts = cnt[:, 0].astype(jnp.int32)
    padded = (counts + MOE_BM - 1) // MOE_BM * MOE_BM
    pad_end = jnp.cumsum(padded)
    pad_start = pad_end - padded
    hit = idx[:, :, None] == jnp.arange(N_EXPERTS, dtype=jnp.int32)
    dest = jnp.sum(jnp.where(hit, pad_start.astype(jnp.int32), 0), axis=-1) + rank
    nb = -(-(n_tok * TOP_K + N_EXPERTS * (MOE_BM - 1)) // MOE_BM)
    tok_ids = jnp.broadcast_to(jnp.arange(n_tok, dtype=jnp.int32)[None], (TOP_K, n_tok))
    row_tok = jnp.zeros((nb * MOE_BM,), jnp.int32).at[dest.reshape(-1)].set(tok_ids.reshape(-1))
    n_used = (pad_end[-1:] // MOE_BM).astype(jnp.int32)
    block_e = jnp.minimum(jnp.sum(pad_end[None, :] <= (jnp.arange(nb, dtype=jnp.int32) * MOE_BM)[:, None], axis=1),
                          N_EXPERTS - 1).astype(jnp.int32)
    ids = jnp.arange(N_EXPERTS, dtype=jnp.int32)
    later = (ids[None, :] > ids[:, None]) & (padded[None, :] > 0)
    next_e = jnp.min(jnp.where(later, ids[None, :], N_EXPERTS), axis=1)
    next_e = jnp.where(next_e < N_EXPERTS, next_e, -1).astype(jnp.int32)

    xs = _gather_rows(n_used, row_tok, hn2)
    act = _expert_up(block_e, next_e, n_used, xs, w_gate, w_up,
                     b_gate.reshape(N_EXPERTS, 1, D_MODEL), b_up.reshape(N_EXPERTS, 1, D_MODEL))
    y_rows = _expert_down(block_e, next_e, n_used, act, w_down, b_down.reshape(N_EXPERTS, 1, D_MODEL))
    dest_t = dest.reshape(TOP_K, n_tok // COMB_TT, COMB_TT).transpose(1, 0, 2).reshape(-1)
    return _combine(dest_t, h2, probs.T, final_norm_w.reshape(1, D_MODEL), y_rows)


def kernel(x, meta_tokens, norm_mix_w, w_in, dn_conv_w, dn_a_log, dn_dt_bias, dn_norm_w, ssd_conv_w, ssd_conv_b, ssd_a_log, ssd_dt_bias, ssd_d, ssd_norm_w, w_proj_dn, w_proj_ssd, w_out, norm_ffn_w, w_router, b_router, w_gate, b_gate, w_up, b_up, w_down, b_down, final_norm_w):
    bsz, seq, d = x.shape
    assert bsz == 1 and d == D_MODEL and seq % TB == 0
    assert norm_mix_w.shape[0] == 1, "single-layer block"
    x2 = x.reshape(seq, d)
    merged = _mixer(x2, meta_tokens.astype(x.dtype), norm_mix_w[0], w_in[0], dn_conv_w[0], dn_a_log[0], dn_dt_bias[0], dn_norm_w[0],
                    ssd_conv_w[0], ssd_conv_b[0], ssd_a_log[0], ssd_dt_bias[0], ssd_d[0], ssd_norm_w[0],
                    w_proj_dn[0], w_proj_ssd[0])
    out = _moe_and_final(merged, x2, w_out[0], norm_ffn_w[0], w_router[0], b_router[0], w_gate[0], b_gate[0],
                         w_up[0], b_up[0], w_down[0], b_down[0], final_norm_w)
    return out.reshape(bsz, seq, d)
```

```python
import functools

import jax
import jax.numpy as jnp
from jax import lax
from jax.experimental import pallas as pl
from jax.experimental.pallas import tpu as pltpu

F32 = jnp.float32
BF16 = jnp.bfloat16

D_MODEL = 2048
N_META = 16
CHUNK = 64
EPS = 1e-6

DN_QK_HEADS = 16
DN_V_HEADS = 32
DN_HEAD_DIM = 128
DN_KEY_DIM = DN_QK_HEADS * DN_HEAD_DIM
DN_VAL_DIM = DN_V_HEADS * DN_HEAD_DIM
DN_CONV_DIM = 2 * DN_KEY_DIM + DN_VAL_DIM

SSD_D_INNER = 2 * D_MODEL
SSD_HEAD_DIM = 64
SSD_HEADS = SSD_D_INNER // SSD_HEAD_DIM
SSD_GROUPS = 8
SSD_REP = SSD_HEADS // SSD_GROUPS
SSD_STATE = 128
SSD_GS = SSD_GROUPS * SSD_STATE
SSD_CONV_DIM = SSD_D_INNER + 2 * SSD_GS
SSD_GROUP_DIM = SSD_D_INNER // SSD_GROUPS

N_EXPERTS = 32
TOP_K = 4
SWIGLU_LIMIT = 7.0
SWIGLU_ALPHA = 1.702

TB = 512
FRONT = TB
CPB = TB // CHUNK
MOE_BM = 256
MOE_TN_UP = 1024
MOE_TN_DOWN = 2048
COMB_TT = 256
ROUTER_RT = 512
DN_GROUP = 2
DN_QPS = 4
VMEM_LIMIT = 56 * 1024 * 1024


def _cparams(sem):
    return pltpu.CompilerParams(dimension_semantics=sem, vmem_limit_bytes=VMEM_LIMIT)


def _bdot(a, b):
    return jnp.dot(a.astype(BF16), b.astype(BF16), preferred_element_type=F32)


def _bdot_nt(a, b):
    return lax.dot_general(a.astype(BF16), b.astype(BF16), (((1,), (1,)), ((), ())),
                           preferred_element_type=F32)


def _bdot_tn(a, b):
    return lax.dot_general(a.astype(BF16), b.astype(BF16), (((0,), (0,)), ((), ())),
                           preferred_element_type=F32)


def _split3(x):
    x1 = x.astype(BF16)
    r1 = x - x1.astype(F32)
    x2 = r1.astype(BF16)
    x3 = (r1 - x2.astype(F32)).astype(BF16)
    return x1, x2, x3


def _dot_sel(x, m):
    x1, x2, x3 = _split3(x)
    d = lambda a: jnp.dot(a, m, preferred_element_type=F32)
    return d(x1) + d(x2) + d(x3)


def _dot3(a, b, nt=False):
    a1 = a.astype(BF16)
    a2 = (a - a1.astype(F32)).astype(BF16)
    b1 = b.astype(BF16)
    b2 = (b - b1.astype(F32)).astype(BF16)
    if nt:
        d = lambda p, q: lax.dot_general(p, q, (((1,), (1,)), ((), ())), preferred_element_type=F32)
    else:
        d = lambda p, q: jnp.dot(p, q, preferred_element_type=F32)
    return d(a1, b1) + d(a2, b1) + d(a1, b2)


def _sigmoid(x):
    return 1.0 / (1.0 + jnp.exp(-x))


def _silu(x):
    return x * _sigmoid(x)


def _softplus(x):
    return jnp.maximum(x, 0.0) + jnp.log(1.0 + jnp.exp(-jnp.abs(x)))


def _tile_lanes(p, n):
    return jnp.concatenate([p] * n, axis=1)


def _norm_small_kernel(x_ref, meta_ref, nw_ref, ws_ref, hn_ref, sm_ref):
    i = pl.program_id(0)

    def emit(h):
        y = h * lax.rsqrt(jnp.mean(h * h, axis=-1, keepdims=True) + EPS) * nw_ref[...]
        hn_ref[...] = y.astype(BF16)
        sm_ref[...] = _dot3(y, ws_ref[...])

    @pl.when(i == 0)
    def _():
        emit(jnp.concatenate([jnp.zeros((FRONT - N_META, D_MODEL), F32), meta_ref[...]], axis=0))

    @pl.when(i > 0)
    def _():
        emit(x_ref[...])


def _norm_small(x2, meta, nw, w_small):
    rows = FRONT + x2.shape[0]
    return pl.pallas_call(
        _norm_small_kernel,
        grid=(rows // TB,),
        in_specs=[pl.BlockSpec((TB, D_MODEL), lambda i: (jnp.maximum(i - 1, 0), 0)),
                  pl.BlockSpec((N_META, D_MODEL), lambda i: (0, 0)),
                  pl.BlockSpec((1, D_MODEL), lambda i: (0, 0)),
                  pl.BlockSpec((D_MODEL, 128), lambda i: (0, 0))],
        out_specs=[pl.BlockSpec((TB, D_MODEL), lambda i: (i, 0)),
                   pl.BlockSpec((TB, 128), lambda i: (i, 0))],
        out_shape=[jax.ShapeDtypeStruct((rows, D_MODEL), BF16),
                   jax.ShapeDtypeStruct((rows, 128), F32)],
        compiler_params=_cparams(("arbitrary",)),
    )(x2, meta, nw, w_small)


def _mm_kernel(x_ref, w_ref, o_ref):
    o_ref[...] = jnp.dot(x_ref[...], w_ref[...], preferred_element_type=F32).astype(o_ref.dtype)


def _matmul(x, w, out_dtype, tm=1024, tn=2048):
    m, k = x.shape
    n = w.shape[1]
    return pl.pallas_call(
        _mm_kernel,
        grid=(n // tn, pl.cdiv(m, tm)),
        in_specs=[pl.BlockSpec((tm, k), lambda j, i: (i, 0)),
                  pl.BlockSpec((k, tn), lambda j, i: (0, j))],
        out_specs=pl.BlockSpec((tm, tn), lambda j, i: (i, j)),
        out_shape=jax.ShapeDtypeStruct((m, n), out_dtype),
        compiler_params=_cparams(("arbitrary", "arbitrary")),
    )(x, w)


def _conv_silu(raw_ref, tail_ref, w_ref, b_ref=None):
    x = raw_ref[...].astype(F32)
    rows = x.shape[0]
    ext = jnp.concatenate([tail_ref[...], x], axis=0)
    w = w_ref[...]
    y = x * w[3:4]
    for j in (1, 2, 3):
        y = y + pltpu.roll(ext, j, 0)[8:8 + rows] * w[3 - j:4 - j]
    tail_ref[...] = x[rows - 8:rows]
    if b_ref is not None:
        y = y + b_ref[...]
    return _silu(y)


def _fill_chunk_matrices(u_ref, e_ref):
    jj = lax.broadcasted_iota(jnp.int32, (TB, TB), 0)
    ii = lax.broadcasted_iota(jnp.int32, (TB, TB), 1)
    same = (jj // CHUNK) == (ii // CHUNK)
    u_ref[...] = jnp.where(same & (jj <= ii), 1.0, 0.0).astype(BF16)
    e_ref[...] = jnp.where(same & (jj % CHUNK == CHUNK - 1), 1.0, 0.0).astype(BF16)


def _col_bcast(row, width):
    return jnp.broadcast_to(row, (width, TB)).T


def _dn_kernel(q_ref, k_ref, v_ref, z_ref, b_ref, a_ref, wq_ref, wk_ref, wv_ref, alog_ref, dtb_ref, nw_ref,
               o_ref, s_ref, tq_ref, tk_ref, tv_ref, u_ref, e_ref):
    t = pl.program_id(1)

    @pl.when(t == 0)
    def _():
        s_ref[...] = jnp.zeros_like(s_ref)
        tq_ref[...] = jnp.zeros_like(tq_ref)
        tk_ref[...] = jnp.zeros_like(tk_ref)
        tv_ref[...] = jnp.zeros_like(tv_ref)
        _fill_chunk_matrices(u_ref, e_ref)

    d = DN_HEAD_DIM
    nq = q_ref.shape[1] // d
    q_all = _conv_silu(q_ref, tq_ref, wq_ref)
    k_all = _conv_silu(k_ref, tk_ref, wk_ref)
    v = _conv_silu(v_ref, tv_ref, wv_ref)
    qs, ks = [], []
    for i in range(nq):
        q = q_all[:, i * d:(i + 1) * d]
        k = k_all[:, i * d:(i + 1) * d]
        qs.append(q * (lax.rsqrt(jnp.sum(q * q, axis=-1, keepdims=True) + EPS) * (d ** -0.5)))
        ks.append(k * lax.rsqrt(jnp.sum(k * k, axis=-1, keepdims=True) + EPS))

    beta = _sigmoid(b_ref[...])
    g = -jnp.exp(_tile_lanes(alog_ref[...], TB // 128)) * _softplus(a_ref[...] + _tile_lanes(dtb_ref[...], TB // 128))
    gc = _dot_sel(g, u_ref[...])
    gl = _dot_sel(gc, e_ref[...])
    eg = jnp.exp(gc)
    ek = jnp.exp(gl - gc)

    gw = DN_GROUP * CHUNK
    lg = CHUNK.bit_length() - 1
    ii = lax.broadcasted_iota(jnp.int32, (gw, gw), 0)
    jj = lax.broadcasted_iota(jnp.int32, (gw, gw), 1)
    same = (ii >> lg) == (jj >> lg)
    causal = same & (ii >= jj)
    strict = same & (ii > jj)
    eye = jnp.where(ii == jj, 1.0, 0.0)
    masks = [strict & ((ii >> 1) == (jj >> 1))]
    blk = 2
    while blk < CHUNK:
        sh = blk.bit_length() - 1
        masks.append(((ii >> (sh + 1)) == (jj >> (sh + 1))) & (((ii >> sh) & 1) == 1) & (((jj >> sh) & 1) == 0))
        blk *= 2

    heads = range(2 * nq)
    chunks = range(CPB)
    groups = range(TB // gw)
    rows_of = [slice(c * CHUNK, (c + 1) * CHUNK) for c in chunks]
    grows = [slice(s * gw, (s + 1) * gw) for s in groups]
    kk = [[_bdot_nt(ks[i][grows[s]], ks[i][grows[s]]) for s in groups] for i in range(nq)]
    qk = [[_bdot_nt(qs[i][grows[s]], ks[i][grows[s]]) for s in groups] for i in range(nq)]

    eg_cb, rhs_all, qd_all, kd_all = [], [], [], []
    dec, a = {}, {}
    for hh in heads:
        q, k = qs[hh // 2], ks[hh // 2]
        gc_cb = _col_bcast(gc[hh:hh + 1], max(gw, DN_HEAD_DIM))
        beta_cb = _col_bcast(beta[hh:hh + 1], max(gw, DN_HEAD_DIM))
        eg_cb.append(_col_bcast(eg[hh:hh + 1], DN_HEAD_DIM))
        ek_cb = _col_bcast(ek[hh:hh + 1], DN_HEAD_DIM)
        beta_h = beta_cb[:, :DN_HEAD_DIM]
        vh = v[:, hh * DN_HEAD_DIM:(hh + 1) * DN_HEAD_DIM]
        rhs_all.append(jnp.concatenate([vh * beta_h, k * (beta_h * eg_cb[hh])], axis=1))
        qd_all.append(q * eg_cb[hh])
        kd_all.append(k * ek_cb)
        for s in groups:
            diff = gc_cb[grows[s], :gw] - gc[hh:hh + 1, s * gw:(s + 1) * gw]
            dec[hh, s] = jnp.where(causal, jnp.exp(jnp.where(causal, diff, 0.0)), 0.0)
            a[hh, s] = jnp.where(strict, beta_cb[grows[s], :gw] * kk[hh // 2][s] * dec[hh, s], 0.0)
    inst = [(hh, s) for s in groups for hh in heads]

    tinv = {i: eye - jnp.where(masks[0], a[i], 0.0) for i in inst}
    for m in masks[1:]:
        x = {i: _bdot(jnp.where(m, a[i], 0.0), tinv[i]) for i in inst}
        tinv = {i: tinv[i] - _bdot(tinv[i], x[i]) for i in inst}
    uw_g = {(hh, s): _bdot(tinv[hh, s], rhs_all[hh][grows[s]]) for hh, s in inst}
    qo_g = {(hh, s): _bdot(qk[hh // 2][s] * dec[hh, s], uw_g[hh, s]) for hh, s in inst}
    in_group = [slice((c % DN_GROUP) * CHUNK, (c % DN_GROUP + 1) * CHUNK) for c in chunks]
    uw = {(hh, c): uw_g[hh, c // DN_GROUP][in_group[c]] for c in chunks for hh in heads}
    qo = {(hh, c): qo_g[hh, c // DN_GROUP][in_group[c]] for c in chunks for hh in heads}
    nk = {(hh, c): _bdot_tn(uw[hh, c], kd_all[hh][rows_of[c]]) for c in chunks for hh in heads}

    state_t = [s_ref[hh] for hh in heads]
    before = {}
    for c in chunks:
        for hh in heads:
            before[hh, c] = state_t[hh]
            g_end = eg_cb[hh][(c + 1) * CHUNK - 1:(c + 1) * CHUNK, :]
            state_t[hh] = (state_t[hh] * g_end - _bdot(state_t[hh], nk[hh, c][DN_HEAD_DIM:])
                           + nk[hh, c][:DN_HEAD_DIM])
    nw = nw_ref[...]
    for hh in heads:
        s_ref[hh] = state_t[hh]
        outs = [_bdot_nt(qd_all[hh][rows_of[c]] - qo[hh, c][:, DN_HEAD_DIM:], before[hh, c])
                + qo[hh, c][:, :DN_HEAD_DIM] for c in chunks]
        o = jnp.concatenate(outs, axis=0)
        zh = z_ref[:, hh * DN_HEAD_DIM:(hh + 1) * DN_HEAD_DIM].astype(F32)
        o = o * lax.rsqrt(jnp.mean(o * o, axis=-1, keepdims=True) + EPS) * nw * _silu(zh)
        o_ref[:, hh * DN_HEAD_DIM:(hh + 1) * DN_HEAD_DIM] = o.astype(o_ref.dtype)


def _gated_deltanet(qkv, z, b_rows, a_rows, conv_w, alog_b, dtb_b, norm_w):
    rows = qkv.shape[0]
    nstep = DN_QK_HEADS // DN_QPS
    qw = DN_QPS * DN_HEAD_DIM
    return pl.pallas_call(
        _dn_kernel,
        grid=(nstep, rows // TB),
        in_specs=[
            pl.BlockSpec((TB, qw), lambda j, t: (t, j)),
            pl.BlockSpec((TB, qw), lambda j, t: (t, nstep + j)),
            pl.BlockSpec((TB, 2 * qw), lambda j, t: (t, nstep + j)),
            pl.BlockSpec((TB, 2 * qw), lambda j, t: (t, j)),
            pl.BlockSpec((None, 2 * DN_QPS, TB), lambda j, t: (j, 0, t)),
            pl.BlockSpec((None, 2 * DN_QPS, TB), lambda j, t: (j, 0, t)),
            pl.BlockSpec((4, qw), lambda j, t: (0, j)),
            pl.BlockSpec((4, qw), lambda j, t: (0, nstep + j)),
            pl.BlockSpec((4, 2 * qw), lambda j, t: (0, nstep + j)),
            pl.BlockSpec((None, 2 * DN_QPS, 128), lambda j, t: (j, 0, 0)),
            pl.BlockSpec((None, 2 * DN_QPS, 128), lambda j, t: (j, 0, 0)),
            pl.BlockSpec((1, DN_HEAD_DIM), lambda j, t: (0, 0)),
        ],
        out_specs=pl.BlockSpec((TB, 2 * qw), lambda j, t: (t, j)),
        out_shape=jax.ShapeDtypeStruct((rows, DN_VAL_DIM), BF16),
        scratch_shapes=[
            pltpu.VMEM((2 * DN_QPS, DN_HEAD_DIM, DN_HEAD_DIM), F32),
            pltpu.VMEM((8, qw), F32),
            pltpu.VMEM((8, qw), F32),
            pltpu.VMEM((8, 2 * qw), F32),
            pltpu.VMEM((TB, TB), BF16),
            pltpu.VMEM((TB, TB), BF16),
        ],
        compiler_params=_cparams(("arbitrary", "arbitrary")),
    )(qkv, qkv, qkv, z, b_rows, a_rows, conv_w, conv_w, conv_w, alog_b, dtb_b, norm_w)


def _ssd_kernel(x_ref, b_ref, c_ref, z_ref, dt_ref, wx_ref, wb_ref, wc_ref, bx_ref, bb_ref, bc_ref,
                alog_ref, dtb_ref, dexp_ref, nw_ref, o_ref, h_ref, tx_ref, tb_ref, tc_ref, u_ref, e_ref):
    t = pl.program_id(1)

    @pl.when(t == 0)
    def _():
        h_ref[...] = jnp.zeros_like(h_ref)
        tx_ref[...] = jnp.zeros_like(tx_ref)
        tb_ref[...] = jnp.zeros_like(tb_ref)
        tc_ref[...] = jnp.zeros_like(tc_ref)
        _fill_chunk_matrices(u_ref, e_ref)

    xh = _conv_silu(x_ref, tx_ref, wx_ref, bx_ref)
    bm = _conv_silu(b_ref, tb_ref, wb_ref, bb_ref)
    cm = _conv_silu(c_ref, tc_ref, wc_ref, bc_ref)

    nrep = TB // 128
    dt = _softplus(dt_ref[...] + _tile_lanes(dtb_ref[...], nrep))
    pos = t * TB + lax.broadcasted_iota(jnp.int32, (SSD_REP, TB), 1)
    dt = jnp.where(pos >= FRONT - N_META, dt, 0.0)
    la = -jnp.exp(_tile_lanes(alog_ref[...], nrep)) * dt
    ac = _dot_sel(la, u_ref[...])
    al = _dot_sel(ac, e_ref[...])
    dtek = dt * jnp.exp(al - ac)

    def col_bcast_heads(rowsv):
        stacked = jnp.concatenate(
            [jnp.broadcast_to(rowsv[r:r + 1], (SSD_HEAD_DIM, TB)) for r in range(SSD_REP)], axis=0)
        return stacked.T

    ac_cb = col_bcast_heads(ac)
    ea_cb = jnp.exp(ac_cb)
    xd = xh * col_bcast_heads(dtek)

    ii = lax.broadcasted_iota(jnp.int32, (CHUNK, 2 * CHUNK), 0)
    jj = lax.broadcasted_iota(jnp.int32, (CHUNK, 2 * CHUNK), 1)
    causal2 = ii >= (jj & (CHUNK - 1))
    li = lax.broadcasted_iota(jnp.int32, (CHUNK, 2 * CHUNK), 1)
    lo_half = li < CHUNK

    state = h_ref[...]
    outs = []
    for c in range(CPB):
        rows = slice(c * CHUNK, (c + 1) * CHUNK)
        cc = cm[rows]
        bc = bm[rows]
        cb = _bdot_nt(cc, bc)
        cb2 = jnp.concatenate([cb, cb], axis=1)
        y_off = _bdot(cc, state) * ea_cb[rows]
        yd = []
        for m in range(SSD_REP // 2):
            lanes = slice(m * 128, (m + 1) * 128)
            rowv = jnp.concatenate([ac[2 * m:2 * m + 1, rows], ac[2 * m + 1:2 * m + 2, rows]], axis=1)
            dtrow = jnp.concatenate([dt[2 * m:2 * m + 1, rows], dt[2 * m + 1:2 * m + 2, rows]], axis=1)
            diff = ac_cb[rows, lanes] - rowv
            seg = jnp.where(causal2, jnp.exp(jnp.where(causal2, diff, 0.0)), 0.0)
            mm = cb2 * seg * dtrow
            x2 = xh[rows, lanes]
            rhs = jnp.concatenate([jnp.where(lo_half, x2, 0.0), jnp.where(lo_half, 0.0, x2)], axis=0)
            yd.append(_bdot(mm, rhs))
        outs.append(jnp.concatenate(yd, axis=1) + y_off)
        ea_last = ea_cb[(c + 1) * CHUNK - 1:(c + 1) * CHUNK, :]
        state = state * ea_last + _bdot_tn(bc, xd[rows])
    h_ref[...] = state

    y = jnp.concatenate(outs, axis=0) + dexp_ref[...] * xh
    y = y * _silu(z_ref[...].astype(F32))
    y = y * lax.rsqrt(jnp.mean(y * y, axis=-1, keepdims=True) + EPS) * nw_ref[...]
    o_ref[...] = y.astype(o_ref.dtype)


def _mamba2_ssd(xbc, z, dt_rows, conv_w, conv_b, alog_b, dtb_b, d_exp, norm_w):
    rows = xbc.shape[0]
    gd = SSD_GROUP_DIM
    nx = SSD_D_INNER // SSD_STATE
    return pl.pallas_call(
        _ssd_kernel,
        grid=(SSD_GROUPS, rows // TB),
        in_specs=[
            pl.BlockSpec((TB, gd), lambda g, t: (t, g)),
            pl.BlockSpec((TB, SSD_STATE), lambda g, t: (t, nx + g)),
            pl.BlockSpec((TB, SSD_STATE), lambda g, t: (t, nx + SSD_GROUPS + g)),
            pl.BlockSpec((TB, gd), lambda g, t: (t, g)),
            pl.BlockSpec((None, SSD_REP, TB), lambda g, t: (g, 0, t)),
            pl.BlockSpec((4, gd), lambda g, t: (0, g)),
            pl.BlockSpec((4, SSD_STATE), lambda g, t: (0, nx + g)),
            pl.BlockSpec((4, SSD_STATE), lambda g, t: (0, nx + SSD_GROUPS + g)),
            pl.BlockSpec((1, gd), lambda g, t: (0, g)),
            pl.BlockSpec((1, SSD_STATE), lambda g, t: (0, nx + g)),
            pl.BlockSpec((1, SSD_STATE), lambda g, t: (0, nx + SSD_GROUPS + g)),
            pl.BlockSpec((None, SSD_REP, 128), lambda g, t: (g, 0, 0)),
            pl.BlockSpec((None, SSD_REP, 128), lambda g, t: (g, 0, 0)),
            pl.BlockSpec((1, gd), lambda g, t: (0, g)),
            pl.BlockSpec((1, gd), lambda g, t: (0, g)),
        ],
        out_specs=pl.BlockSpec((TB, gd), lambda g, t: (t, g)),
        out_shape=jax.ShapeDtypeStruct((rows, SSD_D_INNER), BF16),
        scratch_shapes=[
            pltpu.VMEM((SSD_STATE, gd), F32),
            pltpu.VMEM((8, gd), F32),
            pltpu.VMEM((8, SSD_STATE), F32),
            pltpu.VMEM((8, SSD_STATE), F32),
            pltpu.VMEM((TB, TB), BF16),
            pltpu.VMEM((TB, TB), BF16),
        ],
        compiler_params=_cparams(("arbitrary", "arbitrary")),
    )(xbc, xbc, xbc, z, dt_rows, conv_w, conv_w, conv_w, conv_b, conv_b, conv_b, alog_b, dtb_b, d_exp, norm_w)


def _merge_kernel(ydn_ref, yssd_ref, gd_ref, gs_ref, wdn_ref, wssd_ref, o_ref):
    a = jnp.dot(ydn_ref[...], wdn_ref[...], preferred_element_type=F32)
    b = jnp.dot(yssd_ref[...], wssd_ref[...], preferred_element_type=F32)
    o = _sigmoid(gd_ref[...].astype(F32)) * a + _sigmoid(gs_ref[...].astype(F32)) * b
    o_ref[...] = o.astype(o_ref.dtype)


def _merge(y_dn, y_ssd, gates, w_dn, w_ssd, tn=512):
    ntile = (y_dn.shape[0] - FRONT) // TB
    off = FRONT // TB
    ng = D_MODEL // tn
    return pl.pallas_call(
        _merge_kernel,
        grid=(ng, ntile),
        in_specs=[
            pl.BlockSpec((TB, DN_VAL_DIM), lambda j, i: (i + off, 0)),
            pl.BlockSpec((TB, SSD_D_INNER), lambda j, i: (i + off, 0)),
            pl.BlockSpec((TB, tn), lambda j, i: (i + off, j)),
            pl.BlockSpec((TB, tn), lambda j, i: (i + off, ng + j)),
            pl.BlockSpec((DN_VAL_DIM, tn), lambda j, i: (0, j)),
            pl.BlockSpec((SSD_D_INNER, tn), lambda j, i: (0, j)),
        ],
        out_specs=pl.BlockSpec((TB, tn), lambda j, i: (i, j)),
        out_shape=jax.ShapeDtypeStruct((ntile * TB, D_MODEL), BF16),
        compiler_params=_cparams(("arbitrary", "arbitrary")),
    )(y_dn, y_ssd, gates, gates, w_dn, w_ssd)


def _out_router_kernel(m_ref, wo_ref, x_ref, nw_ref, wr_ref, br_ref,
                       h2_ref, hn_ref, idx_ref, p_ref, rank_ref, cnt_ref, carry_ref, us_ref):
    i = pl.program_id(0)
    rt = x_ref.shape[0]

    @pl.when(i == 0)
    def _():
        carry_ref[...] = jnp.zeros_like(carry_ref)
        jj = lax.broadcasted_iota(jnp.int32, (rt, rt), 0)
        ii = lax.broadcasted_iota(jnp.int32, (rt, rt), 1)
        us_ref[...] = jnp.where(jj < ii, 1.0, 0.0).astype(BF16)

    h2 = x_ref[...] + jnp.dot(m_ref[...], wo_ref[...], preferred_element_type=F32)
    h2_ref[...] = h2
    hn = h2 * lax.rsqrt(jnp.mean(h2 * h2, axis=-1, keepdims=True) + EPS) * nw_ref[...]
    hn_ref[...] = hn

    lg = _dot3(wr_ref[...], hn, nt=True) + br_ref[...]
    eio = lax.broadcasted_iota(jnp.int32, (N_EXPERTS, rt), 0)
    vals, hits, idxs = [], [], []
    cur = lg
    for _ in range(TOP_K):
        m = jnp.max(cur, axis=0, keepdims=True)
        ik = jnp.min(jnp.where(cur == m, eio, N_EXPERTS), axis=0, keepdims=True)
        hit = eio == ik
        vals.append(m)
        idxs.append(ik)
        hits.append(hit)
        cur = jnp.where(hit, -jnp.inf, cur)
    ex = [jnp.exp(vk - vals[0]) for vk in vals]
    den = ex[0] + ex[1] + ex[2] + ex[3]
    idx_ref[...] = jnp.concatenate(idxs, axis=0)
    p_ref[...] = jnp.concatenate([e / den for e in ex], axis=0)

    sel = jnp.zeros((N_EXPERTS, rt), F32)
    for hit in hits:
        sel = sel + jnp.where(hit, 1.0, 0.0)
    before = jnp.dot(sel.astype(BF16), us_ref[...], preferred_element_type=F32) \
        + _tile_lanes(carry_ref[...], rt // 128)
    ranks = [jnp.sum(jnp.where(hit, before, 0.0), axis=0, keepdims=True) for hit in hits]
    rank_ref[...] = jnp.concatenate(ranks, axis=0).astype(jnp.int32)
    carry = carry_ref[...] + jnp.dot(sel.astype(BF16), jnp.ones((rt, 128), BF16), preferred_element_type=F32)
    carry_ref[...] = carry
    cnt_ref[...] = carry


def _out_router(merged, w_out, x, nw, w_router_t, b_router_c):
    n_tok = x.shape[0]
    rt = ROUTER_RT
    tok_spec = pl.BlockSpec((TOP_K, rt), lambda i: (0, i))
    return pl.pallas_call(
        _out_router_kernel,
        grid=(n_tok // rt,),
        in_specs=[
            pl.BlockSpec((rt, D_MODEL), lambda i: (i, 0)),
            pl.BlockSpec((D_MODEL, D_MODEL), lambda i: (0, 0), pipeline_mode=pl.Buffered(1)),
            pl.BlockSpec((rt, D_MODEL), lambda i: (i, 0)),
            pl.BlockSpec((1, D_MODEL), lambda i: (0, 0)),
            pl.BlockSpec((N_EXPERTS, D_MODEL), lambda i: (0, 0)),
            pl.BlockSpec((N_EXPERTS, 1), lambda i: (0, 0)),
        ],
        out_specs=[
            pl.BlockSpec((rt, D_MODEL), lambda i: (i, 0)),
            pl.BlockSpec((rt, D_MODEL), lambda i: (i, 0)),
            tok_spec, tok_spec, tok_spec,
            pl.BlockSpec((N_EXPERTS, 128), lambda i: (0, 0)),
        ],
        out_shape=[
            jax.ShapeDtypeStruct((n_tok, D_MODEL), F32),
            jax.ShapeDtypeStruct((n_tok, D_MODEL), F32),
            jax.ShapeDtypeStruct((TOP_K, n_tok), jnp.int32),
            jax.ShapeDtypeStruct((TOP_K, n_tok), F32),
            jax.ShapeDtypeStruct((TOP_K, n_tok), jnp.int32),
            jax.ShapeDtypeStruct((N_EXPERTS, 128), F32),
        ],
        scratch_shapes=[pltpu.VMEM((N_EXPERTS, 128), F32), pltpu.VMEM((rt, rt), BF16)],
        compiler_params=_cparams(("arbitrary",)),
    )(merged, w_out, x, nw, w_router_t, b_router_c)


def _gather_kernel(nused_ref, tok_ref, src_ref, o_ref, buf_ref, sem):
    b = pl.program_id(0)
    nu = nused_ref[0]

    def issue(blk, slot):
        base = blk * MOE_BM

        def body(g, carry):
            for s in range(8):
                tok = tok_ref[base + g * 8 + s]
                pltpu.make_async_copy(src_ref.at[tok >> 3, pl.ds(tok & 7, 1)],
                                      buf_ref.at[slot, g, pl.ds(s, 1)], sem.at[slot]).start()
            return carry

        lax.fori_loop(0, MOE_BM // 8, body, 0)

    @pl.when(b == 0)
    def _():
        issue(0, 0)

    @pl.when(b + 1 < nu)
    def _():
        issue(b + 1, (b + 1) % 2)

    @pl.when(b < nu)
    def _():
        slot = b % 2
        pltpu.make_async_copy(src_ref.at[pl.ds(0, MOE_BM // 8)], buf_ref.at[slot], sem.at[slot]).wait()
        o_ref[...] = buf_ref[slot].reshape(MOE_BM, D_MODEL).astype(o_ref.dtype)

    @pl.when(b >= nu)
    def _():
        o_ref[...] = jnp.zeros_like(o_ref)


def _gather_rows(n_used, row_tok, src):
    nb = row_tok.shape[0] // MOE_BM
    return pl.pallas_call(
        _gather_kernel,
        grid_spec=pltpu.PrefetchScalarGridSpec(
            num_scalar_prefetch=2,
            grid=(nb,),
            in_specs=[pl.BlockSpec(memory_space=pl.ANY)],
            out_specs=pl.BlockSpec((MOE_BM, D_MODEL), lambda b, nu, tok: (b, 0)),
            scratch_shapes=[pltpu.VMEM((2, MOE_BM // 8, 8, D_MODEL), F32), pltpu.SemaphoreType.DMA((2,))],
        ),
        out_shape=jax.ShapeDtypeStruct((nb * MOE_BM, D_MODEL), BF16),
        compiler_params=_cparams(("arbitrary",)),
    )(n_used, row_tok, src.reshape(src.shape[0] // 8, 8, D_MODEL))


def _expert_weights(e_ref, nxt_ref, nused_ref, run_ref, w_hbm, w_stage, w_bf16, wsem):
    j = pl.program_id(0)
    b = pl.program_id(1)
    tn = w_bf16[0].shape[1]
    e = e_ref[b]
    first = jnp.logical_and(b < nused_ref[0],
                            jnp.logical_or(b == 0, e != e_ref[jnp.maximum(b - 1, 0)]))

    def fetch(ee, jj, slot):
        cols = pl.ds(pl.multiple_of(jj * tn, tn), tn)
        return [pltpu.make_async_copy(w.at[ee, pl.ds(0, D_MODEL), cols], st.at[slot], wsem.at[slot])
                for w, st in zip(w_hbm, w_stage)]

    @pl.when(jnp.logical_and(j == 0, b == 0))
    def _():
        run_ref[0] = 0
        for cp in fetch(e, 0, 0):
            cp.start()

    @pl.when(first)
    def _():
        slot = run_ref[0] % 2
        for cp in fetch(e, j, slot):
            cp.wait()
        for st, wb in zip(w_stage, w_bf16):
            wb[...] = st[slot].astype(BF16)
        nxt = nxt_ref[e]
        same_tile = nxt >= 0

        @pl.when(jnp.logical_or(same_tile, j + 1 < pl.num_programs(0)))
        def _():
            for cp in fetch(jnp.where(same_tile, nxt, e_ref[0]), jnp.where(same_tile, j, j + 1), 1 - slot):
                cp.start()

        run_ref[0] = run_ref[0] + 1


def _expert_up_kernel(e_ref, nxt_ref, nused_ref, x_ref, wg_hbm, wu_hbm, bg_ref, bu_ref, o_ref,
                      wg_st, wu_st, wgb_ref, wub_ref, run_ref, wsem):
    b = pl.program_id(1)
    _expert_weights(e_ref, nxt_ref, nused_ref, run_ref, (wg_hbm, wu_hbm), (wg_st, wu_st),
                    (wgb_ref, wub_ref), wsem)

    @pl.when(b < nused_ref[0])
    def _():
        x = x_ref[...]
        gt = jnp.dot(x, wgb_ref[...], preferred_element_type=F32) + bg_ref[...]
        up = jnp.dot(x, wub_ref[...], preferred_element_type=F32) + bu_ref[...]
        gt = jnp.minimum(gt, SWIGLU_LIMIT)
        up = jnp.clip(up, -SWIGLU_LIMIT, SWIGLU_LIMIT)
        o_ref[...] = (gt * _sigmoid(SWIGLU_ALPHA * gt) * (up + 1.0)).astype(o_ref.dtype)

    @pl.when(b >= nused_ref[0])
    def _():
        o_ref[...] = jnp.zeros_like(o_ref)


def _expert_up(block_e, next_e, n_used, xs, w_gate, w_up, b_gate, b_up):
    nb = xs.shape[0] // MOE_BM
    tn = MOE_TN_UP
    bspec = pl.BlockSpec((None, 1, tn), lambda j, b, be, *_: (be[b], 0, j))
    return pl.pallas_call(
        _expert_up_kernel,
        grid_spec=pltpu.PrefetchScalarGridSpec(
            num_scalar_prefetch=3,
            grid=(D_MODEL // tn, nb),
            in_specs=[pl.BlockSpec((MOE_BM, D_MODEL), lambda j, b, *_: (b, 0)),
                      pl.BlockSpec(memory_space=pl.ANY), pl.BlockSpec(memory_space=pl.ANY), bspec, bspec],
            out_specs=pl.BlockSpec((MOE_BM, tn), lambda j, b, *_: (b, j)),
            scratch_shapes=[pltpu.VMEM((2, D_MODEL, tn), F32), pltpu.VMEM((2, D_MODEL, tn), F32),
                            pltpu.VMEM((D_MODEL, tn), BF16), pltpu.VMEM((D_MODEL, tn), BF16),
                            pltpu.SMEM((1,), jnp.int32), pltpu.SemaphoreType.DMA((2,))],
        ),
        out_shape=jax.ShapeDtypeStruct(xs.shape, BF16),
        compiler_params=_cparams(("arbitrary", "arbitrary")),
    )(block_e, next_e, n_used, xs, w_gate, w_up, b_gate, b_up)


def _expert_down_kernel(e_ref, nxt_ref, nused_ref, a_ref, wd_hbm, bd_ref, o_ref, wd_st, wdb_ref, run_ref, wsem):
    b = pl.program_id(1)
    _expert_weights(e_ref, nxt_ref, nused_ref, run_ref, (wd_hbm,), (wd_st,), (wdb_ref,), wsem)

    @pl.when(b < nused_ref[0])
    def _():
        o_ref[...] = jnp.dot(a_ref[...], wdb_ref[...], preferred_element_type=F32) + bd_ref[...]

    @pl.when(b >= nused_ref[0])
    def _():
        o_ref[...] = jnp.zeros_like(o_ref)


def _expert_down(block_e, next_e, n_used, act, w_down, b_down):
    nb = act.shape[0] // MOE_BM
    tn = MOE_TN_DOWN
    return pl.pallas_call(
        _expert_down_kernel,
        grid_spec=pltpu.PrefetchScalarGridSpec(
            num_scalar_prefetch=3,
            grid=(D_MODEL // tn, nb),
            in_specs=[pl.BlockSpec((MOE_BM, D_MODEL), lambda j, b, *_: (b, 0)),
                      pl.BlockSpec(memory_space=pl.ANY),
                      pl.BlockSpec((None, 1, tn), lambda j, b, be, *_: (be[b], 0, j))],
            out_specs=pl.BlockSpec((MOE_BM, tn), lambda j, b, *_: (b, j)),
            scratch_shapes=[pltpu.VMEM((2, D_MODEL, tn), F32), pltpu.VMEM((D_MODEL, tn), BF16),
                            pltpu.SMEM((1,), jnp.int32), pltpu.SemaphoreType.DMA((2,))],
        ),
        out_shape=jax.ShapeDtypeStruct(act.shape, F32),
        compiler_params=_cparams(("arbitrary", "arbitrary")),
    )(block_e, next_e, n_used, act, w_down, b_down)


def _combine_kernel(dest_ref, h2_ref, p_ref, fw_ref, y_ref, o_ref, buf_ref, sem):
    i = pl.program_id(0)
    tt = COMB_TT

    def issue(tile, slot):
        base = tile * (TOP_K * tt)

        def body(g, carry):
            for s in range(8):
                for k in range(TOP_K):
                    row = dest_ref[base + k * tt + g * 8 + s]
                    pltpu.make_async_copy(y_ref.at[row >> 3, pl.ds(row & 7, 1)],
                                          buf_ref.at[slot, k, g, pl.ds(s, 1)], sem.at[slot]).start()
            return carry

        lax.fori_loop(0, tt // 8, body, 0)

    @pl.when(i == 0)
    def _():
        issue(0, 0)

    @pl.when(i + 1 < pl.num_programs(0))
    def _():
        issue(i + 1, (i + 1) % 2)

    slot = i % 2
    for k in range(TOP_K):
        pltpu.make_async_copy(y_ref.at[pl.ds(0, tt // 8)], buf_ref.at[slot, k], sem.at[slot]).wait()
    p = p_ref[...]
    acc = h2_ref[...]
    for k in range(TOP_K):
        acc = acc + p[:, k:k + 1] * buf_ref[slot, k].reshape(tt, D_MODEL)
    o_ref[...] = acc * lax.rsqrt(jnp.mean(acc * acc, axis=-1, keepdims=True) + EPS) * fw_ref[...]


def _combine(dest, h2, p_cols, final_w, y_rows):
    n_tok = h2.shape[0]
    tt = COMB_TT
    return pl.pallas_call(
        _combine_kernel,
        grid_spec=pltpu.PrefetchScalarGridSpec(
            num_scalar_prefetch=1,
            grid=(n_tok // tt,),
            in_specs=[
                pl.BlockSpec((tt, D_MODEL), lambda i, d: (i, 0)),
                pl.BlockSpec((tt, TOP_K), lambda i, d: (i, 0)),
                pl.BlockSpec((1, D_MODEL), lambda i, d: (0, 0)),
                pl.BlockSpec(memory_space=pl.ANY),
            ],
            out_specs=pl.BlockSpec((tt, D_MODEL), lambda i, d: (i, 0)),
            scratch_shapes=[pltpu.VMEM((2, TOP_K, tt // 8, 8, D_MODEL), F32), pltpu.SemaphoreType.DMA((2,))],
        ),
        out_shape=jax.ShapeDtypeStruct((n_tok, D_MODEL), F32),
        compiler_params=_cparams(("arbitrary",)),
    )(dest, h2, p_cols, final_w, y_rows.reshape(y_rows.shape[0] // 8, 8, D_MODEL))


def _mixer(x2, meta, norm_w, w_in, dn_conv_w, dn_a_log, dn_dt_bias, dn_norm_w,
           ssd_conv_w, ssd_conv_b, ssd_a_log, ssd_dt_bias, ssd_d, ssd_norm_w, w_proj_dn, w_proj_ssd):
    o_qkv = 0
    o_dnz = o_qkv + DN_CONV_DIM
    o_b = o_dnz + DN_VAL_DIM
    o_a = o_b + DN_V_HEADS
    o_sz = o_a + DN_V_HEADS
    o_xbc = o_sz + SSD_D_INNER
    o_dt = o_xbc + SSD_CONV_DIM
    o_gate = o_dt + SSD_HEADS
    d_in = o_gate + 2 * D_MODEL
    w_small = jnp.concatenate([w_in[:, o_b:o_sz], w_in[:, o_dt:o_gate]], axis=1)

    hn, small = _norm_small(x2, meta, norm_w.reshape(1, D_MODEL), w_small)
    qkv = _matmul(hn, w_in[:, o_qkv:o_dnz].astype(BF16), BF16)
    dn_z = _matmul(hn, w_in[:, o_dnz:o_b].astype(BF16), BF16)
    ssd_z = _matmul(hn, w_in[:, o_sz:o_xbc].astype(BF16), BF16)
    xbc = _matmul(hn, w_in[:, o_xbc:o_dt].astype(BF16), BF16)
    gates = _matmul(hn, w_in[:, o_gate:d_in].astype(BF16), BF16)

    rows = hn.shape[0]
    sm_t = small.T
    dn_steps, dn_vps = DN_QK_HEADS // DN_QPS, 2 * DN_QPS
    b_rows = sm_t[0:DN_V_HEADS].reshape(dn_steps, dn_vps, rows)
    a_rows = sm_t[DN_V_HEADS:2 * DN_V_HEADS].reshape(dn_steps, dn_vps, rows)
    dt_rows = sm_t[2 * DN_V_HEADS:].reshape(SSD_GROUPS, SSD_REP, rows)

    bc = lambda p, g, r: jnp.broadcast_to(p.astype(F32).reshape(g, r, 1), (g, r, 128))
    y_dn = _gated_deltanet(qkv, dn_z, b_rows, a_rows, dn_conv_w,
                           bc(dn_a_log, dn_steps, dn_vps), bc(dn_dt_bias, dn_steps, dn_vps),
                           dn_norm_w.reshape(1, DN_HEAD_DIM))
    y_ssd = _mamba2_ssd(xbc, ssd_z, dt_rows, ssd_conv_w, ssd_conv_b.reshape(1, SSD_CONV_DIM),
                        bc(ssd_a_log, SSD_GROUPS, SSD_REP), bc(ssd_dt_bias, SSD_GROUPS, SSD_REP),
                        jnp.repeat(ssd_d, SSD_HEAD_DIM).reshape(1, SSD_D_INNER),
                        ssd_norm_w.reshape(1, SSD_D_INNER))
    return _merge(y_dn, y_ssd, gates, w_proj_dn.astype(BF16), w_proj_ssd.astype(BF16))


def _moe_and_final(merged, x2, w_out, norm_ffn_w, w_router, b_router, w_gate, b_gate, w_up, b_up,
                   w_down, b_down, final_norm_w):
    n_tok = x2.shape[0]
    h2, hn2, idx, probs, rank, cnt = _out_router(
        merged, w_out.astype(BF16), x2, norm_ffn_w.reshape(1, D_MODEL), w_router.T,
        b_router.reshape(N_EXPERTS, 1))

    counts = cnt[:, 0].astype(jnp.int32)
    padded = (counts + MOE_BM - 1) // MOE_BM * MOE_BM
    pad_end = jnp.cumsum(padded)
    pad_start = pad_end - padded
    hit = idx[:, :, None] == jnp.arange(N_EXPERTS, dtype=jnp.int32)
    dest = jnp.sum(jnp.where(hit, pad_start.astype(jnp.int32), 0), axis=-1) + rank
    nb = -(-(n_tok * TOP_K + N_EXPERTS * (MOE_BM - 1)) // MOE_BM)
    tok_ids = jnp.broadcast_to(jnp.arange(n_tok, dtype=jnp.int32)[None], (TOP_K, n_tok))
    row_tok = jnp.zeros((nb * MOE_BM,), jnp.int32).at[dest.reshape(-1)].set(
        tok_ids.reshape(-1), unique_indices=True, mode="promise_in_bounds")
    n_used = (pad_end[-1:] // MOE_BM).astype(jnp.int32)
    block_e = jnp.minimum(jnp.sum(pad_end[None, :] <= (jnp.arange(nb, dtype=jnp.int32) * MOE_BM)[:, None], axis=1),
                          N_EXPERTS - 1).astype(jnp.int32)
    ids = jnp.arange(N_EXPERTS, dtype=jnp.int32)
    later = (ids[None, :] > ids[:, None]) & (padded[None, :] > 0)
    next_e = jnp.min(jnp.where(later, ids[None, :], N_EXPERTS), axis=1)
    next_e = jnp.where(next_e < N_EXPERTS, next_e, -1).astype(jnp.int32)

    xs = _gather_rows(n_used, row_tok, hn2)
    act = _expert_up(block_e, next_e, n_used, xs, w_gate, w_up,
                     b_gate.reshape(N_EXPERTS, 1, D_MODEL), b_up.reshape(N_EXPERTS, 1, D_MODEL))
    y_rows = _expert_down(block_e, next_e, n_used, act, w_down, b_down.reshape(N_EXPERTS, 1, D_MODEL))
    dest_t = dest.reshape(TOP_K, n_tok // COMB_TT, COMB_TT).transpose(1, 0, 2).reshape(-1)
    return _combine(dest_t, h2, probs.T, final_norm_w.reshape(1, D_MODEL), y_rows)


def kernel(x, meta_tokens, norm_mix_w, w_in, dn_conv_w, dn_a_log, dn_dt_bias, dn_norm_w, ssd_conv_w, ssd_conv_b, ssd_a_log, ssd_dt_bias, ssd_d, ssd_norm_w, w_proj_dn, w_proj_ssd, w_out, norm_ffn_w, w_router, b_router, w_gate, b_gate, w_up, b_up, w_down, b_down, final_norm_w):
    bsz, seq, d = x.shape
    assert bsz == 1 and d == D_MODEL and seq % TB == 0
    assert norm_mix_w.shape[0] == 1, "single-layer block"
    x2 = x.reshape(seq, d)
    merged = _mixer(x2, meta_tokens.astype(x.dtype), norm_mix_w[0], w_in[0], dn_conv_w[0], dn_a_log[0], dn_dt_bias[0], dn_norm_w[0],
                    ssd_conv_w[0], ssd_conv_b[0], ssd_a_log[0], ssd_dt_bias[0], ssd_d[0], ssd_norm_w[0],
                    w_proj_dn[0], w_proj_ssd[0])
    out = _moe_and_final(merged, x2, w_out[0], norm_ffn_w[0], w_router[0], b_router[0], w_gate[0], b_gate[0],
                         w_up[0], b_up[0], w_down[0], b_down[0], final_norm_w)
    return out.reshape(bsz, seq, d)
```

```python
import functools

import jax
import jax.numpy as jnp
from jax import lax
from jax.experimental import pallas as pl
from jax.experimental.pallas import tpu as pltpu

F32 = jnp.float32
BF16 = jnp.bfloat16

D_MODEL = 2048
N_META = 16
CHUNK = 64
EPS = 1e-6

DN_QK_HEADS = 16
DN_V_HEADS = 32
DN_HEAD_DIM = 128
DN_KEY_DIM = DN_QK_HEADS * DN_HEAD_DIM
DN_VAL_DIM = DN_V_HEADS * DN_HEAD_DIM
DN_CONV_DIM = 2 * DN_KEY_DIM + DN_VAL_DIM

SSD_D_INNER = 2 * D_MODEL
SSD_HEAD_DIM = 64
SSD_HEADS = SSD_D_INNER // SSD_HEAD_DIM
SSD_GROUPS = 8
SSD_REP = SSD_HEADS // SSD_GROUPS
SSD_STATE = 128
SSD_GS = SSD_GROUPS * SSD_STATE
SSD_CONV_DIM = SSD_D_INNER + 2 * SSD_GS
SSD_GROUP_DIM = SSD_D_INNER // SSD_GROUPS

N_EXPERTS = 32
TOP_K = 4
SWIGLU_LIMIT = 7.0
SWIGLU_ALPHA = 1.702

TB = 512
FRONT = TB
CPB = TB // CHUNK
MOE_BM = 256
MOE_TN_UP = 1024
MOE_TN_DOWN = 2048
COMB_TT = 256
ROUTER_RT = 512
DN_GROUP = 2
DN_QPS = 4
VMEM_LIMIT = 56 * 1024 * 1024


def _cparams(sem):
    return pltpu.CompilerParams(dimension_semantics=sem, vmem_limit_bytes=VMEM_LIMIT)


def _bdot(a, b):
    return jnp.dot(a.astype(BF16), b.astype(BF16), preferred_element_type=F32)


def _bdot_nt(a, b):
    return lax.dot_general(a.astype(BF16), b.astype(BF16), (((1,), (1,)), ((), ())),
                           preferred_element_type=F32)


def _bdot_tn(a, b):
    return lax.dot_general(a.astype(BF16), b.astype(BF16), (((0,), (0,)), ((), ())),
                           preferred_element_type=F32)


def _split3(x):
    x1 = x.astype(BF16)
    r1 = x - x1.astype(F32)
    x2 = r1.astype(BF16)
    x3 = (r1 - x2.astype(F32)).astype(BF16)
    return x1, x2, x3


def _dot_sel(x, m):
    x1, x2, x3 = _split3(x)
    d = lambda a: jnp.dot(a, m, preferred_element_type=F32)
    return d(x1) + d(x2) + d(x3)


def _dot3(a, b, nt=False):
    a1 = a.astype(BF16)
    a2 = (a - a1.astype(F32)).astype(BF16)
    b1 = b.astype(BF16)
    b2 = (b - b1.astype(F32)).astype(BF16)
    if nt:
        d = lambda p, q: lax.dot_general(p, q, (((1,), (1,)), ((), ())), preferred_element_type=F32)
    else:
        d = lambda p, q: jnp.dot(p, q, preferred_element_type=F32)
    return d(a1, b1) + d(a2, b1) + d(a1, b2)


def _sigmoid(x):
    return 1.0 / (1.0 + jnp.exp(-x))


def _silu(x):
    return x * _sigmoid(x)


def _softplus(x):
    return jnp.maximum(x, 0.0) + jnp.log(1.0 + jnp.exp(-jnp.abs(x)))


def _tile_lanes(p, n):
    return jnp.concatenate([p] * n, axis=1)


def _norm_small_kernel(x_ref, meta_ref, nw_ref, ws_ref, hn_ref, sm_ref):
    i = pl.program_id(0)

    def emit(h):
        y = h * lax.rsqrt(jnp.mean(h * h, axis=-1, keepdims=True) + EPS) * nw_ref[...]
        hn_ref[...] = y.astype(BF16)
        sm_ref[...] = _dot3(y, ws_ref[...])

    @pl.when(i == 0)
    def _():
        emit(jnp.concatenate([jnp.zeros((FRONT - N_META, D_MODEL), F32), meta_ref[...]], axis=0))

    @pl.when(i > 0)
    def _():
        emit(x_ref[...])


def _norm_small(x2, meta, nw, w_small):
    rows = FRONT + x2.shape[0]
    return pl.pallas_call(
        _norm_small_kernel,
        grid=(rows // TB,),
        in_specs=[pl.BlockSpec((TB, D_MODEL), lambda i: (jnp.maximum(i - 1, 0), 0)),
                  pl.BlockSpec((N_META, D_MODEL), lambda i: (0, 0)),
                  pl.BlockSpec((1, D_MODEL), lambda i: (0, 0)),
                  pl.BlockSpec((D_MODEL, 128), lambda i: (0, 0))],
        out_specs=[pl.BlockSpec((TB, D_MODEL), lambda i: (i, 0)),
                   pl.BlockSpec((TB, 128), lambda i: (i, 0))],
        out_shape=[jax.ShapeDtypeStruct((rows, D_MODEL), BF16),
                   jax.ShapeDtypeStruct((rows, 128), F32)],
        compiler_params=_cparams(("arbitrary",)),
    )(x2, meta, nw, w_small)


def _mm_kernel(x_ref, w_ref, o_ref):
    o_ref[...] = jnp.dot(x_ref[...], w_ref[...], preferred_element_type=F32).astype(o_ref.dtype)


def _matmul(x, w, out_dtype, tm=1024, tn=2048):
    m, k = x.shape
    n = w.shape[1]
    return pl.pallas_call(
        _mm_kernel,
        grid=(n // tn, pl.cdiv(m, tm)),
        in_specs=[pl.BlockSpec((tm, k), lambda j, i: (i, 0)),
                  pl.BlockSpec((k, tn), lambda j, i: (0, j))],
        out_specs=pl.BlockSpec((tm, tn), lambda j, i: (i, j)),
        out_shape=jax.ShapeDtypeStruct((m, n), out_dtype),
        compiler_params=_cparams(("arbitrary", "arbitrary")),
    )(x, w)


def _conv_silu(raw_ref, tail_ref, w_ref, b_ref=None):
    x = raw_ref[...].astype(F32)
    rows = x.shape[0]
    ext = jnp.concatenate([tail_ref[...], x], axis=0)
    w = w_ref[...]
    y = x * w[3:4]
    for j in (1, 2, 3):
        y = y + pltpu.roll(ext, j, 0)[8:8 + rows] * w[3 - j:4 - j]
    tail_ref[...] = x[rows - 8:rows]
    if b_ref is not None:
        y = y + b_ref[...]
    return _silu(y)


def _fill_chunk_matrices(u_ref, e_ref):
    jj = lax.broadcasted_iota(jnp.int32, (TB, TB), 0)
    ii = lax.broadcasted_iota(jnp.int32, (TB, TB), 1)
    same = (jj // CHUNK) == (ii // CHUNK)
    u_ref[...] = jnp.where(same & (jj <= ii), 1.0, 0.0).astype(BF16)
    e_ref[...] = jnp.where(same & (jj % CHUNK == CHUNK - 1), 1.0, 0.0).astype(BF16)


def _col_bcast(row, width):
    return jnp.broadcast_to(row, (width, TB)).T


def _dn_kernel(q_ref, k_ref, v_ref, z_ref, b_ref, a_ref, wq_ref, wk_ref, wv_ref, alog_ref, dtb_ref, nw_ref,
               o_ref, s_ref, tq_ref, tk_ref, tv_ref, u_ref, e_ref):
    t = pl.program_id(1)

    @pl.when(t == 0)
    def _():
        s_ref[...] = jnp.zeros_like(s_ref)
        tq_ref[...] = jnp.zeros_like(tq_ref)
        tk_ref[...] = jnp.zeros_like(tk_ref)
        tv_ref[...] = jnp.zeros_like(tv_ref)
        _fill_chunk_matrices(u_ref, e_ref)

    d = DN_HEAD_DIM
    nq = q_ref.shape[1] // d
    q_all = _conv_silu(q_ref, tq_ref, wq_ref)
    k_all = _conv_silu(k_ref, tk_ref, wk_ref)
    v = _conv_silu(v_ref, tv_ref, wv_ref)
    qs, ks = [], []
    for i in range(nq):
        q = q_all[:, i * d:(i + 1) * d]
        k = k_all[:, i * d:(i + 1) * d]
        qs.append(q * (lax.rsqrt(jnp.sum(q * q, axis=-1, keepdims=True) + EPS) * (d ** -0.5)))
        ks.append(k * lax.rsqrt(jnp.sum(k * k, axis=-1, keepdims=True) + EPS))

    beta = _sigmoid(b_ref[...])
    g = -jnp.exp(_tile_lanes(alog_ref[...], TB // 128)) * _softplus(a_ref[...] + _tile_lanes(dtb_ref[...], TB // 128))
    gc = _dot_sel(g, u_ref[...])
    gl = _dot_sel(gc, e_ref[...])
    eg = jnp.exp(gc)
    ek = jnp.exp(gl - gc)

    gw = DN_GROUP * CHUNK
    lg = CHUNK.bit_length() - 1
    ii = lax.broadcasted_iota(jnp.int32, (gw, gw), 0)
    jj = lax.broadcasted_iota(jnp.int32, (gw, gw), 1)
    same = (ii >> lg) == (jj >> lg)
    causal = same & (ii >= jj)
    strict = same & (ii > jj)
    eye = jnp.where(ii == jj, 1.0, 0.0)
    masks = [strict & ((ii >> 1) == (jj >> 1))]
    blk = 2
    while blk < CHUNK:
        sh = blk.bit_length() - 1
        masks.append(((ii >> (sh + 1)) == (jj >> (sh + 1))) & (((ii >> sh) & 1) == 1) & (((jj >> sh) & 1) == 0))
        blk *= 2

    heads = range(2 * nq)
    chunks = range(CPB)
    groups = range(TB // gw)
    rows_of = [slice(c * CHUNK, (c + 1) * CHUNK) for c in chunks]
    grows = [slice(s * gw, (s + 1) * gw) for s in groups]
    kk = [[_bdot_nt(ks[i][grows[s]], ks[i][grows[s]]) for s in groups] for i in range(nq)]
    qk = [[_bdot_nt(qs[i][grows[s]], ks[i][grows[s]]) for s in groups] for i in range(nq)]

    eg_cb, rhs_all, qd_all, kd_all = [], [], [], []
    dec, a = {}, {}
    for hh in heads:
        q, k = qs[hh // 2], ks[hh // 2]
        gc_cb = _col_bcast(gc[hh:hh + 1], max(gw, DN_HEAD_DIM))
        beta_cb = _col_bcast(beta[hh:hh + 1], max(gw, DN_HEAD_DIM))
        eg_cb.append(_col_bcast(eg[hh:hh + 1], DN_HEAD_DIM))
        ek_cb = _col_bcast(ek[hh:hh + 1], DN_HEAD_DIM)
        beta_h = beta_cb[:, :DN_HEAD_DIM]
        vh = v[:, hh * DN_HEAD_DIM:(hh + 1) * DN_HEAD_DIM]
        rhs_all.append(jnp.concatenate([vh * beta_h, k * (beta_h * eg_cb[hh])], axis=1))
        qd_all.append(q * eg_cb[hh])
        kd_all.append(k * ek_cb)
        for s in groups:
            diff = gc_cb[grows[s], :gw] - gc[hh:hh + 1, s * gw:(s + 1) * gw]
            dec[hh, s] = jnp.where(causal, jnp.exp(jnp.where(causal, diff, 0.0)), 0.0)
            a[hh, s] = jnp.where(strict, beta_cb[grows[s], :gw] * kk[hh // 2][s] * dec[hh, s], 0.0)
    inst = [(hh, s) for s in groups for hh in heads]

    tinv = {i: eye - jnp.where(masks[0], a[i], 0.0) for i in inst}
    for m in masks[1:]:
        x = {i: _bdot(jnp.where(m, a[i], 0.0), tinv[i]) for i in inst}
        tinv = {i: tinv[i] - _bdot(tinv[i], x[i]) for i in inst}
    uw_g = {(hh, s): _bdot(tinv[hh, s], rhs_all[hh][grows[s]]) for hh, s in inst}
    qo_g = {(hh, s): _bdot(qk[hh // 2][s] * dec[hh, s], uw_g[hh, s]) for hh, s in inst}
    in_group = [slice((c % DN_GROUP) * CHUNK, (c % DN_GROUP + 1) * CHUNK) for c in chunks]
    uw = {(hh, c): uw_g[hh, c // DN_GROUP][in_group[c]] for c in chunks for hh in heads}
    qo = {(hh, c): qo_g[hh, c // DN_GROUP][in_group[c]] for c in chunks for hh in heads}
    nk = {(hh, c): _bdot_tn(uw[hh, c], kd_all[hh][rows_of[c]]) for c in chunks for hh in heads}

    state_t = [s_ref[hh] for hh in heads]
    before = {}
    for c in chunks:
        for hh in heads:
            before[hh, c] = state_t[hh]
            g_end = eg_cb[hh][(c + 1) * CHUNK - 1:(c + 1) * CHUNK, :]
            state_t[hh] = (state_t[hh] * g_end - _bdot(state_t[hh], nk[hh, c][DN_HEAD_DIM:])
                           + nk[hh, c][:DN_HEAD_DIM])
    nw = nw_ref[...]
    for hh in heads:
        s_ref[hh] = state_t[hh]
        outs = [_bdot_nt(qd_all[hh][rows_of[c]] - qo[hh, c][:, DN_HEAD_DIM:], before[hh, c])
                + qo[hh, c][:, :DN_HEAD_DIM] for c in chunks]
        o = jnp.concatenate(outs, axis=0)
        zh = z_ref[:, hh * DN_HEAD_DIM:(hh + 1) * DN_HEAD_DIM].astype(F32)
        o = o * lax.rsqrt(jnp.mean(o * o, axis=-1, keepdims=True) + EPS) * nw * _silu(zh)
        o_ref[:, hh * DN_HEAD_DIM:(hh + 1) * DN_HEAD_DIM] = o.astype(o_ref.dtype)


def _gated_deltanet(qkv, z, b_rows, a_rows, conv_w, alog_b, dtb_b, norm_w):
    rows = qkv.shape[0]
    nstep = DN_QK_HEADS // DN_QPS
    qw = DN_QPS * DN_HEAD_DIM
    return pl.pallas_call(
        _dn_kernel,
        grid=(nstep, rows // TB),
        in_specs=[
            pl.BlockSpec((TB, qw), lambda j, t: (t, j)),
            pl.BlockSpec((TB, qw), lambda j, t: (t, nstep + j)),
            pl.BlockSpec((TB, 2 * qw), lambda j, t: (t, nstep + j)),
            pl.BlockSpec((TB, 2 * qw), lambda j, t: (t, j)),
            pl.BlockSpec((None, 2 * DN_QPS, TB), lambda j, t: (j, 0, t)),
            pl.BlockSpec((None, 2 * DN_QPS, TB), lambda j, t: (j, 0, t)),
            pl.BlockSpec((4, qw), lambda j, t: (0, j)),
            pl.BlockSpec((4, qw), lambda j, t: (0, nstep + j)),
            pl.BlockSpec((4, 2 * qw), lambda j, t: (0, nstep + j)),
            pl.BlockSpec((None, 2 * DN_QPS, 128), lambda j, t: (j, 0, 0)),
            pl.BlockSpec((None, 2 * DN_QPS, 128), lambda j, t: (j, 0, 0)),
            pl.BlockSpec((1, DN_HEAD_DIM), lambda j, t: (0, 0)),
        ],
        out_specs=pl.BlockSpec((TB, 2 * qw), lambda j, t: (t, j)),
        out_shape=jax.ShapeDtypeStruct((rows, DN_VAL_DIM), BF16),
        scratch_shapes=[
            pltpu.VMEM((2 * DN_QPS, DN_HEAD_DIM, DN_HEAD_DIM), F32),
            pltpu.VMEM((8, qw), F32),
            pltpu.VMEM((8, qw), F32),
            pltpu.VMEM((8, 2 * qw), F32),
            pltpu.VMEM((TB, TB), BF16),
            pltpu.VMEM((TB, TB), BF16),
        ],
        compiler_params=_cparams(("arbitrary", "arbitrary")),
    )(qkv, qkv, qkv, z, b_rows, a_rows, conv_w, conv_w, conv_w, alog_b, dtb_b, norm_w)


def _ssd_kernel(x_ref, b_ref, c_ref, z_ref, dt_ref, wx_ref, wb_ref, wc_ref, bx_ref, bb_ref, bc_ref,
                alog_ref, dtb_ref, dexp_ref, nw_ref, o_ref, h_ref, tx_ref, tb_ref, tc_ref, u_ref, e_ref):
    t = pl.program_id(1)

    @pl.when(t == 0)
    def _():
        h_ref[...] = jnp.zeros_like(h_ref)
        tx_ref[...] = jnp.zeros_like(tx_ref)
        tb_ref[...] = jnp.zeros_like(tb_ref)
        tc_ref[...] = jnp.zeros_like(tc_ref)
        _fill_chunk_matrices(u_ref, e_ref)

    xh = _conv_silu(x_ref, tx_ref, wx_ref, bx_ref)
    bm = _conv_silu(b_ref, tb_ref, wb_ref, bb_ref)
    cm = _conv_silu(c_ref, tc_ref, wc_ref, bc_ref)

    nrep = TB // 128
    dt = _softplus(dt_ref[...] + _tile_lanes(dtb_ref[...], nrep))
    pos = t * TB + lax.broadcasted_iota(jnp.int32, (SSD_REP, TB), 1)
    dt = jnp.where(pos >= FRONT - N_META, dt, 0.0)
    la = -jnp.exp(_tile_lanes(alog_ref[...], nrep)) * dt
    ac = _dot_sel(la, u_ref[...])
    al = _dot_sel(ac, e_ref[...])
    dtek = dt * jnp.exp(al - ac)

    def col_bcast_heads(rowsv):
        stacked = jnp.concatenate(
            [jnp.broadcast_to(rowsv[r:r + 1], (SSD_HEAD_DIM, TB)) for r in range(SSD_REP)], axis=0)
        return stacked.T

    ac_cb = col_bcast_heads(ac)
    ea_cb = jnp.exp(ac_cb)
    xd = xh * col_bcast_heads(dtek)

    ii = lax.broadcasted_iota(jnp.int32, (CHUNK, 2 * CHUNK), 0)
    jj = lax.broadcasted_iota(jnp.int32, (CHUNK, 2 * CHUNK), 1)
    causal2 = ii >= (jj & (CHUNK - 1))
    li = lax.broadcasted_iota(jnp.int32, (CHUNK, 2 * CHUNK), 1)
    lo_half = li < CHUNK

    state = h_ref[...]
    outs = []
    for c in range(CPB):
        rows = slice(c * CHUNK, (c + 1) * CHUNK)
        cc = cm[rows]
        bc = bm[rows]
        cb = _bdot_nt(cc, bc)
        cb2 = jnp.concatenate([cb, cb], axis=1)
        y_off = _bdot(cc, state) * ea_cb[rows]
        yd = []
        for m in range(SSD_REP // 2):
            lanes = slice(m * 128, (m + 1) * 128)
            rowv = jnp.concatenate([ac[2 * m:2 * m + 1, rows], ac[2 * m + 1:2 * m + 2, rows]], axis=1)
            dtrow = jnp.concatenate([dt[2 * m:2 * m + 1, rows], dt[2 * m + 1:2 * m + 2, rows]], axis=1)
            diff = ac_cb[rows, lanes] - rowv
            seg = jnp.where(causal2, jnp.exp(jnp.where(causal2, diff, 0.0)), 0.0)
            mm = cb2 * seg * dtrow
            x2 = xh[rows, lanes]
            rhs = jnp.concatenate([jnp.where(lo_half, x2, 0.0), jnp.where(lo_half, 0.0, x2)], axis=0)
            yd.append(_bdot(mm, rhs))
        outs.append(jnp.concatenate(yd, axis=1) + y_off)
        ea_last = ea_cb[(c + 1) * CHUNK - 1:(c + 1) * CHUNK, :]
        state = state * ea_last + _bdot_tn(bc, xd[rows])
    h_ref[...] = state

    y = jnp.concatenate(outs, axis=0) + dexp_ref[...] * xh
    y = y * _silu(z_ref[...].astype(F32))
    y = y * lax.rsqrt(jnp.mean(y * y, axis=-1, keepdims=True) + EPS) * nw_ref[...]
    o_ref[...] = y.astype(o_ref.dtype)


def _mamba2_ssd(xbc, z, dt_rows, conv_w, conv_b, alog_b, dtb_b, d_exp, norm_w):
    rows = xbc.shape[0]
    gd = SSD_GROUP_DIM
    nx = SSD_D_INNER // SSD_STATE
    return pl.pallas_call(
        _ssd_kernel,
        grid=(SSD_GROUPS, rows // TB),
        in_specs=[
            pl.BlockSpec((TB, gd), lambda g, t: (t, g)),
            pl.BlockSpec((TB, SSD_STATE), lambda g, t: (t, nx + g)),
            pl.BlockSpec((TB, SSD_STATE), lambda g, t: (t, nx + SSD_GROUPS + g)),
            pl.BlockSpec((TB, gd), lambda g, t: (t, g)),
            pl.BlockSpec((None, SSD_REP, TB), lambda g, t: (g, 0, t)),
            pl.BlockSpec((4, gd), lambda g, t: (0, g)),
            pl.BlockSpec((4, SSD_STATE), lambda g, t: (0, nx + g)),
            pl.BlockSpec((4, SSD_STATE), lambda g, t: (0, nx + SSD_GROUPS + g)),
            pl.BlockSpec((1, gd), lambda g, t: (0, g)),
            pl.BlockSpec((1, SSD_STATE), lambda g, t: (0, nx + g)),
            pl.BlockSpec((1, SSD_STATE), lambda g, t: (0, nx + SSD_GROUPS + g)),
            pl.BlockSpec((None, SSD_REP, 128), lambda g, t: (g, 0, 0)),
            pl.BlockSpec((None, SSD_REP, 128), lambda g, t: (g, 0, 0)),
            pl.BlockSpec((1, gd), lambda g, t: (0, g)),
            pl.BlockSpec((1, gd), lambda g, t: (0, g)),
        ],
        out_specs=pl.BlockSpec((TB, gd), lambda g, t: (t, g)),
        out_shape=jax.ShapeDtypeStruct((rows, SSD_D_INNER), BF16),
        scratch_shapes=[
            pltpu.VMEM((SSD_STATE, gd), F32),
            pltpu.VMEM((8, gd), F32),
            pltpu.VMEM((8, SSD_STATE), F32),
            pltpu.VMEM((8, SSD_STATE), F32),
            pltpu.VMEM((TB, TB), BF16),
            pltpu.VMEM((TB, TB), BF16),
        ],
        compiler_params=_cparams(("arbitrary", "arbitrary")),
    )(xbc, xbc, xbc, z, dt_rows, conv_w, conv_w, conv_w, conv_b, conv_b, conv_b, alog_b, dtb_b, d_exp, norm_w)


def _merge_kernel(ydn_ref, yssd_ref, gd_ref, gs_ref, wdn_ref, wssd_ref, o_ref):
    a = jnp.dot(ydn_ref[...], wdn_ref[...], preferred_element_type=F32)
    b = jnp.dot(yssd_ref[...], wssd_ref[...], preferred_element_type=F32)
    o = _sigmoid(gd_ref[...].astype(F32)) * a + _sigmoid(gs_ref[...].astype(F32)) * b
    o_ref[...] = o.astype(o_ref.dtype)


def _merge(y_dn, y_ssd, gates, w_dn, w_ssd, tn=512):
    ntile = (y_dn.shape[0] - FRONT) // TB
    off = FRONT // TB
    ng = D_MODEL // tn
    return pl.pallas_call(
        _merge_kernel,
        grid=(ng, ntile),
        in_specs=[
            pl.BlockSpec((TB, DN_VAL_DIM), lambda j, i: (i + off, 0)),
            pl.BlockSpec((TB, SSD_D_INNER), lambda j, i: (i + off, 0)),
            pl.BlockSpec((TB, tn), lambda j, i: (i + off, j)),
            pl.BlockSpec((TB, tn), lambda j, i: (i + off, ng + j)),
            pl.BlockSpec((DN_VAL_DIM, tn), lambda j, i: (0, j)),
            pl.BlockSpec((SSD_D_INNER, tn), lambda j, i: (0, j)),
        ],
        out_specs=pl.BlockSpec((TB, tn), lambda j, i: (i, j)),
        out_shape=jax.ShapeDtypeStruct((ntile * TB, D_MODEL), BF16),
        compiler_params=_cparams(("arbitrary", "arbitrary")),
    )(y_dn, y_ssd, gates, gates, w_dn, w_ssd)


def _out_router_kernel(m_ref, wo_ref, x_ref, nw_ref, wr_ref, br_ref,
                       h2_ref, hn_ref, idx_ref, p_ref, rank_ref, cnt_ref, carry_ref, us_ref):
    i = pl.program_id(0)
    rt = x_ref.shape[0]

    @pl.when(i == 0)
    def _():
        carry_ref[...] = jnp.zeros_like(carry_ref)
        jj = lax.broadcasted_iota(jnp.int32, (rt, rt), 0)
        ii = lax.broadcasted_iota(jnp.int32, (rt, rt), 1)
        us_ref[...] = jnp.where(jj < ii, 1.0, 0.0).astype(BF16)

    h2 = x_ref[...] + jnp.dot(m_ref[...], wo_ref[...], preferred_element_type=F32)
    h2_ref[...] = h2
    hn = h2 * lax.rsqrt(jnp.mean(h2 * h2, axis=-1, keepdims=True) + EPS) * nw_ref[...]
    hn_ref[...] = hn

    lg = _dot3(wr_ref[...], hn, nt=True) + br_ref[...]
    eio = lax.broadcasted_iota(jnp.int32, (N_EXPERTS, rt), 0)
    vals, hits, idxs = [], [], []
    cur = lg
    for _ in range(TOP_K):
        m = jnp.max(cur, axis=0, keepdims=True)
        ik = jnp.min(jnp.where(cur == m, eio, N_EXPERTS), axis=0, keepdims=True)
        hit = eio == ik
        vals.append(m)
        idxs.append(ik)
        hits.append(hit)
        cur = jnp.where(hit, -jnp.inf, cur)
    ex = [jnp.exp(vk - vals[0]) for vk in vals]
    den = ex[0] + ex[1] + ex[2] + ex[3]
    idx_ref[...] = jnp.concatenate(idxs, axis=0)
    p_ref[...] = jnp.concatenate([e / den for e in ex], axis=0)

    sel = jnp.zeros((N_EXPERTS, rt), F32)
    for hit in hits:
        sel = sel + jnp.where(hit, 1.0, 0.0)
    before = jnp.dot(sel.astype(BF16), us_ref[...], preferred_element_type=F32) \
        + _tile_lanes(carry_ref[...], rt // 128)
    ranks = [jnp.sum(jnp.where(hit, before, 0.0), axis=0, keepdims=True) for hit in hits]
    rank_ref[...] = jnp.concatenate(ranks, axis=0).astype(jnp.int32)
    carry = carry_ref[...] + jnp.dot(sel.astype(BF16), jnp.ones((rt, 128), BF16), preferred_element_type=F32)
    carry_ref[...] = carry
    cnt_ref[...] = carry


def _out_router(merged, w_out, x, nw, w_router_t, b_router_c):
    n_tok = x.shape[0]
    rt = ROUTER_RT
    tok_spec = pl.BlockSpec((TOP_K, rt), lambda i: (0, i))
    return pl.pallas_call(
        _out_router_kernel,
        grid=(n_tok // rt,),
        in_specs=[
            pl.BlockSpec((rt, D_MODEL), lambda i: (i, 0)),
            pl.BlockSpec((D_MODEL, D_MODEL), lambda i: (0, 0), pipeline_mode=pl.Buffered(1)),
            pl.BlockSpec((rt, D_MODEL), lambda i: (i, 0)),
            pl.BlockSpec((1, D_MODEL), lambda i: (0, 0)),
            pl.BlockSpec((N_EXPERTS, D_MODEL), lambda i: (0, 0)),
            pl.BlockSpec((N_EXPERTS, 1), lambda i: (0, 0)),
        ],
        out_specs=[
            pl.BlockSpec((rt, D_MODEL), lambda i: (i, 0)),
            pl.BlockSpec((rt, D_MODEL), lambda i: (i, 0)),
            tok_spec, tok_spec, tok_spec,
            pl.BlockSpec((N_EXPERTS, 128), lambda i: (0, 0)),
        ],
        out_shape=[
            jax.ShapeDtypeStruct((n_tok, D_MODEL), F32),
            jax.ShapeDtypeStruct((n_tok, D_MODEL), F32),
            jax.ShapeDtypeStruct((TOP_K, n_tok), jnp.int32),
            jax.ShapeDtypeStruct((TOP_K, n_tok), F32),
            jax.ShapeDtypeStruct((TOP_K, n_tok), jnp.int32),
            jax.ShapeDtypeStruct((N_EXPERTS, 128), F32),
        ],
        scratch_shapes=[pltpu.VMEM((N_EXPERTS, 128), F32), pltpu.VMEM((rt, rt), BF16)],
        compiler_params=_cparams(("arbitrary",)),
    )(merged, w_out, x, nw, w_router_t, b_router_c)


def _dispatch_kernel(dest_ref, ztile_ref, nused_ref, x_ref, o_ref, zbuf, sem, zsem):
    i = pl.program_id(0)
    bt = MOE_BM // 8

    @pl.when(i == 0)
    def _():
        zbuf[...] = jnp.zeros_like(zbuf)

        def zero_block(first_tile):
            cp = pltpu.make_async_copy(zbuf, o_ref.at[pl.ds(first_tile, bt)], zsem)
            cp.start()
            cp.wait()

        for e in range(N_EXPERTS):
            @pl.when(ztile_ref[e] >= 0)
            def _(e=e):
                zero_block(ztile_ref[e])

        def tail(b, carry):
            zero_block(b * bt)
            return carry

        lax.fori_loop(nused_ref[0], o_ref.shape[0] // bt, tail, 0)

    nt = x_ref.shape[0]
    base = i * (TOP_K * nt * 8)

    def body(g, carry):
        for s in range(8):
            for k in range(TOP_K):
                row = dest_ref[base + k * (nt * 8) + g * 8 + s]
                pltpu.make_async_copy(x_ref.at[g, pl.ds(s, 1)], o_ref.at[row >> 3, pl.ds(row & 7, 1)], sem).start()
        return carry

    lax.fori_loop(0, nt, body, 0)
    for k in range(TOP_K):
        pltpu.make_async_copy(x_ref, o_ref.at[pl.ds(0, nt)], sem).wait()


def _dispatch_rows(dest, ztile, n_used, src, nb):
    n_tok = src.shape[0]
    tiles = COMB_TT // 8
    out = pl.pallas_call(
        _dispatch_kernel,
        grid_spec=pltpu.PrefetchScalarGridSpec(
            num_scalar_prefetch=3,
            grid=(n_tok // COMB_TT,),
            in_specs=[pl.BlockSpec((tiles, 8, D_MODEL), lambda i, *_: (i, 0, 0))],
            out_specs=pl.BlockSpec(memory_space=pl.ANY),
            scratch_shapes=[pltpu.VMEM((MOE_BM // 8, 8, D_MODEL), F32),
                            pltpu.SemaphoreType.DMA, pltpu.SemaphoreType.DMA],
        ),
        out_shape=jax.ShapeDtypeStruct((nb * MOE_BM // 8, 8, D_MODEL), F32),
        compiler_params=_cparams(("arbitrary",)),
    )(dest, ztile, n_used, src.reshape(n_tok // 8, 8, D_MODEL))
    return out.reshape(nb * MOE_BM, D_MODEL)


def _expert_weights(e_ref, nxt_ref, nused_ref, run_ref, w_hbm, w_stage, w_bf16, wsem):
    j = pl.program_id(0)
    b = pl.program_id(1)
    tn = w_bf16[0].shape[1]
    e = e_ref[b]
    first = jnp.logical_and(b < nused_ref[0],
                            jnp.logical_or(b == 0, e != e_ref[jnp.maximum(b - 1, 0)]))

    def fetch(ee, jj, slot):
        cols = pl.ds(pl.multiple_of(jj * tn, tn), tn)
        return [pltpu.make_async_copy(w.at[ee, pl.ds(0, D_MODEL), cols], st.at[slot], wsem.at[slot])
                for w, st in zip(w_hbm, w_stage)]

    @pl.when(jnp.logical_and(j == 0, b == 0))
    def _():
        run_ref[0] = 0
        for cp in fetch(e, 0, 0):
            cp.start()

    @pl.when(first)
    def _():
        slot = run_ref[0] % 2
        for cp in fetch(e, j, slot):
            cp.wait()
        for st, wb in zip(w_stage, w_bf16):
            wb[...] = st[slot].astype(BF16)
        nxt = nxt_ref[e]
        same_tile = nxt >= 0

        @pl.when(jnp.logical_or(same_tile, j + 1 < pl.num_programs(0)))
        def _():
            for cp in fetch(jnp.where(same_tile, nxt, e_ref[0]), jnp.where(same_tile, j, j + 1), 1 - slot):
                cp.start()

        run_ref[0] = run_ref[0] + 1


def _expert_up_kernel(e_ref, nxt_ref, nused_ref, x_ref, wg_hbm, wu_hbm, bg_ref, bu_ref, o_ref,
                      wg_st, wu_st, wgb_ref, wub_ref, run_ref, wsem):
    b = pl.program_id(1)
    _expert_weights(e_ref, nxt_ref, nused_ref, run_ref, (wg_hbm, wu_hbm), (wg_st, wu_st),
                    (wgb_ref, wub_ref), wsem)

    @pl.when(b < nused_ref[0])
    def _():
        x = x_ref[...].astype(BF16)
        gt = jnp.dot(x, wgb_ref[...], preferred_element_type=F32) + bg_ref[...]
        up = jnp.dot(x, wub_ref[...], preferred_element_type=F32) + bu_ref[...]
        gt = jnp.minimum(gt, SWIGLU_LIMIT)
        up = jnp.clip(up, -SWIGLU_LIMIT, SWIGLU_LIMIT)
        o_ref[...] = (gt * _sigmoid(SWIGLU_ALPHA * gt) * (up + 1.0)).astype(o_ref.dtype)

    @pl.when(b >= nused_ref[0])
    def _():
        o_ref[...] = jnp.zeros_like(o_ref)


def _expert_up(block_e, next_e, n_used, xs, w_gate, w_up, b_gate, b_up):
    nb = xs.shape[0] // MOE_BM
    tn = MOE_TN_UP
    bspec = pl.BlockSpec((None, 1, tn), lambda j, b, be, *_: (be[b], 0, j))
    return pl.pallas_call(
        _expert_up_kernel,
        grid_spec=pltpu.PrefetchScalarGridSpec(
            num_scalar_prefetch=3,
            grid=(D_MODEL // tn, nb),
            in_specs=[pl.BlockSpec((MOE_BM, D_MODEL), lambda j, b, *_: (b, 0)),
                      pl.BlockSpec(memory_space=pl.ANY), pl.BlockSpec(memory_space=pl.ANY), bspec, bspec],
            out_specs=pl.BlockSpec((MOE_BM, tn), lambda j, b, *_: (b, j)),
            scratch_shapes=[pltpu.VMEM((2, D_MODEL, tn), F32), pltpu.VMEM((2, D_MODEL, tn), F32),
                            pltpu.VMEM((D_MODEL, tn), BF16), pltpu.VMEM((D_MODEL, tn), BF16),
                            pltpu.SMEM((1,), jnp.int32), pltpu.SemaphoreType.DMA((2,))],
        ),
        out_shape=jax.ShapeDtypeStruct(xs.shape, BF16),
        compiler_params=_cparams(("arbitrary", "arbitrary")),
    )(block_e, next_e, n_used, xs, w_gate, w_up, b_gate, b_up)


def _expert_down_kernel(e_ref, nxt_ref, nused_ref, a_ref, wd_hbm, bd_ref, o_ref, wd_st, wdb_ref, run_ref, wsem):
    b = pl.program_id(1)
    _expert_weights(e_ref, nxt_ref, nused_ref, run_ref, (wd_hbm,), (wd_st,), (wdb_ref,), wsem)

    @pl.when(b < nused_ref[0])
    def _():
        o_ref[...] = jnp.dot(a_ref[...], wdb_ref[...], preferred_element_type=F32) + bd_ref[...]

    @pl.when(b >= nused_ref[0])
    def _():
        o_ref[...] = jnp.zeros_like(o_ref)


def _expert_down(block_e, next_e, n_used, act, w_down, b_down):
    nb = act.shape[0] // MOE_BM
    tn = MOE_TN_DOWN
    return pl.pallas_call(
        _expert_down_kernel,
        grid_spec=pltpu.PrefetchScalarGridSpec(
            num_scalar_prefetch=3,
            grid=(D_MODEL // tn, nb),
            in_specs=[pl.BlockSpec((MOE_BM, D_MODEL), lambda j, b, *_: (b, 0)),
                      pl.BlockSpec(memory_space=pl.ANY),
                      pl.BlockSpec((None, 1, tn), lambda j, b, be, *_: (be[b], 0, j))],
            out_specs=pl.BlockSpec((MOE_BM, tn), lambda j, b, *_: (b, j)),
            scratch_shapes=[pltpu.VMEM((2, D_MODEL, tn), F32), pltpu.VMEM((D_MODEL, tn), BF16),
                            pltpu.SMEM((1,), jnp.int32), pltpu.SemaphoreType.DMA((2,))],
        ),
        out_shape=jax.ShapeDtypeStruct(act.shape, F32),
        compiler_params=_cparams(("arbitrary", "arbitrary")),
    )(block_e, next_e, n_used, act, w_down, b_down)


def _combine_kernel(dest_ref, h2_ref, p_ref, fw_ref, y_ref, o_ref, buf_ref, sem):
    i = pl.program_id(0)
    tt = COMB_TT

    def issue(tile, slot):
        base = tile * (TOP_K * tt)

        def body(g, carry):
            for s in range(8):
                for k in range(TOP_K):
                    row = dest_ref[base + k * tt + g * 8 + s]
                    pltpu.make_async_copy(y_ref.at[row >> 3, pl.ds(row & 7, 1)],
                                          buf_ref.at[slot, k, g, pl.ds(s, 1)], sem.at[slot]).start()
            return carry

        lax.fori_loop(0, tt // 8, body, 0)

    @pl.when(i == 0)
    def _():
        issue(0, 0)

    @pl.when(i + 1 < pl.num_programs(0))
    def _():
        issue(i + 1, (i + 1) % 2)

    slot = i % 2
    for k in range(TOP_K):
        pltpu.make_async_copy(y_ref.at[pl.ds(0, tt // 8)], buf_ref.at[slot, k], sem.at[slot]).wait()
    p = p_ref[...]
    acc = h2_ref[...]
    for k in range(TOP_K):
        acc = acc + p[:, k:k + 1] * buf_ref[slot, k].reshape(tt, D_MODEL)
    o_ref[...] = acc * lax.rsqrt(jnp.mean(acc * acc, axis=-1, keepdims=True) + EPS) * fw_ref[...]


def _combine(dest, h2, p_cols, final_w, y_rows):
    n_tok = h2.shape[0]
    tt = COMB_TT
    return pl.pallas_call(
        _combine_kernel,
        grid_spec=pltpu.PrefetchScalarGridSpec(
            num_scalar_prefetch=1,
            grid=(n_tok // tt,),
            in_specs=[
                pl.BlockSpec((tt, D_MODEL), lambda i, d: (i, 0)),
                pl.BlockSpec((tt, TOP_K), lambda i, d: (i, 0)),
                pl.BlockSpec((1, D_MODEL), lambda i, d: (0, 0)),
                pl.BlockSpec(memory_space=pl.ANY),
            ],
            out_specs=pl.BlockSpec((tt, D_MODEL), lambda i, d: (i, 0)),
            scratch_shapes=[pltpu.VMEM((2, TOP_K, tt // 8, 8, D_MODEL), F32), pltpu.SemaphoreType.DMA((2,))],
        ),
        out_shape=jax.ShapeDtypeStruct((n_tok, D_MODEL), F32),
        compiler_params=_cparams(("arbitrary",)),
    )(dest, h2, p_cols, final_w, y_rows.reshape(y_rows.shape[0] // 8, 8, D_MODEL))


def _mixer(x2, meta, norm_w, w_in, dn_conv_w, dn_a_log, dn_dt_bias, dn_norm_w,
           ssd_conv_w, ssd_conv_b, ssd_a_log, ssd_dt_bias, ssd_d, ssd_norm_w, w_proj_dn, w_proj_ssd):
    o_qkv = 0
    o_dnz = o_qkv + DN_CONV_DIM
    o_b = o_dnz + DN_VAL_DIM
    o_a = o_b + DN_V_HEADS
    o_sz = o_a + DN_V_HEADS
    o_xbc = o_sz + SSD_D_INNER
    o_dt = o_xbc + SSD_CONV_DIM
    o_gate = o_dt + SSD_HEADS
    d_in = o_gate + 2 * D_MODEL
    w_small = jnp.concatenate([w_in[:, o_b:o_sz], w_in[:, o_dt:o_gate]], axis=1)

    hn, small = _norm_small(x2, meta, norm_w.reshape(1, D_MODEL), w_small)
    qkv = _matmul(hn, w_in[:, o_qkv:o_dnz].astype(BF16), BF16)
    dn_z = _matmul(hn, w_in[:, o_dnz:o_b].astype(BF16), BF16)
    ssd_z = _matmul(hn, w_in[:, o_sz:o_xbc].astype(BF16), BF16)
    xbc = _matmul(hn, w_in[:, o_xbc:o_dt].astype(BF16), BF16)
    gates = _matmul(hn, w_in[:, o_gate:d_in].astype(BF16), BF16)

    rows = hn.shape[0]
    sm_t = small.T
    dn_steps, dn_vps = DN_QK_HEADS // DN_QPS, 2 * DN_QPS
    b_rows = sm_t[0:DN_V_HEADS].reshape(dn_steps, dn_vps, rows)
    a_rows = sm_t[DN_V_HEADS:2 * DN_V_HEADS].reshape(dn_steps, dn_vps, rows)
    dt_rows = sm_t[2 * DN_V_HEADS:].reshape(SSD_GROUPS, SSD_REP, rows)

    bc = lambda p, g, r: jnp.broadcast_to(p.astype(F32).reshape(g, r, 1), (g, r, 128))
    y_dn = _gated_deltanet(qkv, dn_z, b_rows, a_rows, dn_conv_w,
                           bc(dn_a_log, dn_steps, dn_vps), bc(dn_dt_bias, dn_steps, dn_vps),
                           dn_norm_w.reshape(1, DN_HEAD_DIM))
    y_ssd = _mamba2_ssd(xbc, ssd_z, dt_rows, ssd_conv_w, ssd_conv_b.reshape(1, SSD_CONV_DIM),
                        bc(ssd_a_log, SSD_GROUPS, SSD_REP), bc(ssd_dt_bias, SSD_GROUPS, SSD_REP),
                        jnp.repeat(ssd_d, SSD_HEAD_DIM).reshape(1, SSD_D_INNER),
                        ssd_norm_w.reshape(1, SSD_D_INNER))
    return _merge(y_dn, y_ssd, gates, w_proj_dn.astype(BF16), w_proj_ssd.astype(BF16))


def _moe_and_final(merged, x2, w_out, norm_ffn_w, w_router, b_router, w_gate, b_gate, w_up, b_up,
                   w_down, b_down, final_norm_w):
    n_tok = x2.shape[0]
    h2, hn2, idx, probs, rank, cnt = _out_router(
        merged, w_out.astype(BF16), x2, norm_ffn_w.reshape(1, D_MODEL), w_router.T,
        b_router.reshape(N_EXPERTS, 1))

    counts = cnt[:, 0].astype(jnp.int32)
    padded = (counts + MOE_BM - 1) // MOE_BM * MOE_BM
    pad_end = jnp.cumsum(padded)
    pad_start = pad_end - padded
    hit = idx[:, :, None] == jnp.arange(N_EXPERTS, dtype=jnp.int32)
    dest = jnp.sum(jnp.where(hit, pad_start.astype(jnp.int32), 0), axis=-1) + rank
    nb = -(-(n_tok * TOP_K + N_EXPERTS * (MOE_BM - 1)) // MOE_BM)
    dest_t = dest.reshape(TOP_K, n_tok // COMB_TT, COMB_TT).transpose(1, 0, 2).reshape(-1)
    ztile = jnp.where(padded > 0, (pad_end - MOE_BM) // 8, -1).astype(jnp.int32)
    n_used = (pad_end[-1:] // MOE_BM).astype(jnp.int32)
    block_e = jnp.minimum(jnp.sum(pad_end[None, :] <= (jnp.arange(nb, dtype=jnp.int32) * MOE_BM)[:, None], axis=1),
                          N_EXPERTS - 1).astype(jnp.int32)
    ids = jnp.arange(N_EXPERTS, dtype=jnp.int32)
    later = (ids[None, :] > ids[:, None]) & (padded[None, :] > 0)
    next_e = jnp.min(jnp.where(later, ids[None, :], N_EXPERTS), axis=1)
    next_e = jnp.where(next_e < N_EXPERTS, next_e, -1).astype(jnp.int32)

    xs = _dispatch_rows(dest_t, ztile, n_used, hn2, nb)
    act = _expert_up(block_e, next_e, n_used, xs, w_gate, w_up,
                     b_gate.reshape(N_EXPERTS, 1, D_MODEL), b_up.reshape(N_EXPERTS, 1, D_MODEL))
    y_rows = _expert_down(block_e, next_e, n_used, act, w_down, b_down.reshape(N_EXPERTS, 1, D_MODEL))
    return _combine(dest_t, h2, probs.T, final_norm_w.reshape(1, D_MODEL), y_rows)


def kernel(x, meta_tokens, norm_mix_w, w_in, dn_conv_w, dn_a_log, dn_dt_bias, dn_norm_w, ssd_conv_w, ssd_conv_b, ssd_a_log, ssd_dt_bias, ssd_d, ssd_norm_w, w_proj_dn, w_proj_ssd, w_out, norm_ffn_w, w_router, b_router, w_gate, b_gate, w_up, b_up, w_down, b_down, final_norm_w):
    bsz, seq, d = x.shape
    assert bsz == 1 and d == D_MODEL and seq % TB == 0
    assert norm_mix_w.shape[0] == 1, "single-layer block"
    x2 = x.reshape(seq, d)
    merged = _mixer(x2, meta_tokens.astype(x.dtype), norm_mix_w[0], w_in[0], dn_conv_w[0], dn_a_log[0], dn_dt_bias[0], dn_norm_w[0],
                    ssd_conv_w[0], ssd_conv_b[0], ssd_a_log[0], ssd_dt_bias[0], ssd_d[0], ssd_norm_w[0],
                    w_proj_dn[0], w_proj_ssd[0])
    out = _moe_and_final(merged, x2, w_out[0], norm_ffn_w[0], w_router[0], b_router[0], w_gate[0], b_gate[0],
                         w_up[0], b_up[0], w_down[0], b_down[0], final_norm_w)
    return out.reshape(bsz, seq, d)
```

```python
import functools

import jax
import jax.numpy as jnp
from jax import lax
from jax.experimental import pallas as pl
from jax.experimental.pallas import tpu as pltpu

F32 = jnp.float32
BF16 = jnp.bfloat16

D_MODEL = 2048
N_META = 16
CHUNK = 64
EPS = 1e-6

DN_QK_HEADS = 16
DN_V_HEADS = 32
DN_HEAD_DIM = 128
DN_KEY_DIM = DN_QK_HEADS * DN_HEAD_DIM
DN_VAL_DIM = DN_V_HEADS * DN_HEAD_DIM
DN_CONV_DIM = 2 * DN_KEY_DIM + DN_VAL_DIM

SSD_D_INNER = 2 * D_MODEL
SSD_HEAD_DIM = 64
SSD_HEADS = SSD_D_INNER // SSD_HEAD_DIM
SSD_GROUPS = 8
SSD_REP = SSD_HEADS // SSD_GROUPS
SSD_STATE = 128
SSD_GS = SSD_GROUPS * SSD_STATE
SSD_CONV_DIM = SSD_D_INNER + 2 * SSD_GS
SSD_GROUP_DIM = SSD_D_INNER // SSD_GROUPS

N_EXPERTS = 32
TOP_K = 4
SWIGLU_LIMIT = 7.0
SWIGLU_ALPHA = 1.702

TB = 512
FRONT = TB
CPB = TB // CHUNK
MOE_BM = 256
MOE_TN_UP = 1024
MOE_TN_DOWN = 2048
COMB_TT = 256
DISP_TT = 512
ROUTER_RT = 512
DN_GROUP = 2
DN_QPS = 8
VMEM_LIMIT = 56 * 1024 * 1024


def _cparams(sem):
    return pltpu.CompilerParams(dimension_semantics=sem, vmem_limit_bytes=VMEM_LIMIT)


def _bdot(a, b):
    return jnp.dot(a.astype(BF16), b.astype(BF16), preferred_element_type=F32)


def _bdot_nt(a, b):
    return lax.dot_general(a.astype(BF16), b.astype(BF16), (((1,), (1,)), ((), ())),
                           preferred_element_type=F32)


def _bdot_tn(a, b):
    return lax.dot_general(a.astype(BF16), b.astype(BF16), (((0,), (0,)), ((), ())),
                           preferred_element_type=F32)


def _split3(x):
    x1 = x.astype(BF16)
    r1 = x - x1.astype(F32)
    x2 = r1.astype(BF16)
    x3 = (r1 - x2.astype(F32)).astype(BF16)
    return x1, x2, x3


def _dot_sel(x, m):
    x1, x2, x3 = _split3(x)
    d = lambda a: jnp.dot(a, m, preferred_element_type=F32)
    return d(x1) + d(x2) + d(x3)


def _dot3(a, b, nt=False):
    a1 = a.astype(BF16)
    a2 = (a - a1.astype(F32)).astype(BF16)
    b1 = b.astype(BF16)
    b2 = (b - b1.astype(F32)).astype(BF16)
    if nt:
        d = lambda p, q: lax.dot_general(p, q, (((1,), (1,)), ((), ())), preferred_element_type=F32)
    else:
        d = lambda p, q: jnp.dot(p, q, preferred_element_type=F32)
    return d(a1, b1) + d(a2, b1) + d(a1, b2)


def _sigmoid(x):
    return 1.0 / (1.0 + jnp.exp(-x))


def _silu(x):
    return x * _sigmoid(x)


def _softplus(x):
    return jnp.maximum(x, 0.0) + jnp.log(1.0 + jnp.exp(-jnp.abs(x)))


def _tile_lanes(p, n):
    return jnp.concatenate([p] * n, axis=1)


def _norm_small_kernel(x_ref, meta_ref, nw_ref, ws_ref, hn_ref, sm_ref):
    i = pl.program_id(0)

    def emit(h):
        y = h * lax.rsqrt(jnp.mean(h * h, axis=-1, keepdims=True) + EPS) * nw_ref[...]
        hn_ref[...] = y.astype(BF16)
        sm_ref[...] = _dot3(y, ws_ref[...])

    @pl.when(i == 0)
    def _():
        emit(jnp.concatenate([jnp.zeros((FRONT - N_META, D_MODEL), F32), meta_ref[...]], axis=0))

    @pl.when(i > 0)
    def _():
        emit(x_ref[...])


def _norm_small(x2, meta, nw, w_small):
    rows = FRONT + x2.shape[0]
    return pl.pallas_call(
        _norm_small_kernel,
        grid=(rows // TB,),
        in_specs=[pl.BlockSpec((TB, D_MODEL), lambda i: (jnp.maximum(i - 1, 0), 0)),
                  pl.BlockSpec((N_META, D_MODEL), lambda i: (0, 0)),
                  pl.BlockSpec((1, D_MODEL), lambda i: (0, 0)),
                  pl.BlockSpec((D_MODEL, 128), lambda i: (0, 0))],
        out_specs=[pl.BlockSpec((TB, D_MODEL), lambda i: (i, 0)),
                   pl.BlockSpec((TB, 128), lambda i: (i, 0))],
        out_shape=[jax.ShapeDtypeStruct((rows, D_MODEL), BF16),
                   jax.ShapeDtypeStruct((rows, 128), F32)],
        compiler_params=_cparams(("arbitrary",)),
    )(x2, meta, nw, w_small)


def _mm_kernel(x_ref, w_ref, o_ref):
    o_ref[...] = jnp.dot(x_ref[...], w_ref[...], preferred_element_type=F32).astype(o_ref.dtype)


def _matmul(x, w, out_dtype, tm=1024, tn=2048):
    m, k = x.shape
    n = w.shape[1]
    return pl.pallas_call(
        _mm_kernel,
        grid=(n // tn, pl.cdiv(m, tm)),
        in_specs=[pl.BlockSpec((tm, k), lambda j, i: (i, 0)),
                  pl.BlockSpec((k, tn), lambda j, i: (0, j))],
        out_specs=pl.BlockSpec((tm, tn), lambda j, i: (i, j)),
        out_shape=jax.ShapeDtypeStruct((m, n), out_dtype),
        compiler_params=_cparams(("arbitrary", "arbitrary")),
    )(x, w)


def _conv_silu(raw_ref, tail_ref, w_ref, b_ref=None):
    x = raw_ref[...].astype(F32)
    rows = x.shape[0]
    ext = jnp.concatenate([tail_ref[...], x], axis=0)
    w = w_ref[...]
    y = x * w[3:4]
    for j in (1, 2, 3):
        y = y + pltpu.roll(ext, j, 0)[8:8 + rows] * w[3 - j:4 - j]
    tail_ref[...] = x[rows - 8:rows]
    if b_ref is not None:
        y = y + b_ref[...]
    return _silu(y)


def _fill_chunk_matrices(u_ref, e_ref):
    jj = lax.broadcasted_iota(jnp.int32, (TB, TB), 0)
    ii = lax.broadcasted_iota(jnp.int32, (TB, TB), 1)
    same = (jj // CHUNK) == (ii // CHUNK)
    u_ref[...] = jnp.where(same & (jj <= ii), 1.0, 0.0).astype(BF16)
    e_ref[...] = jnp.where(same & (jj % CHUNK == CHUNK - 1), 1.0, 0.0).astype(BF16)


def _col_bcast(row, width):
    return jnp.broadcast_to(row, (width, TB)).T


def _dn_kernel(q_ref, k_ref, v_ref, z_ref, b_ref, a_ref, wq_ref, wk_ref, wv_ref, alog_ref, dtb_ref, nw_ref,
               o_ref, s_ref, tq_ref, tk_ref, tv_ref, u_ref, e_ref):
    t = pl.program_id(1)

    @pl.when(t == 0)
    def _():
        s_ref[...] = jnp.zeros_like(s_ref)
        tq_ref[...] = jnp.zeros_like(tq_ref)
        tk_ref[...] = jnp.zeros_like(tk_ref)
        tv_ref[...] = jnp.zeros_like(tv_ref)
        _fill_chunk_matrices(u_ref, e_ref)

    d = DN_HEAD_DIM
    nq = q_ref.shape[1] // d
    q_all = _conv_silu(q_ref, tq_ref, wq_ref)
    k_all = _conv_silu(k_ref, tk_ref, wk_ref)
    v = _conv_silu(v_ref, tv_ref, wv_ref)
    qs, ks = [], []
    for i in range(nq):
        q = q_all[:, i * d:(i + 1) * d]
        k = k_all[:, i * d:(i + 1) * d]
        qs.append(q * (lax.rsqrt(jnp.sum(q * q, axis=-1, keepdims=True) + EPS) * (d ** -0.5)))
        ks.append(k * lax.rsqrt(jnp.sum(k * k, axis=-1, keepdims=True) + EPS))

    beta = _sigmoid(b_ref[...])
    g = -jnp.exp(_tile_lanes(alog_ref[...], TB // 128)) * _softplus(a_ref[...] + _tile_lanes(dtb_ref[...], TB // 128))
    gc = _dot_sel(g, u_ref[...])
    gl = _dot_sel(gc, e_ref[...])
    eg = jnp.exp(gc)
    ek = jnp.exp(gl - gc)

    gw = DN_GROUP * CHUNK
    lg = CHUNK.bit_length() - 1
    ii = lax.broadcasted_iota(jnp.int32, (gw, gw), 0)
    jj = lax.broadcasted_iota(jnp.int32, (gw, gw), 1)
    same = (ii >> lg) == (jj >> lg)
    causal = same & (ii >= jj)
    strict = same & (ii > jj)
    eye = jnp.where(ii == jj, 1.0, 0.0)
    masks = [strict & ((ii >> 1) == (jj >> 1))]
    blk = 2
    while blk < CHUNK:
        sh = blk.bit_length() - 1
        masks.append(((ii >> (sh + 1)) == (jj >> (sh + 1))) & (((ii >> sh) & 1) == 1) & (((jj >> sh) & 1) == 0))
        blk *= 2

    heads = range(2 * nq)
    chunks = range(CPB)
    groups = range(TB // gw)
    rows_of = [slice(c * CHUNK, (c + 1) * CHUNK) for c in chunks]
    grows = [slice(s * gw, (s + 1) * gw) for s in groups]
    kk = [[_bdot_nt(ks[i][grows[s]], ks[i][grows[s]]) for s in groups] for i in range(nq)]
    qk = [[_bdot_nt(qs[i][grows[s]], ks[i][grows[s]]) for s in groups] for i in range(nq)]

    eg_cb, rhs_all, qd_all, kd_all = [], [], [], []
    dec, a = {}, {}
    for hh in heads:
        q, k = qs[hh // 2], ks[hh // 2]
        gc_cb = _col_bcast(gc[hh:hh + 1], max(gw, DN_HEAD_DIM))
        beta_cb = _col_bcast(beta[hh:hh + 1], max(gw, DN_HEAD_DIM))
        eg_cb.append(_col_bcast(eg[hh:hh + 1], DN_HEAD_DIM))
        ek_cb = _col_bcast(ek[hh:hh + 1], DN_HEAD_DIM)
        beta_h = beta_cb[:, :DN_HEAD_DIM]
        vh = v[:, hh * DN_HEAD_DIM:(hh + 1) * DN_HEAD_DIM]
        rhs_all.append(jnp.concatenate([vh * beta_h, k * (beta_h * eg_cb[hh])], axis=1))
        qd_all.append(q * eg_cb[hh])
        kd_all.append(k * ek_cb)
        for s in groups:
            diff = gc_cb[grows[s], :gw] - gc[hh:hh + 1, s * gw:(s + 1) * gw]
            dec[hh, s] = jnp.where(causal, jnp.exp(jnp.where(causal, diff, 0.0)), 0.0)
            a[hh, s] = jnp.where(strict, beta_cb[grows[s], :gw] * kk[hh // 2][s] * dec[hh, s], 0.0)
    inst = [(hh, s) for s in groups for hh in heads]

    tinv = {i: eye - jnp.where(masks[0], a[i], 0.0) for i in inst}
    for m in masks[1:]:
        x = {i: _bdot(jnp.where(m, a[i], 0.0), tinv[i]) for i in inst}
        tinv = {i: tinv[i] - _bdot(tinv[i], x[i]) for i in inst}
    uw_g = {(hh, s): _bdot(tinv[hh, s], rhs_all[hh][grows[s]]) for hh, s in inst}
    qo_g = {(hh, s): _bdot(qk[hh // 2][s] * dec[hh, s], uw_g[hh, s]) for hh, s in inst}
    in_group = [slice((c % DN_GROUP) * CHUNK, (c % DN_GROUP + 1) * CHUNK) for c in chunks]
    uw = {(hh, c): uw_g[hh, c // DN_GROUP][in_group[c]] for c in chunks for hh in heads}
    qo = {(hh, c): qo_g[hh, c // DN_GROUP][in_group[c]] for c in chunks for hh in heads}
    nk = {(hh, c): _bdot_tn(uw[hh, c], kd_all[hh][rows_of[c]]) for c in chunks for hh in heads}

    state_t = [s_ref[hh] for hh in heads]
    before = {}
    for c in chunks:
        for hh in heads:
            before[hh, c] = state_t[hh]
            g_end = eg_cb[hh][(c + 1) * CHUNK - 1:(c + 1) * CHUNK, :]
            state_t[hh] = (state_t[hh] * g_end - _bdot(state_t[hh], nk[hh, c][DN_HEAD_DIM:])
                           + nk[hh, c][:DN_HEAD_DIM])
    nw = nw_ref[...]
    for hh in heads:
        s_ref[hh] = state_t[hh]
        outs = [_bdot_nt(qd_all[hh][rows_of[c]] - qo[hh, c][:, DN_HEAD_DIM:], before[hh, c])
                + qo[hh, c][:, :DN_HEAD_DIM] for c in chunks]
        o = jnp.concatenate(outs, axis=0)
        zh = z_ref[:, hh * DN_HEAD_DIM:(hh + 1) * DN_HEAD_DIM].astype(F32)
        o = o * lax.rsqrt(jnp.mean(o * o, axis=-1, keepdims=True) + EPS) * nw * _silu(zh)
        o_ref[:, hh * DN_HEAD_DIM:(hh + 1) * DN_HEAD_DIM] = o.astype(o_ref.dtype)


def _gated_deltanet(qkv, z, b_rows, a_rows, conv_w, alog_b, dtb_b, norm_w):
    rows = qkv.shape[0]
    nstep = DN_QK_HEADS // DN_QPS
    qw = DN_QPS * DN_HEAD_DIM
    return pl.pallas_call(
        _dn_kernel,
        grid=(nstep, rows // TB),
        in_specs=[
            pl.BlockSpec((TB, qw), lambda j, t: (t, j)),
            pl.BlockSpec((TB, qw), lambda j, t: (t, nstep + j)),
            pl.BlockSpec((TB, 2 * qw), lambda j, t: (t, nstep + j)),
            pl.BlockSpec((TB, 2 * qw), lambda j, t: (t, j)),
            pl.BlockSpec((None, 2 * DN_QPS, TB), lambda j, t: (j, 0, t)),
            pl.BlockSpec((None, 2 * DN_QPS, TB), lambda j, t: (j, 0, t)),
            pl.BlockSpec((4, qw), lambda j, t: (0, j)),
            pl.BlockSpec((4, qw), lambda j, t: (0, nstep + j)),
            pl.BlockSpec((4, 2 * qw), lambda j, t: (0, nstep + j)),
            pl.BlockSpec((None, 2 * DN_QPS, 128), lambda j, t: (j, 0, 0)),
            pl.BlockSpec((None, 2 * DN_QPS, 128), lambda j, t: (j, 0, 0)),
            pl.BlockSpec((1, DN_HEAD_DIM), lambda j, t: (0, 0)),
        ],
        out_specs=pl.BlockSpec((TB, 2 * qw), lambda j, t: (t, j)),
        out_shape=jax.ShapeDtypeStruct((rows, DN_VAL_DIM), BF16),
        scratch_shapes=[
            pltpu.VMEM((2 * DN_QPS, DN_HEAD_DIM, DN_HEAD_DIM), F32),
            pltpu.VMEM((8, qw), F32),
            pltpu.VMEM((8, qw), F32),
            pltpu.VMEM((8, 2 * qw), F32),
            pltpu.VMEM((TB, TB), BF16),
            pltpu.VMEM((TB, TB), BF16),
        ],
        compiler_params=_cparams(("arbitrary", "arbitrary")),
    )(qkv, qkv, qkv, z, b_rows, a_rows, conv_w, conv_w, conv_w, alog_b, dtb_b, norm_w)


def _ssd_kernel(x_ref, b_ref, c_ref, z_ref, dt_ref, wx_ref, wb_ref, wc_ref, bx_ref, bb_ref, bc_ref,
                alog_ref, dtb_ref, dexp_ref, nw_ref, o_ref, h_ref, tx_ref, tb_ref, tc_ref, u_ref, e_ref):
    t = pl.program_id(1)

    @pl.when(t == 0)
    def _():
        h_ref[...] = jnp.zeros_like(h_ref)
        tx_ref[...] = jnp.zeros_like(tx_ref)
        tb_ref[...] = jnp.zeros_like(tb_ref)
        tc_ref[...] = jnp.zeros_like(tc_ref)
        _fill_chunk_matrices(u_ref, e_ref)

    xh = _conv_silu(x_ref, tx_ref, wx_ref, bx_ref)
    bm = _conv_silu(b_ref, tb_ref, wb_ref, bb_ref)
    cm = _conv_silu(c_ref, tc_ref, wc_ref, bc_ref)

    nrep = TB // 128
    dt = _softplus(dt_ref[...] + _tile_lanes(dtb_ref[...], nrep))
    pos = t * TB + lax.broadcasted_iota(jnp.int32, (SSD_REP, TB), 1)
    dt = jnp.where(pos >= FRONT - N_META, dt, 0.0)
    la = -jnp.exp(_tile_lanes(alog_ref[...], nrep)) * dt
    ac = _dot_sel(la, u_ref[...])
    al = _dot_sel(ac, e_ref[...])
    dtek = dt * jnp.exp(al - ac)

    def col_bcast_heads(rowsv):
        stacked = jnp.concatenate(
            [jnp.broadcast_to(rowsv[r:r + 1], (SSD_HEAD_DIM, TB)) for r in range(SSD_REP)], axis=0)
        return stacked.T

    ac_cb = col_bcast_heads(ac)
    ea_cb = jnp.exp(ac_cb)
    xd = xh * col_bcast_heads(dtek)

    ii = lax.broadcasted_iota(jnp.int32, (CHUNK, 2 * CHUNK), 0)
    jj = lax.broadcasted_iota(jnp.int32, (CHUNK, 2 * CHUNK), 1)
    causal2 = ii >= (jj & (CHUNK - 1))
    li = lax.broadcasted_iota(jnp.int32, (CHUNK, 2 * CHUNK), 1)
    lo_half = li < CHUNK

    state = h_ref[...]
    outs = []
    for c in range(CPB):
        rows = slice(c * CHUNK, (c + 1) * CHUNK)
        cc = cm[rows]
        bc = bm[rows]
        cb = _bdot_nt(cc, bc)
        cb2 = jnp.concatenate([cb, cb], axis=1)
        y_off = _bdot(cc, state) * ea_cb[rows]
        yd = []
        for m in range(SSD_REP // 2):
            lanes = slice(m * 128, (m + 1) * 128)
            rowv = jnp.concatenate([ac[2 * m:2 * m + 1, rows], ac[2 * m + 1:2 * m + 2, rows]], axis=1)
            dtrow = jnp.concatenate([dt[2 * m:2 * m + 1, rows], dt[2 * m + 1:2 * m + 2, rows]], axis=1)
            diff = ac_cb[rows, lanes] - rowv
            seg = jnp.where(causal2, jnp.exp(jnp.where(causal2, diff, 0.0)), 0.0)
            mm = cb2 * seg * dtrow
            x2 = xh[rows, lanes]
            rhs = jnp.concatenate([jnp.where(lo_half, x2, 0.0), jnp.where(lo_half, 0.0, x2)], axis=0)
            yd.append(_bdot(mm, rhs))
        outs.append(jnp.concatenate(yd, axis=1) + y_off)
        ea_last = ea_cb[(c + 1) * CHUNK - 1:(c + 1) * CHUNK, :]
        state = state * ea_last + _bdot_tn(bc, xd[rows])
    h_ref[...] = state

    y = jnp.concatenate(outs, axis=0) + dexp_ref[...] * xh
    y = y * _silu(z_ref[...].astype(F32))
    y = y * lax.rsqrt(jnp.mean(y * y, axis=-1, keepdims=True) + EPS) * nw_ref[...]
    o_ref[...] = y.astype(o_ref.dtype)


def _mamba2_ssd(xbc, z, dt_rows, conv_w, conv_b, alog_b, dtb_b, d_exp, norm_w):
    rows = xbc.shape[0]
    gd = SSD_GROUP_DIM
    nx = SSD_D_INNER // SSD_STATE
    return pl.pallas_call(
        _ssd_kernel,
        grid=(SSD_GROUPS, rows // TB),
        in_specs=[
            pl.BlockSpec((TB, gd), lambda g, t: (t, g)),
            pl.BlockSpec((TB, SSD_STATE), lambda g, t: (t, nx + g)),
            pl.BlockSpec((TB, SSD_STATE), lambda g, t: (t, nx + SSD_GROUPS + g)),
            pl.BlockSpec((TB, gd), lambda g, t: (t, g)),
            pl.BlockSpec((None, SSD_REP, TB), lambda g, t: (g, 0, t)),
            pl.BlockSpec((4, gd), lambda g, t: (0, g)),
            pl.BlockSpec((4, SSD_STATE), lambda g, t: (0, nx + g)),
            pl.BlockSpec((4, SSD_STATE), lambda g, t: (0, nx + SSD_GROUPS + g)),
            pl.BlockSpec((1, gd), lambda g, t: (0, g)),
            pl.BlockSpec((1, SSD_STATE), lambda g, t: (0, nx + g)),
            pl.BlockSpec((1, SSD_STATE), lambda g, t: (0, nx + SSD_GROUPS + g)),
            pl.BlockSpec((None, SSD_REP, 128), lambda g, t: (g, 0, 0)),
            pl.BlockSpec((None, SSD_REP, 128), lambda g, t: (g, 0, 0)),
            pl.BlockSpec((1, gd), lambda g, t: (0, g)),
            pl.BlockSpec((1, gd), lambda g, t: (0, g)),
        ],
        out_specs=pl.BlockSpec((TB, gd), lambda g, t: (t, g)),
        out_shape=jax.ShapeDtypeStruct((rows, SSD_D_INNER), BF16),
        scratch_shapes=[
            pltpu.VMEM((SSD_STATE, gd), F32),
            pltpu.VMEM((8, gd), F32),
            pltpu.VMEM((8, SSD_STATE), F32),
            pltpu.VMEM((8, SSD_STATE), F32),
            pltpu.VMEM((TB, TB), BF16),
            pltpu.VMEM((TB, TB), BF16),
        ],
        compiler_params=_cparams(("arbitrary", "arbitrary")),
    )(xbc, xbc, xbc, z, dt_rows, conv_w, conv_w, conv_w, conv_b, conv_b, conv_b, alog_b, dtb_b, d_exp, norm_w)


def _merge_kernel(ydn_ref, yssd_ref, gd_ref, gs_ref, wdn_ref, wssd_ref, o_ref):
    a = jnp.dot(ydn_ref[...], wdn_ref[...], preferred_element_type=F32)
    b = jnp.dot(yssd_ref[...], wssd_ref[...], preferred_element_type=F32)
    o = _sigmoid(gd_ref[...].astype(F32)) * a + _sigmoid(gs_ref[...].astype(F32)) * b
    o_ref[...] = o.astype(o_ref.dtype)


def _merge(y_dn, y_ssd, gates, w_dn, w_ssd, tn=512):
    ntile = (y_dn.shape[0] - FRONT) // TB
    off = FRONT // TB
    ng = D_MODEL // tn
    return pl.pallas_call(
        _merge_kernel,
        grid=(ng, ntile),
        in_specs=[
            pl.BlockSpec((TB, DN_VAL_DIM), lambda j, i: (i + off, 0)),
            pl.BlockSpec((TB, SSD_D_INNER), lambda j, i: (i + off, 0)),
            pl.BlockSpec((TB, tn), lambda j, i: (i + off, j)),
            pl.BlockSpec((TB, tn), lambda j, i: (i + off, ng + j)),
            pl.BlockSpec((DN_VAL_DIM, tn), lambda j, i: (0, j)),
            pl.BlockSpec((SSD_D_INNER, tn), lambda j, i: (0, j)),
        ],
        out_specs=pl.BlockSpec((TB, tn), lambda j, i: (i, j)),
        out_shape=jax.ShapeDtypeStruct((ntile * TB, D_MODEL), BF16),
        compiler_params=_cparams(("arbitrary", "arbitrary")),
    )(y_dn, y_ssd, gates, gates, w_dn, w_ssd)


def _out_router_kernel(m_ref, wo_ref, x_ref, nw_ref, wr_ref, br_ref,
                       h2_ref, hn_ref, idx_ref, p_ref, rank_ref, cnt_ref, carry_ref, us_ref):
    i = pl.program_id(0)
    rt = x_ref.shape[0]

    @pl.when(i == 0)
    def _():
        carry_ref[...] = jnp.zeros_like(carry_ref)
        jj = lax.broadcasted_iota(jnp.int32, (rt, rt), 0)
        ii = lax.broadcasted_iota(jnp.int32, (rt, rt), 1)
        us_ref[...] = jnp.where(jj < ii, 1.0, 0.0).astype(BF16)

    h2 = x_ref[...] + jnp.dot(m_ref[...], wo_ref[...], preferred_element_type=F32)
    h2_ref[...] = h2
    hn = h2 * lax.rsqrt(jnp.mean(h2 * h2, axis=-1, keepdims=True) + EPS) * nw_ref[...]
    hn_ref[...] = hn

    lg = _dot3(wr_ref[...], hn, nt=True) + br_ref[...]
    eio = lax.broadcasted_iota(jnp.int32, (N_EXPERTS, rt), 0)
    vals, hits, idxs = [], [], []
    cur = lg
    for _ in range(TOP_K):
        m = jnp.max(cur, axis=0, keepdims=True)
        ik = jnp.min(jnp.where(cur == m, eio, N_EXPERTS), axis=0, keepdims=True)
        hit = eio == ik
        vals.append(m)
        idxs.append(ik)
        hits.append(hit)
        cur = jnp.where(hit, -jnp.inf, cur)
    ex = [jnp.exp(vk - vals[0]) for vk in vals]
    den = ex[0] + ex[1] + ex[2] + ex[3]
    idx_ref[...] = jnp.concatenate(idxs, axis=0)
    p_ref[...] = jnp.concatenate([e / den for e in ex], axis=0)

    sel = jnp.zeros((N_EXPERTS, rt), F32)
    for hit in hits:
        sel = sel + jnp.where(hit, 1.0, 0.0)
    before = jnp.dot(sel.astype(BF16), us_ref[...], preferred_element_type=F32) \
        + _tile_lanes(carry_ref[...], rt // 128)
    ranks = [jnp.sum(jnp.where(hit, before, 0.0), axis=0, keepdims=True) for hit in hits]
    rank_ref[...] = jnp.concatenate(ranks, axis=0).astype(jnp.int32)
    carry = carry_ref[...] + jnp.dot(sel.astype(BF16), jnp.ones((rt, 128), BF16), preferred_element_type=F32)
    carry_ref[...] = carry
    cnt_ref[...] = carry


def _out_router(merged, w_out, x, nw, w_router_t, b_router_c):
    n_tok = x.shape[0]
    rt = ROUTER_RT
    tok_spec = pl.BlockSpec((TOP_K, rt), lambda i: (0, i))
    return pl.pallas_call(
        _out_router_kernel,
        grid=(n_tok // rt,),
        in_specs=[
            pl.BlockSpec((rt, D_MODEL), lambda i: (i, 0)),
            pl.BlockSpec((D_MODEL, D_MODEL), lambda i: (0, 0), pipeline_mode=pl.Buffered(1)),
            pl.BlockSpec((rt, D_MODEL), lambda i: (i, 0)),
            pl.BlockSpec((1, D_MODEL), lambda i: (0, 0)),
            pl.BlockSpec((N_EXPERTS, D_MODEL), lambda i: (0, 0)),
            pl.BlockSpec((N_EXPERTS, 1), lambda i: (0, 0)),
        ],
        out_specs=[
            pl.BlockSpec((rt, D_MODEL), lambda i: (i, 0)),
            pl.BlockSpec((rt, D_MODEL), lambda i: (i, 0)),
            tok_spec, tok_spec, tok_spec,
            pl.BlockSpec((N_EXPERTS, 128), lambda i: (0, 0)),
        ],
        out_shape=[
            jax.ShapeDtypeStruct((n_tok, D_MODEL), F32),
            jax.ShapeDtypeStruct((n_tok, D_MODEL), F32),
            jax.ShapeDtypeStruct((TOP_K, n_tok), jnp.int32),
            jax.ShapeDtypeStruct((TOP_K, n_tok), F32),
            jax.ShapeDtypeStruct((TOP_K, n_tok), jnp.int32),
            jax.ShapeDtypeStruct((N_EXPERTS, 128), F32),
        ],
        scratch_shapes=[pltpu.VMEM((N_EXPERTS, 128), F32), pltpu.VMEM((rt, rt), BF16)],
        compiler_params=_cparams(("arbitrary",)),
    )(merged, w_out, x, nw, w_router_t, b_router_c)


def _dispatch_kernel(dest_ref, ztile_ref, nused_ref, x_ref, o_ref, zbuf, sem, zsem):
    i = pl.program_id(0)
    bt = MOE_BM // 8

    @pl.when(i == 0)
    def _():
        zbuf[...] = jnp.zeros_like(zbuf)

        def zero_block(first_tile):
            cp = pltpu.make_async_copy(zbuf, o_ref.at[pl.ds(first_tile, bt)], zsem)
            cp.start()
            cp.wait()

        for e in range(N_EXPERTS):
            @pl.when(ztile_ref[e] >= 0)
            def _(e=e):
                zero_block(ztile_ref[e])

        def tail(b, carry):
            zero_block(b * bt)
            return carry

        lax.fori_loop(nused_ref[0], o_ref.shape[0] // bt, tail, 0)

    nt = x_ref.shape[0]
    base = i * (TOP_K * nt * 8)

    def body(g, carry):
        for s in range(8):
            for k in range(TOP_K):
                row = dest_ref[base + k * (nt * 8) + g * 8 + s]
                pltpu.make_async_copy(x_ref.at[g, pl.ds(s, 1)], o_ref.at[row >> 3, pl.ds(row & 7, 1)], sem).start()
        return carry

    lax.fori_loop(0, nt, body, 0)
    for k in range(TOP_K):
        pltpu.make_async_copy(x_ref, o_ref.at[pl.ds(0, nt)], sem).wait()


def _dispatch_rows(dest, ztile, n_used, src, nb):
    n_tok = src.shape[0]
    tiles = DISP_TT // 8
    out = pl.pallas_call(
        _dispatch_kernel,
        grid_spec=pltpu.PrefetchScalarGridSpec(
            num_scalar_prefetch=3,
            grid=(n_tok // DISP_TT,),
            in_specs=[pl.BlockSpec((tiles, 8, D_MODEL), lambda i, *_: (i, 0, 0))],
            out_specs=pl.BlockSpec(memory_space=pl.ANY),
            scratch_shapes=[pltpu.VMEM((MOE_BM // 8, 8, D_MODEL), F32),
                            pltpu.SemaphoreType.DMA, pltpu.SemaphoreType.DMA],
        ),
        out_shape=jax.ShapeDtypeStruct((nb * MOE_BM // 8, 8, D_MODEL), F32),
        compiler_params=_cparams(("arbitrary",)),
    )(dest, ztile, n_used, src.reshape(n_tok // 8, 8, D_MODEL))
    return out.reshape(nb * MOE_BM, D_MODEL)


def _expert_weights(e_ref, nxt_ref, nused_ref, run_ref, w_hbm, w_stage, w_bf16, wsem):
    j = pl.program_id(0)
    b = pl.program_id(1)
    tn = w_bf16[0].shape[1]
    e = e_ref[b]
    first = jnp.logical_and(b < nused_ref[0],
                            jnp.logical_or(b == 0, e != e_ref[jnp.maximum(b - 1, 0)]))

    def fetch(ee, jj, slot):
        cols = pl.ds(pl.multiple_of(jj * tn, tn), tn)
        return [pltpu.make_async_copy(w.at[ee, pl.ds(0, D_MODEL), cols], st.at[slot], wsem.at[slot])
                for w, st in zip(w_hbm, w_stage)]

    @pl.when(jnp.logical_and(j == 0, b == 0))
    def _():
        run_ref[0] = 0
        for cp in fetch(e, 0, 0):
            cp.start()

    @pl.when(first)
    def _():
        slot = run_ref[0] % 2
        for cp in fetch(e, j, slot):
            cp.wait()
        for st, wb in zip(w_stage, w_bf16):
            wb[...] = st[slot].astype(BF16)
        nxt = nxt_ref[e]
        same_tile = nxt >= 0

        @pl.when(jnp.logical_or(same_tile, j + 1 < pl.num_programs(0)))
        def _():
            for cp in fetch(jnp.where(same_tile, nxt, e_ref[0]), jnp.where(same_tile, j, j + 1), 1 - slot):
                cp.start()

        run_ref[0] = run_ref[0] + 1


def _expert_up_kernel(e_ref, nxt_ref, nused_ref, x_ref, wg_hbm, wu_hbm, bg_ref, bu_ref, o_ref,
                      wg_st, wu_st, wgb_ref, wub_ref, run_ref, wsem):
    b = pl.program_id(1)
    _expert_weights(e_ref, nxt_ref, nused_ref, run_ref, (wg_hbm, wu_hbm), (wg_st, wu_st),
                    (wgb_ref, wub_ref), wsem)

    @pl.when(b < nused_ref[0])
    def _():
        x = x_ref[...].astype(BF16)
        gt = jnp.dot(x, wgb_ref[...], preferred_element_type=F32) + bg_ref[...]
        up = jnp.dot(x, wub_ref[...], preferred_element_type=F32) + bu_ref[...]
        gt = jnp.minimum(gt, SWIGLU_LIMIT)
        up = jnp.clip(up, -SWIGLU_LIMIT, SWIGLU_LIMIT)
        o_ref[...] = (gt * _sigmoid(SWIGLU_ALPHA * gt) * (up + 1.0)).astype(o_ref.dtype)

    @pl.when(b >= nused_ref[0])
    def _():
        o_ref[...] = jnp.zeros_like(o_ref)


def _expert_up(block_e, next_e, n_used, xs, w_gate, w_up, b_gate, b_up):
    nb = xs.shape[0] // MOE_BM
    tn = MOE_TN_UP
    bspec = pl.BlockSpec((None, 1, tn), lambda j, b, be, *_: (be[b], 0, j))
    return pl.pallas_call(
        _expert_up_kernel,
        grid_spec=pltpu.PrefetchScalarGridSpec(
            num_scalar_prefetch=3,
            grid=(D_MODEL // tn, nb),
            in_specs=[pl.BlockSpec((MOE_BM, D_MODEL), lambda j, b, *_: (b, 0)),
                      pl.BlockSpec(memory_space=pl.ANY), pl.BlockSpec(memory_space=pl.ANY), bspec, bspec],
            out_specs=pl.BlockSpec((MOE_BM, tn), lambda j, b, *_: (b, j)),
            scratch_shapes=[pltpu.VMEM((2, D_MODEL, tn), F32), pltpu.VMEM((2, D_MODEL, tn), F32),
                            pltpu.VMEM((D_MODEL, tn), BF16), pltpu.VMEM((D_MODEL, tn), BF16),
                            pltpu.SMEM((1,), jnp.int32), pltpu.SemaphoreType.DMA((2,))],
        ),
        out_shape=jax.ShapeDtypeStruct(xs.shape, BF16),
        compiler_params=_cparams(("arbitrary", "arbitrary")),
    )(block_e, next_e, n_used, xs, w_gate, w_up, b_gate, b_up)


def _expert_down_kernel(e_ref, nxt_ref, nused_ref, a_ref, wd_hbm, bd_ref, o_ref, wd_st, wdb_ref, run_ref, wsem):
    b = pl.program_id(1)
    _expert_weights(e_ref, nxt_ref, nused_ref, run_ref, (wd_hbm,), (wd_st,), (wdb_ref,), wsem)

    @pl.when(b < nused_ref[0])
    def _():
        o_ref[...] = jnp.dot(a_ref[...], wdb_ref[...], preferred_element_type=F32) + bd_ref[...]

    @pl.when(b >= nused_ref[0])
    def _():
        o_ref[...] = jnp.zeros_like(o_ref)


def _expert_down(block_e, next_e, n_used, act, w_down, b_down):
    nb = act.shape[0] // MOE_BM
    tn = MOE_TN_DOWN
    return pl.pallas_call(
        _expert_down_kernel,
        grid_spec=pltpu.PrefetchScalarGridSpec(
            num_scalar_prefetch=3,
            grid=(D_MODEL // tn, nb),
            in_specs=[pl.BlockSpec((MOE_BM, D_MODEL), lambda j, b, *_: (b, 0)),
                      pl.BlockSpec(memory_space=pl.ANY),
                      pl.BlockSpec((None, 1, tn), lambda j, b, be, *_: (be[b], 0, j))],
            out_specs=pl.BlockSpec((MOE_BM, tn), lambda j, b, *_: (b, j)),
            scratch_shapes=[pltpu.VMEM((2, D_MODEL, tn), F32), pltpu.VMEM((D_MODEL, tn), BF16),
                            pltpu.SMEM((1,), jnp.int32), pltpu.SemaphoreType.DMA((2,))],
        ),
        out_shape=jax.ShapeDtypeStruct(act.shape, F32),
        compiler_params=_cparams(("arbitrary", "arbitrary")),
    )(block_e, next_e, n_used, act, w_down, b_down)


def _combine_kernel(dest_ref, h2_ref, p_ref, fw_ref, y_ref, o_ref, buf_ref, sem):
    i = pl.program_id(0)
    tt = COMB_TT

    def issue(tile, slot):
        base = tile * (TOP_K * tt)

        def body(g, carry):
            for s in range(8):
                for k in range(TOP_K):
                    row = dest_ref[base + k * tt + g * 8 + s]
                    pltpu.make_async_copy(y_ref.at[row >> 3, pl.ds(row & 7, 1)],
                                          buf_ref.at[slot, k, g, pl.ds(s, 1)], sem.at[slot]).start()
            return carry

        lax.fori_loop(0, tt // 8, body, 0)

    @pl.when(i == 0)
    def _():
        issue(0, 0)

    @pl.when(i + 1 < pl.num_programs(0))
    def _():
        issue(i + 1, (i + 1) % 2)

    slot = i % 2
    for k in range(TOP_K):
        pltpu.make_async_copy(y_ref.at[pl.ds(0, tt // 8)], buf_ref.at[slot, k], sem.at[slot]).wait()
    p = p_ref[...]
    acc = h2_ref[...]
    for k in range(TOP_K):
        acc = acc + p[:, k:k + 1] * buf_ref[slot, k].reshape(tt, D_MODEL)
    o_ref[...] = acc * lax.rsqrt(jnp.mean(acc * acc, axis=-1, keepdims=True) + EPS) * fw_ref[...]


def _combine(dest, h2, p_cols, final_w, y_rows):
    n_tok = h2.shape[0]
    tt = COMB_TT
    return pl.pallas_call(
        _combine_kernel,
        grid_spec=pltpu.PrefetchScalarGridSpec(
            num_scalar_prefetch=1,
            grid=(n_tok // tt,),
            in_specs=[
                pl.BlockSpec((tt, D_MODEL), lambda i, d: (i, 0)),
                pl.BlockSpec((tt, TOP_K), lambda i, d: (i, 0)),
                pl.BlockSpec((1, D_MODEL), lambda i, d: (0, 0)),
                pl.BlockSpec(memory_space=pl.ANY),
            ],
            out_specs=pl.BlockSpec((tt, D_MODEL), lambda i, d: (i, 0)),
            scratch_shapes=[pltpu.VMEM((2, TOP_K, tt // 8, 8, D_MODEL), F32), pltpu.SemaphoreType.DMA((2,))],
        ),
        out_shape=jax.ShapeDtypeStruct((n_tok, D_MODEL), F32),
        compiler_params=_cparams(("arbitrary",)),
    )(dest, h2, p_cols, final_w, y_rows.reshape(y_rows.shape[0] // 8, 8, D_MODEL))


def _mixer(x2, meta, norm_w, w_in, dn_conv_w, dn_a_log, dn_dt_bias, dn_norm_w,
           ssd_conv_w, ssd_conv_b, ssd_a_log, ssd_dt_bias, ssd_d, ssd_norm_w, w_proj_dn, w_proj_ssd):
    o_qkv = 0
    o_dnz = o_qkv + DN_CONV_DIM
    o_b = o_dnz + DN_VAL_DIM
    o_a = o_b + DN_V_HEADS
    o_sz = o_a + DN_V_HEADS
    o_xbc = o_sz + SSD_D_INNER
    o_dt = o_xbc + SSD_CONV_DIM
    o_gate = o_dt + SSD_HEADS
    d_in = o_gate + 2 * D_MODEL
    w_small = jnp.concatenate([w_in[:, o_b:o_sz], w_in[:, o_dt:o_gate]], axis=1)

    hn, small = _norm_small(x2, meta, norm_w.reshape(1, D_MODEL), w_small)
    qkv = _matmul(hn, w_in[:, o_qkv:o_dnz].astype(BF16), BF16)
    dn_z = _matmul(hn, w_in[:, o_dnz:o_b].astype(BF16), BF16)
    ssd_z = _matmul(hn, w_in[:, o_sz:o_xbc].astype(BF16), BF16)
    xbc = _matmul(hn, w_in[:, o_xbc:o_dt].astype(BF16), BF16)
    gates = _matmul(hn, w_in[:, o_gate:d_in].astype(BF16), BF16)

    rows = hn.shape[0]
    sm_t = small.T
    dn_steps, dn_vps = DN_QK_HEADS // DN_QPS, 2 * DN_QPS
    b_rows = sm_t[0:DN_V_HEADS].reshape(dn_steps, dn_vps, rows)
    a_rows = sm_t[DN_V_HEADS:2 * DN_V_HEADS].reshape(dn_steps, dn_vps, rows)
    dt_rows = sm_t[2 * DN_V_HEADS:].reshape(SSD_GROUPS, SSD_REP, rows)

    bc = lambda p, g, r: jnp.broadcast_to(p.astype(F32).reshape(g, r, 1), (g, r, 128))
    y_dn = _gated_deltanet(qkv, dn_z, b_rows, a_rows, dn_conv_w,
                           bc(dn_a_log, dn_steps, dn_vps), bc(dn_dt_bias, dn_steps, dn_vps),
                           dn_norm_w.reshape(1, DN_HEAD_DIM))
    y_ssd = _mamba2_ssd(xbc, ssd_z, dt_rows, ssd_conv_w, ssd_conv_b.reshape(1, SSD_CONV_DIM),
                        bc(ssd_a_log, SSD_GROUPS, SSD_REP), bc(ssd_dt_bias, SSD_GROUPS, SSD_REP),
                        jnp.repeat(ssd_d, SSD_HEAD_DIM).reshape(1, SSD_D_INNER),
                        ssd_norm_w.reshape(1, SSD_D_INNER))
    return _merge(y_dn, y_ssd, gates, w_proj_dn.astype(BF16), w_proj_ssd.astype(BF16))


def _moe_and_final(merged, x2, w_out, norm_ffn_w, w_router, b_router, w_gate, b_gate, w_up, b_up,
                   w_down, b_down, final_norm_w):
    n_tok = x2.shape[0]
    h2, hn2, idx, probs, rank, cnt = _out_router(
        merged, w_out.astype(BF16), x2, norm_ffn_w.reshape(1, D_MODEL), w_router.T,
        b_router.reshape(N_EXPERTS, 1))

    counts = cnt[:, 0].astype(jnp.int32)
    padded = (counts + MOE_BM - 1) // MOE_BM * MOE_BM
    pad_end = jnp.cumsum(padded)
    pad_start = pad_end - padded
    hit = idx[:, :, None] == jnp.arange(N_EXPERTS, dtype=jnp.int32)
    dest = jnp.sum(jnp.where(hit, pad_start.astype(jnp.int32), 0), axis=-1) + rank
    nb = -(-(n_tok * TOP_K + N_EXPERTS * (MOE_BM - 1)) // MOE_BM)
    by_tile = lambda tt: dest.reshape(TOP_K, n_tok // tt, tt).transpose(1, 0, 2).reshape(-1)
    ztile = jnp.where(padded > 0, (pad_end - MOE_BM) // 8, -1).astype(jnp.int32)
    n_used = (pad_end[-1:] // MOE_BM).astype(jnp.int32)
    block_e = jnp.minimum(jnp.sum(pad_end[None, :] <= (jnp.arange(nb, dtype=jnp.int32) * MOE_BM)[:, None], axis=1),
                          N_EXPERTS - 1).astype(jnp.int32)
    ids = jnp.arange(N_EXPERTS, dtype=jnp.int32)
    later = (ids[None, :] > ids[:, None]) & (padded[None, :] > 0)
    next_e = jnp.min(jnp.where(later, ids[None, :], N_EXPERTS), axis=1)
    next_e = jnp.where(next_e < N_EXPERTS, next_e, -1).astype(jnp.int32)

    xs = _dispatch_rows(by_tile(DISP_TT), ztile, n_used, hn2, nb)
    act = _expert_up(block_e, next_e, n_used, xs, w_gate, w_up,
                     b_gate.reshape(N_EXPERTS, 1, D_MODEL), b_up.reshape(N_EXPERTS, 1, D_MODEL))
    y_rows = _expert_down(block_e, next_e, n_used, act, w_down, b_down.reshape(N_EXPERTS, 1, D_MODEL))
    return _combine(by_tile(COMB_TT), h2, probs.T, final_norm_w.reshape(1, D_MODEL), y_rows)


def kernel(x, meta_tokens, norm_mix_w, w_in, dn_conv_w, dn_a_log, dn_dt_bias, dn_norm_w, ssd_conv_w, ssd_conv_b, ssd_a_log, ssd_dt_bias, ssd_d, ssd_norm_w, w_proj_dn, w_proj_ssd, w_out, norm_ffn_w, w_router, b_router, w_gate, b_gate, w_up, b_up, w_down, b_down, final_norm_w):
    bsz, seq, d = x.shape
    assert bsz == 1 and d == D_MODEL and seq % TB == 0
    assert norm_mix_w.shape[0] == 1, "single-layer block"
    x2 = x.reshape(seq, d)
    merged = _mixer(x2, meta_tokens.astype(x.dtype), norm_mix_w[0], w_in[0], dn_conv_w[0], dn_a_log[0], dn_dt_bias[0], dn_norm_w[0],
                    ssd_conv_w[0], ssd_conv_b[0], ssd_a_log[0], ssd_dt_bias[0], ssd_d[0], ssd_norm_w[0],
                    w_proj_dn[0], w_proj_ssd[0])
    out = _moe_and_final(merged, x2, w_out[0], norm_ffn_w[0], w_router[0], b_router[0], w_gate[0], b_gate[0],
                         w_up[0], b_up[0], w_down[0], b_down[0], final_norm_w)
    return out.reshape(bsz, seq, d)
```

```python
import functools

import jax
import jax.numpy as jnp
from jax import lax
from jax.experimental import pallas as pl
from jax.experimental.pallas import tpu as pltpu

F32 = jnp.float32
BF16 = jnp.bfloat16

D_MODEL = 2048
N_META = 16
CHUNK = 64
EPS = 1e-6

DN_QK_HEADS = 16
DN_V_HEADS = 32
DN_HEAD_DIM = 128
DN_KEY_DIM = DN_QK_HEADS * DN_HEAD_DIM
DN_VAL_DIM = DN_V_HEADS * DN_HEAD_DIM
DN_CONV_DIM = 2 * DN_KEY_DIM + DN_VAL_DIM

SSD_D_INNER = 2 * D_MODEL
SSD_HEAD_DIM = 64
SSD_HEADS = SSD_D_INNER // SSD_HEAD_DIM
SSD_GROUPS = 8
SSD_REP = SSD_HEADS // SSD_GROUPS
SSD_STATE = 128
SSD_GS = SSD_GROUPS * SSD_STATE
SSD_CONV_DIM = SSD_D_INNER + 2 * SSD_GS
SSD_GROUP_DIM = SSD_D_INNER // SSD_GROUPS

N_EXPERTS = 32
TOP_K = 4
SWIGLU_LIMIT = 7.0
SWIGLU_ALPHA = 1.702

TB = 512
FRONT = TB
CPB = TB // CHUNK
MOE_BM = 256
MOE_TN_UP = 1024
MOE_TN_DOWN = 2048
COMB_TT = 256
DISP_TT = 512
ROUTER_RT = 512
DN_GROUP = 2
DN_QPS = 8
VMEM_LIMIT = 56 * 1024 * 1024


def _cparams(sem):
    return pltpu.CompilerParams(dimension_semantics=sem, vmem_limit_bytes=VMEM_LIMIT)


def _bdot(a, b):
    return jnp.dot(a.astype(BF16), b.astype(BF16), preferred_element_type=F32)


def _bdot_nt(a, b):
    return lax.dot_general(a.astype(BF16), b.astype(BF16), (((1,), (1,)), ((), ())),
                           preferred_element_type=F32)


def _bdot_tn(a, b):
    return lax.dot_general(a.astype(BF16), b.astype(BF16), (((0,), (0,)), ((), ())),
                           preferred_element_type=F32)


def _split3(x):
    x1 = x.astype(BF16)
    r1 = x - x1.astype(F32)
    x2 = r1.astype(BF16)
    x3 = (r1 - x2.astype(F32)).astype(BF16)
    return x1, x2, x3


def _dot_sel(x, m):
    x1, x2, x3 = _split3(x)
    d = lambda a: jnp.dot(a, m, preferred_element_type=F32)
    return d(x1) + d(x2) + d(x3)


def _dot3(a, b, nt=False):
    a1 = a.astype(BF16)
    a2 = (a - a1.astype(F32)).astype(BF16)
    b1 = b.astype(BF16)
    b2 = (b - b1.astype(F32)).astype(BF16)
    if nt:
        d = lambda p, q: lax.dot_general(p, q, (((1,), (1,)), ((), ())), preferred_element_type=F32)
    else:
        d = lambda p, q: jnp.dot(p, q, preferred_element_type=F32)
    return d(a1, b1) + d(a2, b1) + d(a1, b2)


def _sigmoid(x):
    return 1.0 / (1.0 + jnp.exp(-x))


def _silu(x):
    return x * _sigmoid(x)


def _softplus(x):
    return jnp.maximum(x, 0.0) + jnp.log(1.0 + jnp.exp(-jnp.abs(x)))


def _tile_lanes(p, n):
    return jnp.concatenate([p] * n, axis=1)


def _norm_small_kernel(x_ref, meta_ref, nw_ref, ws_ref, hn_ref, sm_ref):
    i = pl.program_id(0)

    def emit(h):
        y = h * lax.rsqrt(jnp.mean(h * h, axis=-1, keepdims=True) + EPS) * nw_ref[...]
        hn_ref[...] = y.astype(BF16)
        sm_ref[...] = _dot3(y, ws_ref[...])

    @pl.when(i == 0)
    def _():
        emit(jnp.concatenate([jnp.zeros((FRONT - N_META, D_MODEL), F32), meta_ref[...]], axis=0))

    @pl.when(i > 0)
    def _():
        emit(x_ref[...])


def _norm_small(x2, meta, nw, w_small):
    rows = FRONT + x2.shape[0]
    return pl.pallas_call(
        _norm_small_kernel,
        grid=(rows // TB,),
        in_specs=[pl.BlockSpec((TB, D_MODEL), lambda i: (jnp.maximum(i - 1, 0), 0)),
                  pl.BlockSpec((N_META, D_MODEL), lambda i: (0, 0)),
                  pl.BlockSpec((1, D_MODEL), lambda i: (0, 0)),
                  pl.BlockSpec((D_MODEL, 128), lambda i: (0, 0))],
        out_specs=[pl.BlockSpec((TB, D_MODEL), lambda i: (i, 0)),
                   pl.BlockSpec((TB, 128), lambda i: (i, 0))],
        out_shape=[jax.ShapeDtypeStruct((rows, D_MODEL), BF16),
                   jax.ShapeDtypeStruct((rows, 128), F32)],
        compiler_params=_cparams(("arbitrary",)),
    )(x2, meta, nw, w_small)


def _mm_kernel(x_ref, w_ref, o_ref):
    o_ref[...] = jnp.dot(x_ref[...], w_ref[...], preferred_element_type=F32).astype(o_ref.dtype)


def _matmul(x, w, out_dtype, tm=1024, tn=2048):
    m, k = x.shape
    n = w.shape[1]
    return pl.pallas_call(
        _mm_kernel,
        grid=(n // tn, pl.cdiv(m, tm)),
        in_specs=[pl.BlockSpec((tm, k), lambda j, i: (i, 0)),
                  pl.BlockSpec((k, tn), lambda j, i: (0, j))],
        out_specs=pl.BlockSpec((tm, tn), lambda j, i: (i, j)),
        out_shape=jax.ShapeDtypeStruct((m, n), out_dtype),
        compiler_params=_cparams(("arbitrary", "arbitrary")),
    )(x, w)


def _conv_silu(raw_ref, tail_ref, w_ref, b_ref=None):
    x = raw_ref[...].astype(F32)
    rows = x.shape[0]
    ext = jnp.concatenate([tail_ref[...], x], axis=0)
    w = w_ref[...]
    y = x * w[3:4]
    for j in (1, 2, 3):
        y = y + pltpu.roll(ext, j, 0)[8:8 + rows] * w[3 - j:4 - j]
    tail_ref[...] = x[rows - 8:rows]
    if b_ref is not None:
        y = y + b_ref[...]
    return _silu(y)


def _fill_chunk_matrices(u_ref, e_ref):
    jj = lax.broadcasted_iota(jnp.int32, (TB, TB), 0)
    ii = lax.broadcasted_iota(jnp.int32, (TB, TB), 1)
    same = (jj // CHUNK) == (ii // CHUNK)
    u_ref[...] = jnp.where(same & (jj <= ii), 1.0, 0.0).astype(BF16)
    e_ref[...] = jnp.where(same & (jj % CHUNK == CHUNK - 1), 1.0, 0.0).astype(BF16)


def _col_bcast(row, width):
    return jnp.broadcast_to(row, (width, TB)).T


def _dn_kernel(q_ref, k_ref, v_ref, z_ref, b_ref, a_ref, wq_ref, wk_ref, wv_ref, alog_ref, dtb_ref, nw_ref,
               o_ref, s_ref, tq_ref, tk_ref, tv_ref, u_ref, e_ref):
    t = pl.program_id(1)

    @pl.when(t == 0)
    def _():
        s_ref[...] = jnp.zeros_like(s_ref)
        tq_ref[...] = jnp.zeros_like(tq_ref)
        tk_ref[...] = jnp.zeros_like(tk_ref)
        tv_ref[...] = jnp.zeros_like(tv_ref)
        _fill_chunk_matrices(u_ref, e_ref)

    d = DN_HEAD_DIM
    nq = q_ref.shape[1] // d
    q_all = _conv_silu(q_ref, tq_ref, wq_ref)
    k_all = _conv_silu(k_ref, tk_ref, wk_ref)
    v = _conv_silu(v_ref, tv_ref, wv_ref)
    qs, ks = [], []
    for i in range(nq):
        q = q_all[:, i * d:(i + 1) * d]
        k = k_all[:, i * d:(i + 1) * d]
        qs.append(q * (lax.rsqrt(jnp.sum(q * q, axis=-1, keepdims=True) + EPS) * (d ** -0.5)))
        ks.append(k * lax.rsqrt(jnp.sum(k * k, axis=-1, keepdims=True) + EPS))

    beta = _sigmoid(b_ref[...])
    g = -jnp.exp(_tile_lanes(alog_ref[...], TB // 128)) * _softplus(a_ref[...] + _tile_lanes(dtb_ref[...], TB // 128))
    gc = _dot_sel(g, u_ref[...])
    gl = _dot_sel(gc, e_ref[...])
    eg = jnp.exp(gc)
    ek = jnp.exp(gl - gc)

    gw = DN_GROUP * CHUNK
    lg = CHUNK.bit_length() - 1
    ii = lax.broadcasted_iota(jnp.int32, (gw, gw), 0)
    jj = lax.broadcasted_iota(jnp.int32, (gw, gw), 1)
    same = (ii >> lg) == (jj >> lg)
    causal = same & (ii >= jj)
    strict = same & (ii > jj)
    eye = jnp.where(ii == jj, 1.0, 0.0)
    masks = [strict & ((ii >> 1) == (jj >> 1))]
    blk = 2
    while blk < CHUNK:
        sh = blk.bit_length() - 1
        masks.append(((ii >> (sh + 1)) == (jj >> (sh + 1))) & (((ii >> sh) & 1) == 1) & (((jj >> sh) & 1) == 0))
        blk *= 2

    heads = range(2 * nq)
    chunks = range(CPB)
    groups = range(TB // gw)
    rows_of = [slice(c * CHUNK, (c + 1) * CHUNK) for c in chunks]
    grows = [slice(s * gw, (s + 1) * gw) for s in groups]
    kk = [[_bdot_nt(ks[i][grows[s]], ks[i][grows[s]]) for s in groups] for i in range(nq)]
    qk = [[_bdot_nt(qs[i][grows[s]], ks[i][grows[s]]) for s in groups] for i in range(nq)]

    eg_cb, rhs_all, qd_all, kd_all = [], [], [], []
    dec, a = {}, {}
    for hh in heads:
        q, k = qs[hh // 2], ks[hh // 2]
        gc_cb = _col_bcast(gc[hh:hh + 1], max(gw, DN_HEAD_DIM))
        beta_cb = _col_bcast(beta[hh:hh + 1], max(gw, DN_HEAD_DIM))
        eg_cb.append(_col_bcast(eg[hh:hh + 1], DN_HEAD_DIM))
        ek_cb = _col_bcast(ek[hh:hh + 1], DN_HEAD_DIM)
        beta_h = beta_cb[:, :DN_HEAD_DIM]
        vh = v[:, hh * DN_HEAD_DIM:(hh + 1) * DN_HEAD_DIM]
        rhs_all.append(jnp.concatenate([vh * beta_h, k * (beta_h * eg_cb[hh])], axis=1))
        qd_all.append(q * eg_cb[hh])
        kd_all.append(k * ek_cb)
        for s in groups:
            diff = gc_cb[grows[s], :gw] - gc[hh:hh + 1, s * gw:(s + 1) * gw]
            dec[hh, s] = jnp.where(causal, jnp.exp(jnp.where(causal, diff, 0.0)), 0.0)
            a[hh, s] = jnp.where(strict, beta_cb[grows[s], :gw] * kk[hh // 2][s] * dec[hh, s], 0.0)
    inst = [(hh, s) for s in groups for hh in heads]

    tinv = {i: eye - jnp.where(masks[0], a[i], 0.0) for i in inst}
    for m in masks[1:]:
        x = {i: _bdot(jnp.where(m, a[i], 0.0), tinv[i]) for i in inst}
        tinv = {i: tinv[i] - _bdot(tinv[i], x[i]) for i in inst}
    uw_g = {(hh, s): _bdot(tinv[hh, s], rhs_all[hh][grows[s]]) for hh, s in inst}
    qo_g = {(hh, s): _bdot(qk[hh // 2][s] * dec[hh, s], uw_g[hh, s]) for hh, s in inst}
    in_group = [slice((c % DN_GROUP) * CHUNK, (c % DN_GROUP + 1) * CHUNK) for c in chunks]
    uw = {(hh, c): uw_g[hh, c // DN_GROUP][in_group[c]] for c in chunks for hh in heads}
    qo = {(hh, c): qo_g[hh, c // DN_GROUP][in_group[c]] for c in chunks for hh in heads}
    nk = {(hh, c): _bdot_tn(uw[hh, c], kd_all[hh][rows_of[c]]) for c in chunks for hh in heads}

    state_t = [s_ref[hh] for hh in heads]
    before = {}
    for c in chunks:
        for hh in heads:
            before[hh, c] = state_t[hh]
            g_end = eg_cb[hh][(c + 1) * CHUNK - 1:(c + 1) * CHUNK, :]
            state_t[hh] = (state_t[hh] * g_end - _bdot(state_t[hh], nk[hh, c][DN_HEAD_DIM:])
                           + nk[hh, c][:DN_HEAD_DIM])
    nw = nw_ref[...]
    for hh in heads:
        s_ref[hh] = state_t[hh]
        outs = [_bdot_nt(qd_all[hh][rows_of[c]] - qo[hh, c][:, DN_HEAD_DIM:], before[hh, c])
                + qo[hh, c][:, :DN_HEAD_DIM] for c in chunks]
        o = jnp.concatenate(outs, axis=0)
        zh = z_ref[:, hh * DN_HEAD_DIM:(hh + 1) * DN_HEAD_DIM].astype(F32)
        o = o * lax.rsqrt(jnp.mean(o * o, axis=-1, keepdims=True) + EPS) * nw * _silu(zh)
        o_ref[:, hh * DN_HEAD_DIM:(hh + 1) * DN_HEAD_DIM] = o.astype(o_ref.dtype)


def _gated_deltanet(qkv, z, b_rows, a_rows, conv_w, alog_b, dtb_b, norm_w):
    rows = qkv.shape[0]
    nstep = DN_QK_HEADS // DN_QPS
    qw = DN_QPS * DN_HEAD_DIM
    return pl.pallas_call(
        _dn_kernel,
        grid=(nstep, rows // TB),
        in_specs=[
            pl.BlockSpec((TB, qw), lambda j, t: (t, j)),
            pl.BlockSpec((TB, qw), lambda j, t: (t, nstep + j)),
            pl.BlockSpec((TB, 2 * qw), lambda j, t: (t, nstep + j)),
            pl.BlockSpec((TB, 2 * qw), lambda j, t: (t, j)),
            pl.BlockSpec((None, 2 * DN_QPS, TB), lambda j, t: (j, 0, t)),
            pl.BlockSpec((None, 2 * DN_QPS, TB), lambda j, t: (j, 0, t)),
            pl.BlockSpec((4, qw), lambda j, t: (0, j)),
            pl.BlockSpec((4, qw), lambda j, t: (0, nstep + j)),
            pl.BlockSpec((4, 2 * qw), lambda j, t: (0, nstep + j)),
            pl.BlockSpec((None, 2 * DN_QPS, 128), lambda j, t: (j, 0, 0)),
            pl.BlockSpec((None, 2 * DN_QPS, 128), lambda j, t: (j, 0, 0)),
            pl.BlockSpec((1, DN_HEAD_DIM), lambda j, t: (0, 0)),
        ],
        out_specs=pl.BlockSpec((TB, 2 * qw), lambda j, t: (t, j)),
        out_shape=jax.ShapeDtypeStruct((rows, DN_VAL_DIM), BF16),
        scratch_shapes=[
            pltpu.VMEM((2 * DN_QPS, DN_HEAD_DIM, DN_HEAD_DIM), F32),
            pltpu.VMEM((8, qw), F32),
            pltpu.VMEM((8, qw), F32),
            pltpu.VMEM((8, 2 * qw), F32),
            pltpu.VMEM((TB, TB), BF16),
            pltpu.VMEM((TB, TB), BF16),
        ],
        compiler_params=_cparams(("arbitrary", "arbitrary")),
    )(qkv, qkv, qkv, z, b_rows, a_rows, conv_w, conv_w, conv_w, alog_b, dtb_b, norm_w)


def _ssd_kernel(x_ref, b_ref, c_ref, z_ref, dt_ref, wx_ref, wb_ref, wc_ref, bx_ref, bb_ref, bc_ref,
                alog_ref, dtb_ref, dexp_ref, nw_ref, o_ref, h_ref, tx_ref, tb_ref, tc_ref, u_ref, e_ref):
    t = pl.program_id(1)

    @pl.when(t == 0)
    def _():
        h_ref[...] = jnp.zeros_like(h_ref)
        tx_ref[...] = jnp.zeros_like(tx_ref)
        tb_ref[...] = jnp.zeros_like(tb_ref)
        tc_ref[...] = jnp.zeros_like(tc_ref)
        _fill_chunk_matrices(u_ref, e_ref)

    xh = _conv_silu(x_ref, tx_ref, wx_ref, bx_ref)
    bm = _conv_silu(b_ref, tb_ref, wb_ref, bb_ref)
    cm = _conv_silu(c_ref, tc_ref, wc_ref, bc_ref)

    nrep = TB // 128
    dt = _softplus(dt_ref[...] + _tile_lanes(dtb_ref[...], nrep))
    pos = t * TB + lax.broadcasted_iota(jnp.int32, (SSD_REP, TB), 1)
    dt = jnp.where(pos >= FRONT - N_META, dt, 0.0)
    la = -jnp.exp(_tile_lanes(alog_ref[...], nrep)) * dt
    ac = _dot_sel(la, u_ref[...])
    al = _dot_sel(ac, e_ref[...])
    dtek = dt * jnp.exp(al - ac)

    def col_bcast_heads(rowsv):
        stacked = jnp.concatenate(
            [jnp.broadcast_to(rowsv[r:r + 1], (SSD_HEAD_DIM, TB)) for r in range(SSD_REP)], axis=0)
        return stacked.T

    ac_cb = col_bcast_heads(ac)
    ea_cb = jnp.exp(ac_cb)
    xd = xh * col_bcast_heads(dtek)

    ii = lax.broadcasted_iota(jnp.int32, (CHUNK, 2 * CHUNK), 0)
    jj = lax.broadcasted_iota(jnp.int32, (CHUNK, 2 * CHUNK), 1)
    causal2 = ii >= (jj & (CHUNK - 1))
    li = lax.broadcasted_iota(jnp.int32, (CHUNK, 2 * CHUNK), 1)
    lo_half = li < CHUNK

    state = h_ref[...]
    outs = []
    for c in range(CPB):
        rows = slice(c * CHUNK, (c + 1) * CHUNK)
        cc = cm[rows]
        bc = bm[rows]
        cb = _bdot_nt(cc, bc)
        cb2 = jnp.concatenate([cb, cb], axis=1)
        y_off = _bdot(cc, state) * ea_cb[rows]
        yd = []
        for m in range(SSD_REP // 2):
            lanes = slice(m * 128, (m + 1) * 128)
            rowv = jnp.concatenate([ac[2 * m:2 * m + 1, rows], ac[2 * m + 1:2 * m + 2, rows]], axis=1)
            dtrow = jnp.concatenate([dt[2 * m:2 * m + 1, rows], dt[2 * m + 1:2 * m + 2, rows]], axis=1)
            diff = ac_cb[rows, lanes] - rowv
            seg = jnp.where(causal2, jnp.exp(jnp.where(causal2, diff, 0.0)), 0.0)
            mm = cb2 * seg * dtrow
            x2 = xh[rows, lanes]
            rhs = jnp.concatenate([jnp.where(lo_half, x2, 0.0), jnp.where(lo_half, 0.0, x2)], axis=0)
            yd.append(_bdot(mm, rhs))
        outs.append(jnp.concatenate(yd, axis=1) + y_off)
        ea_last = ea_cb[(c + 1) * CHUNK - 1:(c + 1) * CHUNK, :]
        state = state * ea_last + _bdot_tn(bc, xd[rows])
    h_ref[...] = state

    y = jnp.concatenate(outs, axis=0) + dexp_ref[...] * xh
    y = y * _silu(z_ref[...].astype(F32))
    y = y * lax.rsqrt(jnp.mean(y * y, axis=-1, keepdims=True) + EPS) * nw_ref[...]
    o_ref[...] = y.astype(o_ref.dtype)


def _mamba2_ssd(xbc, z, dt_rows, conv_w, conv_b, alog_b, dtb_b, d_exp, norm_w):
    rows = xbc.shape[0]
    gd = SSD_GROUP_DIM
    nx = SSD_D_INNER // SSD_STATE
    return pl.pallas_call(
        _ssd_kernel,
        grid=(SSD_GROUPS, rows // TB),
        in_specs=[
            pl.BlockSpec((TB, gd), lambda g, t: (t, g)),
            pl.BlockSpec((TB, SSD_STATE), lambda g, t: (t, nx + g)),
            pl.BlockSpec((TB, SSD_STATE), lambda g, t: (t, nx + SSD_GROUPS + g)),
            pl.BlockSpec((TB, gd), lambda g, t: (t, g)),
            pl.BlockSpec((None, SSD_REP, TB), lambda g, t: (g, 0, t)),
            pl.BlockSpec((4, gd), lambda g, t: (0, g)),
            pl.BlockSpec((4, SSD_STATE), lambda g, t: (0, nx + g)),
            pl.BlockSpec((4, SSD_STATE), lambda g, t: (0, nx + SSD_GROUPS + g)),
            pl.BlockSpec((1, gd), lambda g, t: (0, g)),
            pl.BlockSpec((1, SSD_STATE), lambda g, t: (0, nx + g)),
            pl.BlockSpec((1, SSD_STATE), lambda g, t: (0, nx + SSD_GROUPS + g)),
            pl.BlockSpec((None, SSD_REP, 128), lambda g, t: (g, 0, 0)),
            pl.BlockSpec((None, SSD_REP, 128), lambda g, t: (g, 0, 0)),
            pl.BlockSpec((1, gd), lambda g, t: (0, g)),
            pl.BlockSpec((1, gd), lambda g, t: (0, g)),
        ],
        out_specs=pl.BlockSpec((TB, gd), lambda g, t: (t, g)),
        out_shape=jax.ShapeDtypeStruct((rows, SSD_D_INNER), BF16),
        scratch_shapes=[
            pltpu.VMEM((SSD_STATE, gd), F32),
            pltpu.VMEM((8, gd), F32),
            pltpu.VMEM((8, SSD_STATE), F32),
            pltpu.VMEM((8, SSD_STATE), F32),
            pltpu.VMEM((TB, TB), BF16),
            pltpu.VMEM((TB, TB), BF16),
        ],
        compiler_params=_cparams(("arbitrary", "arbitrary")),
    )(xbc, xbc, xbc, z, dt_rows, conv_w, conv_w, conv_w, conv_b, conv_b, conv_b, alog_b, dtb_b, d_exp, norm_w)


def _merge_kernel(ydn_ref, yssd_ref, gd_ref, gs_ref, wdn_ref, wssd_ref, o_ref):
    a = jnp.dot(ydn_ref[...], wdn_ref[...], preferred_element_type=F32)
    b = jnp.dot(yssd_ref[...], wssd_ref[...], preferred_element_type=F32)
    o = _sigmoid(gd_ref[...].astype(F32)) * a + _sigmoid(gs_ref[...].astype(F32)) * b
    o_ref[...] = o.astype(o_ref.dtype)


def _merge(y_dn, y_ssd, gates, w_dn, w_ssd, tn=512):
    ntile = (y_dn.shape[0] - FRONT) // TB
    off = FRONT // TB
    ng = D_MODEL // tn
    return pl.pallas_call(
        _merge_kernel,
        grid=(ng, ntile),
        in_specs=[
            pl.BlockSpec((TB, DN_VAL_DIM), lambda j, i: (i + off, 0)),
            pl.BlockSpec((TB, SSD_D_INNER), lambda j, i: (i + off, 0)),
            pl.BlockSpec((TB, tn), lambda j, i: (i + off, j)),
            pl.BlockSpec((TB, tn), lambda j, i: (i + off, ng + j)),
            pl.BlockSpec((DN_VAL_DIM, tn), lambda j, i: (0, j)),
            pl.BlockSpec((SSD_D_INNER, tn), lambda j, i: (0, j)),
        ],
        out_specs=pl.BlockSpec((TB, tn), lambda j, i: (i, j)),
        out_shape=jax.ShapeDtypeStruct((ntile * TB, D_MODEL), BF16),
        compiler_params=_cparams(("arbitrary", "arbitrary")),
    )(y_dn, y_ssd, gates, gates, w_dn, w_ssd)


def _out_router_kernel(m_ref, wo_ref, x_ref, nw_ref, wr_ref, br_ref,
                       h2_ref, hn_ref, idx_ref, p_ref, rank_ref, cnt_ref, carry_ref, us_ref):
    i = pl.program_id(0)
    rt = x_ref.shape[0]

    @pl.when(i == 0)
    def _():
        carry_ref[...] = jnp.zeros_like(carry_ref)
        jj = lax.broadcasted_iota(jnp.int32, (rt, rt), 0)
        ii = lax.broadcasted_iota(jnp.int32, (rt, rt), 1)
        us_ref[...] = jnp.where(jj < ii, 1.0, 0.0).astype(BF16)

    h2 = x_ref[...] + jnp.dot(m_ref[...], wo_ref[...], preferred_element_type=F32)
    h2_ref[...] = h2
    hn = h2 * lax.rsqrt(jnp.mean(h2 * h2, axis=-1, keepdims=True) + EPS) * nw_ref[...]
    hn_ref[...] = hn

    lg = _dot3(wr_ref[...], hn, nt=True) + br_ref[...]
    eio = lax.broadcasted_iota(jnp.int32, (N_EXPERTS, rt), 0)
    vals, hits, idxs = [], [], []
    cur = lg
    for _ in range(TOP_K):
        m = jnp.max(cur, axis=0, keepdims=True)
        ik = jnp.min(jnp.where(cur == m, eio, N_EXPERTS), axis=0, keepdims=True)
        hit = eio == ik
        vals.append(m)
        idxs.append(ik)
        hits.append(hit)
        cur = jnp.where(hit, -jnp.inf, cur)
    ex = [jnp.exp(vk - vals[0]) for vk in vals]
    den = ex[0] + ex[1] + ex[2] + ex[3]
    idx_ref[...] = jnp.concatenate(idxs, axis=0)
    p_ref[...] = jnp.concatenate([e / den for e in ex], axis=0)

    sel = jnp.zeros((N_EXPERTS, rt), F32)
    for hit in hits:
        sel = sel + jnp.where(hit, 1.0, 0.0)
    before = jnp.dot(sel.astype(BF16), us_ref[...], preferred_element_type=F32) \
        + _tile_lanes(carry_ref[...], rt // 128)
    ranks = [jnp.sum(jnp.where(hit, before, 0.0), axis=0, keepdims=True) for hit in hits]
    rank_ref[...] = jnp.concatenate(ranks, axis=0).astype(jnp.int32)
    carry = carry_ref[...] + jnp.dot(sel.astype(BF16), jnp.ones((rt, 128), BF16), preferred_element_type=F32)
    carry_ref[...] = carry
    cnt_ref[...] = carry


def _out_router(merged, w_out, x, nw, w_router_t, b_router_c):
    n_tok = x.shape[0]
    rt = ROUTER_RT
    tok_spec = pl.BlockSpec((TOP_K, rt), lambda i: (0, i))
    return pl.pallas_call(
        _out_router_kernel,
        grid=(n_tok // rt,),
        in_specs=[
            pl.BlockSpec((rt, D_MODEL), lambda i: (i, 0)),
            pl.BlockSpec((D_MODEL, D_MODEL), lambda i: (0, 0), pipeline_mode=pl.Buffered(1)),
            pl.BlockSpec((rt, D_MODEL), lambda i: (i, 0)),
            pl.BlockSpec((1, D_MODEL), lambda i: (0, 0)),
            pl.BlockSpec((N_EXPERTS, D_MODEL), lambda i: (0, 0)),
            pl.BlockSpec((N_EXPERTS, 1), lambda i: (0, 0)),
        ],
        out_specs=[
            pl.BlockSpec((rt, D_MODEL), lambda i: (i, 0)),
            pl.BlockSpec((rt, D_MODEL), lambda i: (i, 0)),
            tok_spec, tok_spec, tok_spec,
            pl.BlockSpec((N_EXPERTS, 128), lambda i: (0, 0)),
        ],
        out_shape=[
            jax.ShapeDtypeStruct((n_tok, D_MODEL), F32),
            jax.ShapeDtypeStruct((n_tok, D_MODEL), F32),
            jax.ShapeDtypeStruct((TOP_K, n_tok), jnp.int32),
            jax.ShapeDtypeStruct((TOP_K, n_tok), F32),
            jax.ShapeDtypeStruct((TOP_K, n_tok), jnp.int32),
            jax.ShapeDtypeStruct((N_EXPERTS, 128), F32),
        ],
        scratch_shapes=[pltpu.VMEM((N_EXPERTS, 128), F32), pltpu.VMEM((rt, rt), BF16)],
        compiler_params=_cparams(("arbitrary",)),
    )(merged, w_out, x, nw, w_router_t, b_router_c)


def _dispatch_kernel(dest_ref, ztile_ref, nused_ref, x_ref, o_ref, zbuf, sem, zsem):
    i = pl.program_id(0)
    bt = MOE_BM // 8

    @pl.when(i == 0)
    def _():
        zbuf[...] = jnp.zeros_like(zbuf)

        def zero_block(first_tile):
            cp = pltpu.make_async_copy(zbuf, o_ref.at[pl.ds(first_tile, bt)], zsem)
            cp.start()
            cp.wait()

        for e in range(N_EXPERTS):
            @pl.when(ztile_ref[e] >= 0)
            def _(e=e):
                zero_block(ztile_ref[e])

        def tail(b, carry):
            zero_block(b * bt)
            return carry

        lax.fori_loop(nused_ref[0], o_ref.shape[0] // bt, tail, 0)

    nt = x_ref.shape[0]
    base = i * (TOP_K * nt * 8)

    def body(g, carry):
        for s in range(8):
            for k in range(TOP_K):
                row = dest_ref[base + k * (nt * 8) + g * 8 + s]
                pltpu.make_async_copy(x_ref.at[g, pl.ds(s, 1)], o_ref.at[row >> 3, pl.ds(row & 7, 1)],
                                      sem).start(priority=(s * TOP_K + k) % 2)
        return carry

    lax.fori_loop(0, nt, body, 0)
    for k in range(TOP_K):
        pltpu.make_async_copy(x_ref, o_ref.at[pl.ds(0, nt)], sem).wait()


def _dispatch_rows(dest, ztile, n_used, src, nb):
    n_tok = src.shape[0]
    tiles = DISP_TT // 8
    out = pl.pallas_call(
        _dispatch_kernel,
        grid_spec=pltpu.PrefetchScalarGridSpec(
            num_scalar_prefetch=3,
            grid=(n_tok // DISP_TT,),
            in_specs=[pl.BlockSpec((tiles, 8, D_MODEL), lambda i, *_: (i, 0, 0))],
            out_specs=pl.BlockSpec(memory_space=pl.ANY),
            scratch_shapes=[pltpu.VMEM((MOE_BM // 8, 8, D_MODEL), F32),
                            pltpu.SemaphoreType.DMA, pltpu.SemaphoreType.DMA],
        ),
        out_shape=jax.ShapeDtypeStruct((nb * MOE_BM // 8, 8, D_MODEL), F32),
        compiler_params=_cparams(("arbitrary",)),
    )(dest, ztile, n_used, src.reshape(n_tok // 8, 8, D_MODEL))
    return out.reshape(nb * MOE_BM, D_MODEL)


def _expert_weights(e_ref, nxt_ref, nused_ref, run_ref, w_hbm, w_stage, w_bf16, wsem):
    j = pl.program_id(0)
    b = pl.program_id(1)
    tn = w_bf16[0].shape[1]
    e = e_ref[b]
    first = jnp.logical_and(b < nused_ref[0],
                            jnp.logical_or(b == 0, e != e_ref[jnp.maximum(b - 1, 0)]))

    def fetch(ee, jj, slot):
        cols = pl.ds(pl.multiple_of(jj * tn, tn), tn)
        return [pltpu.make_async_copy(w.at[ee, pl.ds(0, D_MODEL), cols], st.at[slot], wsem.at[slot])
                for w, st in zip(w_hbm, w_stage)]

    @pl.when(jnp.logical_and(j == 0, b == 0))
    def _():
        run_ref[0] = 0
        for cp in fetch(e, 0, 0):
            cp.start()

    @pl.when(first)
    def _():
        slot = run_ref[0] % 2
        for cp in fetch(e, j, slot):
            cp.wait()
        for st, wb in zip(w_stage, w_bf16):
            wb[...] = st[slot].astype(BF16)
        nxt = nxt_ref[e]
        same_tile = nxt >= 0

        @pl.when(jnp.logical_or(same_tile, j + 1 < pl.num_programs(0)))
        def _():
            for cp in fetch(jnp.where(same_tile, nxt, e_ref[0]), jnp.where(same_tile, j, j + 1), 1 - slot):
                cp.start()

        run_ref[0] = run_ref[0] + 1


def _expert_up_kernel(e_ref, nxt_ref, nused_ref, x_ref, wg_hbm, wu_hbm, bg_ref, bu_ref, o_ref,
                      wg_st, wu_st, wgb_ref, wub_ref, run_ref, wsem):
    b = pl.program_id(1)
    _expert_weights(e_ref, nxt_ref, nused_ref, run_ref, (wg_hbm, wu_hbm), (wg_st, wu_st),
                    (wgb_ref, wub_ref), wsem)

    @pl.when(b < nused_ref[0])
    def _():
        x = x_ref[...].astype(BF16)
        gt = jnp.dot(x, wgb_ref[...], preferred_element_type=F32) + bg_ref[...]
        up = jnp.dot(x, wub_ref[...], preferred_element_type=F32) + bu_ref[...]
        gt = jnp.minimum(gt, SWIGLU_LIMIT)
        up = jnp.clip(up, -SWIGLU_LIMIT, SWIGLU_LIMIT)
        o_ref[...] = (gt * _sigmoid(SWIGLU_ALPHA * gt) * (up + 1.0)).astype(o_ref.dtype)

    @pl.when(b >= nused_ref[0])
    def _():
        o_ref[...] = jnp.zeros_like(o_ref)


def _expert_up(block_e, next_e, n_used, xs, w_gate, w_up, b_gate, b_up):
    nb = xs.shape[0] // MOE_BM
    tn = MOE_TN_UP
    bspec = pl.BlockSpec((None, 1, tn), lambda j, b, be, *_: (be[b], 0, j))
    return pl.pallas_call(
        _expert_up_kernel,
        grid_spec=pltpu.PrefetchScalarGridSpec(
            num_scalar_prefetch=3,
            grid=(D_MODEL // tn, nb),
            in_specs=[pl.BlockSpec((MOE_BM, D_MODEL), lambda j, b, *_: (b, 0)),
                      pl.BlockSpec(memory_space=pl.ANY), pl.BlockSpec(memory_space=pl.ANY), bspec, bspec],
            out_specs=pl.BlockSpec((MOE_BM, tn), lambda j, b, *_: (b, j)),
            scratch_shapes=[pltpu.VMEM((2, D_MODEL, tn), F32), pltpu.VMEM((2, D_MODEL, tn), F32),
                            pltpu.VMEM((D_MODEL, tn), BF16), pltpu.VMEM((D_MODEL, tn), BF16),
                            pltpu.SMEM((1,), jnp.int32), pltpu.SemaphoreType.DMA((2,))],
        ),
        out_shape=jax.ShapeDtypeStruct(xs.shape, BF16),
        compiler_params=_cparams(("arbitrary", "arbitrary")),
    )(block_e, next_e, n_used, xs, w_gate, w_up, b_gate, b_up)


def _expert_down_kernel(e_ref, nxt_ref, nused_ref, a_ref, wd_hbm, bd_ref, o_ref, wd_st, wdb_ref, run_ref, wsem):
    b = pl.program_id(1)
    _expert_weights(e_ref, nxt_ref, nused_ref, run_ref, (wd_hbm,), (wd_st,), (wdb_ref,), wsem)

    @pl.when(b < nused_ref[0])
    def _():
        o_ref[...] = jnp.dot(a_ref[...], wdb_ref[...], preferred_element_type=F32) + bd_ref[...]

    @pl.when(b >= nused_ref[0])
    def _():
        o_ref[...] = jnp.zeros_like(o_ref)


def _expert_down(block_e, next_e, n_used, act, w_down, b_down):
    nb = act.shape[0] // MOE_BM
    tn = MOE_TN_DOWN
    return pl.pallas_call(
        _expert_down_kernel,
        grid_spec=pltpu.PrefetchScalarGridSpec(
            num_scalar_prefetch=3,
            grid=(D_MODEL // tn, nb),
            in_specs=[pl.BlockSpec((MOE_BM, D_MODEL), lambda j, b, *_: (b, 0)),
                      pl.BlockSpec(memory_space=pl.ANY),
                      pl.BlockSpec((None, 1, tn), lambda j, b, be, *_: (be[b], 0, j))],
            out_specs=pl.BlockSpec((MOE_BM, tn), lambda j, b, *_: (b, j)),
            scratch_shapes=[pltpu.VMEM((2, D_MODEL, tn), F32), pltpu.VMEM((D_MODEL, tn), BF16),
                            pltpu.SMEM((1,), jnp.int32), pltpu.SemaphoreType.DMA((2,))],
        ),
        out_shape=jax.ShapeDtypeStruct(act.shape, F32),
        compiler_params=_cparams(("arbitrary", "arbitrary")),
    )(block_e, next_e, n_used, act, w_down, b_down)


def _combine_kernel(dest_ref, h2_ref, p_ref, fw_ref, y_ref, o_ref, buf_ref, sem):
    i = pl.program_id(0)
    tt = COMB_TT

    def issue(tile, slot):
        base = tile * (TOP_K * tt)

        def body(g, carry):
            for s in range(8):
                for k in range(TOP_K):
                    row = dest_ref[base + k * tt + g * 8 + s]
                    pltpu.make_async_copy(y_ref.at[row >> 3, pl.ds(row & 7, 1)],
                                          buf_ref.at[slot, k, g, pl.ds(s, 1)],
                                          sem.at[slot]).start(priority=(s * TOP_K + k) % 2)
            return carry

        lax.fori_loop(0, tt // 8, body, 0)

    @pl.when(i == 0)
    def _():
        issue(0, 0)

    @pl.when(i + 1 < pl.num_programs(0))
    def _():
        issue(i + 1, (i + 1) % 2)

    slot = i % 2
    for k in range(TOP_K):
        pltpu.make_async_copy(y_ref.at[pl.ds(0, tt // 8)], buf_ref.at[slot, k], sem.at[slot]).wait()
    p = p_ref[...]
    acc = h2_ref[...]
    for k in range(TOP_K):
        acc = acc + p[:, k:k + 1] * buf_ref[slot, k].reshape(tt, D_MODEL)
    o_ref[...] = acc * lax.rsqrt(jnp.mean(acc * acc, axis=-1, keepdims=True) + EPS) * fw_ref[...]


def _combine(dest, h2, p_cols, final_w, y_rows):
    n_tok = h2.shape[0]
    tt = COMB_TT
    return pl.pallas_call(
        _combine_kernel,
        grid_spec=pltpu.PrefetchScalarGridSpec(
            num_scalar_prefetch=1,
            grid=(n_tok // tt,),
            in_specs=[
                pl.BlockSpec((tt, D_MODEL), lambda i, d: (i, 0)),
                pl.BlockSpec((tt, TOP_K), lambda i, d: (i, 0)),
                pl.BlockSpec((1, D_MODEL), lambda i, d: (0, 0)),
                pl.BlockSpec(memory_space=pl.ANY),
            ],
            out_specs=pl.BlockSpec((tt, D_MODEL), lambda i, d: (i, 0)),
            scratch_shapes=[pltpu.VMEM((2, TOP_K, tt // 8, 8, D_MODEL), F32), pltpu.SemaphoreType.DMA((2,))],
        ),
        out_shape=jax.ShapeDtypeStruct((n_tok, D_MODEL), F32),
        compiler_params=_cparams(("arbitrary",)),
    )(dest, h2, p_cols, final_w, y_rows.reshape(y_rows.shape[0] // 8, 8, D_MODEL))


def _mixer(x2, meta, norm_w, w_in, dn_conv_w, dn_a_log, dn_dt_bias, dn_norm_w,
           ssd_conv_w, ssd_conv_b, ssd_a_log, ssd_dt_bias, ssd_d, ssd_norm_w, w_proj_dn, w_proj_ssd):
    o_qkv = 0
    o_dnz = o_qkv + DN_CONV_DIM
    o_b = o_dnz + DN_VAL_DIM
    o_a = o_b + DN_V_HEADS
    o_sz = o_a + DN_V_HEADS
    o_xbc = o_sz + SSD_D_INNER
    o_dt = o_xbc + SSD_CONV_DIM
    o_gate = o_dt + SSD_HEADS
    d_in = o_gate + 2 * D_MODEL
    w_small = jnp.concatenate([w_in[:, o_b:o_sz], w_in[:, o_dt:o_gate]], axis=1)

    hn, small = _norm_small(x2, meta, norm_w.reshape(1, D_MODEL), w_small)
    qkv = _matmul(hn, w_in[:, o_qkv:o_dnz].astype(BF16), BF16)
    dn_z = _matmul(hn, w_in[:, o_dnz:o_b].astype(BF16), BF16)
    ssd_z = _matmul(hn, w_in[:, o_sz:o_xbc].astype(BF16), BF16)
    xbc = _matmul(hn, w_in[:, o_xbc:o_dt].astype(BF16), BF16)
    gates = _matmul(hn, w_in[:, o_gate:d_in].astype(BF16), BF16)

    rows = hn.shape[0]
    sm_t = small.T
    dn_steps, dn_vps = DN_QK_HEADS // DN_QPS, 2 * DN_QPS
    b_rows = sm_t[0:DN_V_HEADS].reshape(dn_steps, dn_vps, rows)
    a_rows = sm_t[DN_V_HEADS:2 * DN_V_HEADS].reshape(dn_steps, dn_vps, rows)
    dt_rows = sm_t[2 * DN_V_HEADS:].reshape(SSD_GROUPS, SSD_REP, rows)

    bc = lambda p, g, r: jnp.broadcast_to(p.astype(F32).reshape(g, r, 1), (g, r, 128))
    y_dn = _gated_deltanet(qkv, dn_z, b_rows, a_rows, dn_conv_w,
                           bc(dn_a_log, dn_steps, dn_vps), bc(dn_dt_bias, dn_steps, dn_vps),
                           dn_norm_w.reshape(1, DN_HEAD_DIM))
    y_ssd = _mamba2_ssd(xbc, ssd_z, dt_rows, ssd_conv_w, ssd_conv_b.reshape(1, SSD_CONV_DIM),
                        bc(ssd_a_log, SSD_GROUPS, SSD_REP), bc(ssd_dt_bias, SSD_GROUPS, SSD_REP),
                        jnp.repeat(ssd_d, SSD_HEAD_DIM).reshape(1, SSD_D_INNER),
                        ssd_norm_w.reshape(1, SSD_D_INNER))
    return _merge(y_dn, y_ssd, gates, w_proj_dn.astype(BF16), w_proj_ssd.astype(BF16))


def _moe_and_final(merged, x2, w_out, norm_ffn_w, w_router, b_router, w_gate, b_gate, w_up, b_up,
                   w_down, b_down, final_norm_w):
    n_tok = x2.shape[0]
    h2, hn2, idx, probs, rank, cnt = _out_router(
        merged, w_out.astype(BF16), x2, norm_ffn_w.reshape(1, D_MODEL), w_router.T,
        b_router.reshape(N_EXPERTS, 1))

    counts = cnt[:, 0].astype(jnp.int32)
    padded = (counts + MOE_BM - 1) // MOE_BM * MOE_BM
    pad_end = jnp.cumsum(padded)
    pad_start = pad_end - padded
    hit = idx[:, :, None] == jnp.arange(N_EXPERTS, dtype=jnp.int32)
    dest = jnp.sum(jnp.where(hit, pad_start.astype(jnp.int32), 0), axis=-1) + rank
    nb = -(-(n_tok * TOP_K + N_EXPERTS * (MOE_BM - 1)) // MOE_BM)
    by_tile = lambda tt: dest.reshape(TOP_K, n_tok // tt, tt).transpose(1, 0, 2).reshape(-1)
    ztile = jnp.where(padded > 0, (pad_end - MOE_BM) // 8, -1).astype(jnp.int32)
    n_used = (pad_end[-1:] // MOE_BM).astype(jnp.int32)
    block_e = jnp.minimum(jnp.sum(pad_end[None, :] <= (jnp.arange(nb, dtype=jnp.int32) * MOE_BM)[:, None], axis=1),
                          N_EXPERTS - 1).astype(jnp.int32)
    ids = jnp.arange(N_EXPERTS, dtype=jnp.int32)
    later = (ids[None, :] > ids[:, None]) & (padded[None, :] > 0)
    next_e = jnp.min(jnp.where(later, ids[None, :], N_EXPERTS), axis=1)
    next_e = jnp.where(next_e < N_EXPERTS, next_e, -1).astype(jnp.int32)

    xs = _dispatch_rows(by_tile(DISP_TT), ztile, n_used, hn2, nb)
    act = _expert_up(block_e, next_e, n_used, xs, w_gate, w_up,
                     b_gate.reshape(N_EXPERTS, 1, D_MODEL), b_up.reshape(N_EXPERTS, 1, D_MODEL))
    y_rows = _expert_down(block_e, next_e, n_used, act, w_down, b_down.reshape(N_EXPERTS, 1, D_MODEL))
    return _combine(by_tile(COMB_TT), h2, probs.T, final_norm_w.reshape(1, D_MODEL), y_rows)


def kernel(x, meta_tokens, norm_mix_w, w_in, dn_conv_w, dn_a_log, dn_dt_bias, dn_norm_w, ssd_conv_w, ssd_conv_b, ssd_a_log, ssd_dt_bias, ssd_d, ssd_norm_w, w_proj_dn, w_proj_ssd, w_out, norm_ffn_w, w_router, b_router, w_gate, b_gate, w_up, b_up, w_down, b_down, final_norm_w):
    bsz, seq, d = x.shape
    assert bsz == 1 and d == D_MODEL and seq % TB == 0
    assert norm_mix_w.shape[0] == 1, "single-layer block"
    x2 = x.reshape(seq, d)
    merged = _mixer(x2, meta_tokens.astype(x.dtype), norm_mix_w[0], w_in[0], dn_conv_w[0], dn_a_log[0], dn_dt_bias[0], dn_norm_w[0],
                    ssd_conv_w[0], ssd_conv_b[0], ssd_a_log[0], ssd_dt_bias[0], ssd_d[0], ssd_norm_w[0],
                    w_proj_dn[0], w_proj_ssd[0])
    out = _moe_and_final(merged, x2, w_out[0], norm_ffn_w[0], w_router[0], b_router[0], w_gate[0], b_gate[0],
                         w_up[0], b_up[0], w_down[0], b_down[0], final_norm_w)
    return out.reshape(bsz, seq, d)
```
